```python
import jax
import jax.numpy as jnp
from jax import lax
import numpy as np

D_MODEL = 1024
BATCH = 2
SEQ = 16384
DEPTH = 2
DEC_BATCH = 32
DEC_SEQ = 32
PAST_LEN = 2048

CHUNK = 64
N_MIXERS = 4
N_HEADS = 4
HEAD_DIM = 64
BRANCH_W = N_HEADS * HEAD_DIM
BAND_CHUNKS = 8
BAND_PAST = BAND_CHUNKS * CHUNK
REL_CLIP = 128
H_IDX = 4
D_IDX = 64
TOPK_MAX = 256
QBLOCK = 128
ROPE_THETA = 10000.0
D_FF = 2816
CONV_W = 3
EPS = 1e-6
IN_SIZES = (3 * BRANCH_W, 3 * BRANCH_W, 3 * BRANCH_W, H_IDX * D_IDX, D_IDX, H_IDX, 3 * BRANCH_W, N_HEADS, N_MIXERS * D_MODEL)
IN_COLS = 12 * BRANCH_W + H_IDX * D_IDX + D_IDX + H_IDX + N_HEADS + N_MIXERS * D_MODEL

kernel_name = 'hybrid_streaming_encoder_step'


def _rmsnorm(x, g):
    xf = x.astype(jnp.float32)
    y = xf * lax.rsqrt(jnp.mean(xf * xf, axis=-1, keepdims=True) + EPS)
    return (y * g.astype(jnp.float32)).astype(x.dtype)


def _heads(t):
    return t.reshape(t.shape[:-1] + (N_HEADS, HEAD_DIM))


def _rope(x, pos):
    half = x.shape[-1] // 2
    inv = ROPE_THETA ** (-jnp.arange(half, dtype=jnp.float32) / half)
    ang = pos.astype(jnp.float32)[:, None] * inv[None, :]
    cos = jnp.cos(ang)[:, None, :]
    sin = jnp.sin(ang)[:, None, :]
    xf = x.astype(jnp.float32)
    x1, x2 = xf[..., :half], xf[..., half:]
    return jnp.concatenate([x1 * cos - x2 * sin, x2 * cos + x1 * sin], axis=-1).astype(x.dtype)


def _query_blocks(fn, xs, lq):
    n_blocks = max(1, lq // QBLOCK)
    blk = lq // n_blocks

    def split(a):
        return jnp.moveaxis(a.reshape((a.shape[0], n_blocks, blk) + a.shape[2:]), 1, 0)

    offsets = jnp.arange(n_blocks, dtype=jnp.int32) * blk
    out = lax.map(fn, (offsets, tuple(split(a) for a in xs)))
    b, tail = out.shape[1], out.shape[3:]
    return jnp.moveaxis(out, 0, 1).reshape((b, lq) + tail)


def _mixer_a(q, k_all, v_all, q_start, k_start, rel_table):
    b, l, h, d = q.shape
    n_keys = k_all.shape[1]
    nc = -(-l // CHUNK)
    lp = nc * CHUNK
    band = BAND_PAST + CHUNK
    q_c = jnp.pad(q, ((0, 0), (0, lp - l), (0, 0), (0, 0))).reshape(b, nc, CHUNK, h, d)
    c_start = q_start + CHUNK * jnp.arange(nc)
    bpos = c_start[:, None] - BAND_PAST + jnp.arange(band)[None, :]
    bidx = bpos - k_start
    valid = (bidx >= 0) & (bidx < n_keys)
    bidx = jnp.clip(bidx, 0, n_keys - 1)
    kb = k_all[:, bidx]
    vb = v_all[:, bidx]
    rel = jnp.arange(CHUNK)[:, None] + BAND_PAST - jnp.arange(band)[None, :]
    bias = rel_table[jnp.clip(rel, -REL_CLIP, REL_CLIP) + REL_CLIP]
    logits = jnp.einsum('bcqhd,bckhd->bchqk', q_c, kb, preferred_element_type=jnp.float32) * (d ** -0.5)
    logits = logits + jnp.moveaxis(bias, -1, 0).astype(jnp.float32)[None, None]
    logits = jnp.where(valid[None, :, None, None, :], logits, -jnp.inf)
    p = jax.nn.softmax(logits, axis=-1).astype(v_all.dtype)
    o = jnp.einsum('bchqk,bckhd->bcqhd', p, vb).reshape(b, lp, h, d)
    return o[:, :l]


def _mixer_b(q, k_all, v_all, q_start):
    b, l, h, d = q.shape
    k_pos = jnp.arange(k_all.shape[1])

    def block(args):
        off, (qb,) = args
        q_pos = q_start + off + jnp.arange(qb.shape[1])
        z = jnp.einsum('bqhd,bkhd->bhqk', qb, k_all, preferred_element_type=jnp.float32) * (d ** -0.5)
        mask = k_pos[None, :] < q_pos[:, None]
        log_stay = jnp.where(mask, jax.nn.log_sigmoid(-z), 0.0)
        tail = lax.cumsum(log_stay, axis=3, reverse=True) - log_stay
        w = jnp.where(mask, jnp.exp(jax.nn.log_sigmoid(z) + tail), 0.0)
        return jnp.einsum('bhqk,bkhd->bqhd', w.astype(v_all.dtype), v_all)

    return _query_blocks(block, (q,), l)


def _mixer_c(q, k_all, v_all, iq, iw, ik_all, q_start):
    b, l, h, d = q.shape
    n_keys = k_all.shape[1]
    topk = min(TOPK_MAX, n_keys // 4)
    k_chunk = jnp.arange(n_keys) // CHUNK
    gather = jax.vmap(lambda rows, idx: rows[idx])

    def block(args):
        off, (qb, iqb, iwb) = args
        q_chunk = (q_start + off + jnp.arange(qb.shape[1])) // CHUNK
        adm = k_chunk[None, :] <= q_chunk[:, None]
        s = jax.nn.relu(jnp.einsum('bqgd,bkd->bqgk', iqb, ik_all, preferred_element_type=jnp.float32))
        score = jnp.einsum('bqg,bqgk->bqk', iwb.astype(jnp.float32), s) * (D_IDX ** -0.5)
        score = jnp.where(adm[None], score, -jnp.inf)
        _, idx = lax.top_k(score, topk)
        valid = (idx // CHUNK) <= q_chunk[None, :, None]
        kg = gather(k_all, idx)
        vg = gather(v_all, idx)
        logits = jnp.einsum('bqhd,bqkhd->bhqk', qb, kg, preferred_element_type=jnp.float32) * (d ** -0.5)
        logits = jnp.where(valid[:, None], logits, -jnp.inf)
        p = jax.nn.softmax(logits, axis=-1).astype(v_all.dtype)
        return jnp.einsum('bhqk,bqkhd->bqhd', p, vg)

    return _query_blocks(block, (q, iq, iw), l)


def _mixer_d(q, k_all, v_all, logf_all, q_start):
    b, l, h, d = q.shape
    k_pos = jnp.arange(k_all.shape[1])
    f_cum = jnp.cumsum(logf_all.astype(jnp.float32), axis=1)
    f_key = jnp.moveaxis(f_cum, 1, 2)[:, :, None, :]
    f_q = f_cum[:, q_start:]

    def block(args):
        off, (qb, fqb) = args
        q_pos = q_start + off + jnp.arange(qb.shape[1])
        logits = jnp.einsum('bqhd,bkhd->bhqk', qb, k_all, preferred_element_type=jnp.float32) * (d ** -0.5)
        logits = logits + jnp.moveaxis(fqb, 1, 2)[..., None] - f_key
        logits = jnp.where((k_pos[None, :] <= q_pos[:, None]), logits, -jnp.inf)
        p = jax.nn.softmax(logits, axis=-1).astype(v_all.dtype)
        return jnp.einsum('bhqk,bkhd->bqhd', p, v_all)

    return _query_blocks(block, (q, f_q), l)


def _layer(x, past, params):
    (pa_k, pa_v, pb_k, pb_v, pc_k, pc_v, pc_ki, pd_k, pd_v, pd_f, p_conv) = past
    (norm1_g, w_in, b_forget, qk_g, rel_table, w_branch, w_o,
     norm2_g, w_ffn_in, w_conv, b_conv, w_down) = params
    b, l, _ = x.shape
    q_start = pb_k.shape[1]
    a_past = pa_k.shape[1]
    pos = q_start + jnp.arange(l)

    h = _rmsnorm(x, norm1_g)
    points = [int(v) for v in np.cumsum(IN_SIZES)[:-1]]
    qkv_a, qkv_b, qkv_c, iq, ik, iw, qkv_d, f_lin, gate_lin = jnp.split(h @ w_in, points, axis=-1)

    qa, ka, va = (_heads(t) for t in jnp.split(qkv_a, 3, axis=-1))
    qa, ka = _rmsnorm(qa, qk_g[0]), _rmsnorm(ka, qk_g[1])
    qb, kb, vb = (_heads(t) for t in jnp.split(qkv_b, 3, axis=-1))
    qc, kc, vc = (_heads(t) for t in jnp.split(qkv_c, 3, axis=-1))
    qc, kc = _rope(_rmsnorm(qc, qk_g[2]), pos), _rope(_rmsnorm(kc, qk_g[3]), pos)
    iq = _rope(iq.reshape(b, l, H_IDX, D_IDX), pos)
    ik = _rope(ik[:, :, None, :], pos)[:, :, 0, :]
    iw = iw * (H_IDX ** -0.5)
    qd, kd, vd = (_heads(t) for t in jnp.split(qkv_d, 3, axis=-1))
    qd, kd = _rmsnorm(qd, qk_g[4]), _rmsnorm(kd, qk_g[5])
    logf = jax.nn.log_sigmoid(f_lin.astype(jnp.float32) + b_forget.astype(jnp.float32)).astype(x.dtype)

    ka_all, va_all = jnp.concatenate([pa_k, ka], axis=1), jnp.concatenate([pa_v, va], axis=1)
    kb_all, vb_all = jnp.concatenate([pb_k, kb], axis=1), jnp.concatenate([pb_v, vb], axis=1)
    kc_all, vc_all = jnp.concatenate([pc_k, kc], axis=1), jnp.concatenate([pc_v, vc], axis=1)
    ik_all = jnp.concatenate([pc_ki, ik], axis=1)
    kd_all, vd_all = jnp.concatenate([pd_k, kd], axis=1), jnp.concatenate([pd_v, vd], axis=1)
    logf_all = jnp.concatenate([pd_f, logf], axis=1)

    outs = (
        _mixer_a(qa, ka_all, va_all, q_start, q_start - a_past, rel_table),
        _mixer_b(qb, kb_all, vb_all, q_start),
        _mixer_c(qc, kc_all, vc_all, iq, iw, ik_all, q_start),
        _mixer_d(qd, kd_all, vd_all, logf_all, q_start),
    )
    gates = jax.nn.sigmoid(gate_lin.astype(jnp.float32)).astype(x.dtype).reshape(b, l, N_MIXERS, D_MODEL)
    merged = sum(gates[:, :, g] * (o.reshape(b, l, BRANCH_W) @ w_branch[g]) for g, o in enumerate(outs))
    x = x + merged @ w_o

    h2 = _rmsnorm(x, norm2_g)
    u, gt = jnp.split(h2 @ w_ffn_in, 2, axis=-1)
    u_all = jnp.concatenate([p_conv, u], axis=1)
    conv = b_conv + sum(u_all[:, j:j + l] * w_conv[j] for j in range(CONV_W))
    x = x + (jax.nn.gelu(conv) * gt) @ w_down

    a_keep = a_past if a_past > 0 else min(BAND_PAST, l)
    new = (ka_all[:, -a_keep:], va_all[:, -a_keep:], kb, vb, kc, vc, ik, kd, vd, logf,
           u_all[:, -(CONV_W - 1):])
    return x, new


def _empty_past(b, dtype):
    kv = jnp.zeros((b, 0, N_HEADS, HEAD_DIM), dtype)
    return (kv, kv, kv, kv, kv, kv, jnp.zeros((b, 0, D_IDX), dtype), kv, kv,
            jnp.zeros((b, 0, N_HEADS), dtype), jnp.zeros((b, CONV_W - 1, D_FF), dtype))


def setup_inputs(seed: int = 0) -> dict:
    key = jax.random.key(seed)
    ks = jax.random.split(key, 32)
    f32 = jnp.float32
    a_win = min(BAND_PAST, PAST_LEN)

    def nrm(k, shape, scale=1.0):
        return jax.random.normal(k, shape, f32) * scale

    kv_shape = (DEPTH, DEC_BATCH, PAST_LEN, N_HEADS, HEAD_DIM)
    return {
        'x_prompt': nrm(ks[0], (BATCH, SEQ, D_MODEL)),
        'x_sample': nrm(ks[1], (DEC_BATCH, DEC_SEQ, D_MODEL)),
        'cache_a_k': nrm(ks[2], (DEPTH, DEC_BATCH, a_win, N_HEADS, HEAD_DIM)),
        'cache_a_v': nrm(ks[3], (DEPTH, DEC_BATCH, a_win, N_HEADS, HEAD_DIM)),
        'cache_b_k': nrm(ks[4], kv_shape),
        'cache_b_v': nrm(ks[5], kv_shape),
        'cache_c_k': nrm(ks[6], kv_shape),
        'cache_c_v': nrm(ks[7], kv_shape),
        'cache_c_kidx': nrm(ks[8], (DEPTH, DEC_BATCH, PAST_LEN, D_IDX)),
        'cache_d_k': nrm(ks[9], kv_shape),
        'cache_d_v': nrm(ks[10], kv_shape),
        'cache_d_logf': jax.nn.log_sigmoid(3.0 + nrm(ks[11], (DEPTH, DEC_BATCH, PAST_LEN, N_HEADS))),
        'state_ffn_conv': nrm(ks[12], (DEPTH, DEC_BATCH, CONV_W - 1, D_FF)),
        'norm1_g': 1.0 + nrm(ks[13], (DEPTH, D_MODEL), 0.05),
        'w_in': nrm(ks[14], (DEPTH, D_MODEL, IN_COLS), D_MODEL ** -0.5),
        'b_forget': 3.0 + nrm(ks[15], (DEPTH, N_HEADS), 0.1),
        'qk_norm_g': 1.0 + nrm(ks[16], (DEPTH, 6, HEAD_DIM), 0.05),
        'rel_bias': nrm(ks[17], (DEPTH, 2 * REL_CLIP + 1, N_HEADS), 0.1),
        'w_branch': nrm(ks[18], (DEPTH, N_MIXERS, BRANCH_W, D_MODEL), BRANCH_W ** -0.5),
        'w_o': nrm(ks[19], (DEPTH, D_MODEL, D_MODEL), D_MODEL ** -0.5),
        'norm2_g': 1.0 + nrm(ks[20], (DEPTH, D_MODEL), 0.05),
        'w_ffn_in': nrm(ks[21], (DEPTH, D_MODEL, 2 * D_FF), D_MODEL ** -0.5),
        'w_conv': nrm(ks[22], (DEPTH, CONV_W, D_FF), CONV_W ** -0.5),
        'b_conv': nrm(ks[23], (DEPTH, D_FF), 0.02),
        'w_down': nrm(ks[24], (DEPTH, D_FF, D_MODEL), D_FF ** -0.5),
    }


def reference(x_prompt, x_sample, cache_a_k, cache_a_v, cache_b_k, cache_b_v, cache_c_k, cache_c_v,
              cache_c_kidx, cache_d_k, cache_d_v, cache_d_logf, state_ffn_conv,
              norm1_g, w_in, b_forget, qk_norm_g, rel_bias, w_branch, w_o, norm2_g,
              w_ffn_in, w_conv, b_conv, w_down):
    y_prompt, y_sample = x_prompt, x_sample
    p_states, s_states = [], []
    for l in range(DEPTH):
        params = (norm1_g[l], w_in[l], b_forget[l], qk_norm_g[l], rel_bias[l], w_branch[l], w_o[l],
                  norm2_g[l], w_ffn_in[l], w_conv[l], b_conv[l], w_down[l])
        y_prompt, st_p = _layer(y_prompt, _empty_past(x_prompt.shape[0], x_prompt.dtype), params)
        past = (cache_a_k[l], cache_a_v[l], cache_b_k[l], cache_b_v[l], cache_c_k[l], cache_c_v[l],
                cache_c_kidx[l], cache_d_k[l], cache_d_v[l], cache_d_logf[l], state_ffn_conv[l])
        y_sample, st_s = _layer(y_sample, past, params)
        p_states.append(st_p)
        s_states.append(st_s)
    (p_a_k, p_a_v, p_b_k, p_b_v, p_c_k, p_c_v, p_c_kidx, p_d_k, p_d_v, p_d_logf,
     p_ffn_conv) = [jnp.stack(t) for t in zip(*p_states)]
    (s_a_k, s_a_v, s_b_k, s_b_v, s_c_k, s_c_v, s_c_kidx, s_d_k, s_d_v, s_d_logf,
     s_ffn_conv) = [jnp.stack(t) for t in zip(*s_states)]
    return (y_prompt, y_sample,
            p_a_k, p_a_v, p_b_k, p_b_v, p_c_k, p_c_v, p_c_kidx, p_d_k, p_d_v, p_d_logf, p_ffn_conv,
            s_a_k, s_a_v, s_b_k, s_b_v, s_c_k, s_c_v, s_c_kidx, s_d_k, s_d_v, s_d_logf, s_ffn_conv)
```

```python
import functools

import numpy as np
import jax
import jax.numpy as jnp
from jax import lax
from jax.experimental import pallas as pl
from jax.experimental.pallas import tpu as pltpu

D_MODEL = 1024
CHUNK = 64
N_MIXERS = 4
N_HEADS = 4
HEAD_DIM = 64
BRANCH_W = N_HEADS * HEAD_DIM
BAND_PAST = 8 * CHUNK
REL_CLIP = 128
H_IDX = 4
D_IDX = 64
TOPK_MAX = 256
ROPE_THETA = 10000.0
D_FF = 2816
CONV_W = 3
EPS = 1e-6

BF = jnp.bfloat16
F32 = jnp.float32
NEG = -1e30
INT_MIN = -2 ** 31
LANES = 128
SUBLANES = 8
CTX_ROWS = 2 * SUBLANES
QKV_W = 3 * BRANCH_W
MISC_W = 3 * LANES
VMEM_LIMIT = 56 * 1024 * 1024


def _cparams(*sem):
    return pltpu.CompilerParams(dimension_semantics=sem, vmem_limit_bytes=VMEM_LIMIT)


def _dot(a, b):
    return jnp.dot(a, b, preferred_element_type=F32)


def _dot_nt(a, b):
    return lax.dot_general(a, b, (((1,), (1,)), ((), ())), preferred_element_type=F32)


def _split_dot(x, m, terms):
    acc = None
    r = x
    for t in range(terms):
        hi = r.astype(BF)
        d = _dot(hi, m)
        acc = d if acc is None else acc + d
        if t + 1 < terms:
            r = r - hi.astype(F32)
    return acc


def _rms(x, g):
    ms = jnp.mean(x * x, axis=-1, keepdims=True)
    return x * lax.rsqrt(ms + EPS) * g


def _head_rms(t, g, head_ones):
    ms = _split_dot(t * t, head_ones, 3) * (1.0 / HEAD_DIM)
    return t * lax.rsqrt(ms + EPS) * g


def _swap_halves(y):
    n = y.shape[-1]
    lane = lax.broadcasted_iota(jnp.int32, y.shape, 1)
    first = (lane & (HEAD_DIM - 1)) < (HEAD_DIM // 2)
    return jnp.where(first, pltpu.roll(y, n - HEAD_DIM // 2, 1), pltpu.roll(y, HEAD_DIM // 2, 1))


def _log1p_exp_neg_abs(z):
    return jnp.log1p(jnp.exp(-jnp.abs(z)))


def _proj_kernel(x_ref, g1_ref, wqkv_ref, wmisc_ref, qkg_ref, hones_ref, cq_ref, sq_ref, cm_ref, sm_ref,
                 bf_ref, q_ref, kbf_ref, vbf_ref, ka_ref, va_ref, kb_ref, vb_ref, kc_ref, vc_ref,
                 kd_ref, vd_ref, iq_ref, misc_ref):
    h = _rms(x_ref[...], g1_ref[...]).astype(BF)
    hones = hones_ref[...]
    k32 = (ka_ref, kb_ref, kc_ref, kd_ref)
    v32 = (va_ref, vb_ref, vc_ref, vd_ref)
    norm_row = (0, None, 2, 4)
    for m in range(N_MIXERS):
        y = _dot(h, wqkv_ref[:, m * QKV_W:(m + 1) * QKV_W])
        q, k, v = y[:, :BRANCH_W], y[:, BRANCH_W:2 * BRANCH_W], y[:, 2 * BRANCH_W:]
        if norm_row[m] is not None:
            r = norm_row[m]
            q = _head_rms(q, qkg_ref[r:r + 1, :], hones)
            k = _head_rms(k, qkg_ref[r + 1:r + 2, :], hones)
        if m == 2:
            c, s = cq_ref[...], sq_ref[...]
            q = q * c + _swap_halves(q) * s
            k = k * c + _swap_halves(k) * s
        cols = slice(m * BRANCH_W, (m + 1) * BRANCH_W)
        q_ref[:, cols] = (q * (HEAD_DIM ** -0.5)).astype(BF)
        kbf_ref[:, cols] = k.astype(BF)
        vbf_ref[:, cols] = v.astype(BF)
        k32[m][...] = k
        v32[m][...] = v
    ym = _dot(h, wmisc_ref[...])
    r = ym * cm_ref[...] + _swap_halves(ym) * sm_ref[...]
    iq_ref[...] = r[:, :H_IDX * D_IDX].astype(BF)
    g2 = r[:, H_IDX * D_IDX:]
    lane = lax.broadcasted_iota(jnp.int32, g2.shape, 1)
    z = g2 + bf_ref[...]
    logf = jnp.minimum(z, 0.0) - _log1p_exp_neg_abs(z)
    is_w = (lane >= D_IDX) & (lane < D_IDX + H_IDX)
    is_f = (lane >= D_IDX + H_IDX) & (lane < D_IDX + H_IDX + N_HEADS)
    misc_ref[...] = jnp.where(is_f, logf, jnp.where(is_w, g2 * (H_IDX ** -0.5 * D_IDX ** -0.5), g2))


def _projection(x2, lp, tabs, tm):
    m_rows = x2.shape[0]
    n_tab = tabs["cq"].shape[0] // tm
    row = lambda i: (i, 0)
    const = lambda i: (0, 0)
    tab = lambda i: (i % n_tab, 0)
    wide = pl.BlockSpec((tm, N_MIXERS * BRANCH_W), row)
    head = pl.BlockSpec((tm, BRANCH_W), row)
    in_specs = [
        pl.BlockSpec((tm, D_MODEL), row),
        pl.BlockSpec((1, D_MODEL), const),
        pl.BlockSpec((D_MODEL, N_MIXERS * QKV_W), const),
        pl.BlockSpec((D_MODEL, MISC_W), const),
        pl.BlockSpec((6, BRANCH_W), const),
        pl.BlockSpec((BRANCH_W, BRANCH_W), const),
        pl.BlockSpec((tm, BRANCH_W), tab),
        pl.BlockSpec((tm, BRANCH_W), tab),
        pl.BlockSpec((tm, MISC_W), tab),
        pl.BlockSpec((tm, MISC_W), tab),
        pl.BlockSpec((1, LANES), const),
    ]
    out_shape = ([jax.ShapeDtypeStruct((m_rows, N_MIXERS * BRANCH_W), BF)] * 3
                 + [jax.ShapeDtypeStruct((m_rows, BRANCH_W), F32)] * 8
                 + [jax.ShapeDtypeStruct((m_rows, BRANCH_W), BF),
                    jax.ShapeDtypeStruct((m_rows, LANES), F32)])
    out_specs = [wide] * 3 + [head] * 8 + [head, pl.BlockSpec((tm, LANES), row)]
    return pl.pallas_call(
        _proj_kernel, grid=(m_rows // tm,), in_specs=in_specs, out_specs=out_specs, out_shape=out_shape,
        compiler_params=_cparams("parallel"), name="projection",
    )(x2, lp["g1"], lp["w_qkv"], lp["w_misc"], lp["qkg"], lp["head_ones"],
      tabs["cq"], tabs["sq"], tabs["cm"], tabs["sm"], lp["bf_row"])


def _mixer_a_kernel(*refs, nwb, wb):
    q_ref = refs[0]
    k_refs = refs[1:1 + nwb]
    v_refs = refs[1 + nwb:1 + 2 * nwb]
    bias_ref = refs[1 + 2 * nwb]
    o_ref = refs[2 + 2 * nwb]
    i = pl.program_id(1)
    for h in range(N_HEADS):
        cols = slice(h * HEAD_DIM, (h + 1) * HEAD_DIM)
        qh = q_ref[0, :, cols]
        logits = []
        for r in range(nwb):
            s = _dot_nt(qh, k_refs[r][0, :, cols]) + bias_ref[h, :, r * wb:(r + 1) * wb]
            logits.append(jnp.where(i - (nwb - 1) + r >= 0, s, NEG))
        m = functools.reduce(jnp.maximum, [jnp.max(s, axis=1, keepdims=True) for s in logits])
        ps = [jnp.exp(s - m) for s in logits]
        l = functools.reduce(jnp.add, [jnp.sum(p, axis=1, keepdims=True) for p in ps])
        acc = functools.reduce(jnp.add, [_dot(p.astype(BF), v_refs[r][0, :, cols]) for r, p in enumerate(ps)])
        o_ref[0, :, cols] = (acc / l).astype(o_ref.dtype)


def _mixer_a(q_all, k_arr, v_arr, kv_col, bias, tq, nwb, wb):
    b, l, _ = q_all.shape
    kspec = lambda r: pl.BlockSpec((1, wb, BRANCH_W),
                                   lambda bi, i, r=r: (bi, jnp.maximum(i - (nwb - 1) + r, 0), kv_col))
    in_specs = ([pl.BlockSpec((1, tq, BRANCH_W), lambda bi, i: (bi, i, 0))]
                + [kspec(r) for r in range(nwb)] * 2
                + [pl.BlockSpec(bias.shape, lambda bi, i: (0, 0, 0))])
    return pl.pallas_call(
        functools.partial(_mixer_a_kernel, nwb=nwb, wb=wb),
        grid=(b, l // tq), in_specs=in_specs,
        out_specs=pl.BlockSpec((1, tq, BRANCH_W), lambda bi, i: (bi, i, 0)),
        out_shape=jax.ShapeDtypeStruct((b, l, BRANCH_W), BF),
        compiler_params=_cparams("parallel", "parallel"), name="mixer_a",
    )(q_all, *([k_arr] * nwb), *([v_arr] * nwb), bias)


def _band_bias(rel_table, tq, w, n_valid_cols):
    t = np.arange(tq)[:, None]
    c = np.arange(w)[None, :]
    krel = c - BAND_PAST
    ct = t // CHUNK
    inband = (krel >= CHUNK * ct - BAND_PAST) & (krel < CHUNK * ct + CHUNK) & (c < n_valid_cols)
    rel = np.clip(t - krel, -REL_CLIP, REL_CLIP) + REL_CLIP
    bias = jnp.where(inband[..., None], rel_table[rel].astype(F32), NEG)
    return jnp.moveaxis(bias, -1, 0)


def _mixer_b_kernel(q_ref, k_ref, v_ref, upper_ref, o_ref, *, tq, tk, q_start):
    q0 = q_start + pl.program_id(1) * tq
    nblk = (q0 + tq - 2) // tk + 1
    upper = upper_ref[...]
    qpos = q0 + lax.broadcasted_iota(jnp.int32, (tq, tk), 0)
    kiota = lax.broadcasted_iota(jnp.int32, (tq, tk), 1)
    for h in range(N_HEADS):
        cols = slice(h * HEAD_DIM, (h + 1) * HEAD_DIM)
        qh = q_ref[0, :, cols]

        def body(t, carry, qh=qh, cols=cols):
            acc, later = carry
            ks = pl.multiple_of((nblk - 1 - t) * tk, tk)
            z = _dot_nt(qh, k_ref[0, pl.ds(ks, tk), cols])
            l1p = _log1p_exp_neg_abs(z)
            mask = (ks + kiota) < qpos
            log_stay = jnp.where(mask, -(jnp.maximum(z, 0.0) + l1p), 0.0)
            tail = _split_dot(log_stay, upper, 2) + later
            w = jnp.where(mask, jnp.exp(jnp.minimum(z, 0.0) - l1p + tail), 0.0)
            acc = acc + _dot(w.astype(BF), v_ref[0, pl.ds(ks, tk), cols])
            return acc, later + jnp.sum(log_stay, axis=1, keepdims=True)

        acc, _ = lax.fori_loop(0, nblk, body, (jnp.zeros((tq, HEAD_DIM), F32), jnp.zeros((tq, 1), F32)))
        o_ref[0, :, cols] = acc.astype(o_ref.dtype)


def _kv_spec(kp, col):
    return pl.BlockSpec((1, kp, BRANCH_W), lambda bi, i: (bi, 0, col))


def _mixer_b(q_all, k_arr, v_arr, kv_col, tq, tk, q_start):
    b, l, _ = q_all.shape
    kp = k_arr.shape[1]
    upper = (np.arange(tk)[:, None] > np.arange(tk)[None, :]).astype(np.float32)
    return pl.pallas_call(
        functools.partial(_mixer_b_kernel, tq=tq, tk=tk, q_start=q_start),
        grid=(b, l // tq),
        in_specs=[pl.BlockSpec((1, tq, BRANCH_W), lambda bi, i: (bi, i, 1)), _kv_spec(kp, kv_col),
                  _kv_spec(kp, kv_col), pl.BlockSpec((tk, tk), lambda bi, i: (0, 0))],
        out_specs=pl.BlockSpec((1, tq, BRANCH_W), lambda bi, i: (bi, i, 0)),
        out_shape=jax.ShapeDtypeStruct((b, l, BRANCH_W), BF),
        compiler_params=_cparams("parallel", "parallel"), name="mixer_b",
    )(q_all, k_arr, v_arr, jnp.asarray(upper, BF))


def _mixer_c_kernel(q_ref, iq_ref, iw_ref, ik_ref, k_ref, v_ref, prefix_ref, o_ref, key_ref,
                    *, tq, tk, q_start, n_keys, topk):
    q0 = q_start + pl.program_id(1) * tq
    last_adm = jnp.minimum(((q0 + tq - 1) // CHUNK) * CHUNK + CHUNK - 1, n_keys - 1)
    nblk = last_adm // tk + 1
    qchunk = (q0 + lax.broadcasted_iota(jnp.int32, (tq, tk), 0)) // CHUNK
    kiota = lax.broadcasted_iota(jnp.int32, (tq, tk), 1)
    iw = iw_ref[0]
    topk_f = jnp.float32(topk)

    def score_body(j, carry):
        ks = pl.multiple_of(j * tk, tk)
        ikb = ik_ref[0, pl.ds(ks, tk), :]
        sc = None
        for g in range(H_IDX):
            s = _dot_nt(iq_ref[0, :, g * D_IDX:(g + 1) * D_IDX], ikb)
            term = iw[:, g:g + 1] * jnp.maximum(s, 0.0)
            sc = term if sc is None else sc + term
        sc = jnp.where(sc == 0.0, 0.0, sc)
        bits = pltpu.bitcast(sc, jnp.int32)
        key = bits ^ ((bits >> 31) & jnp.int32(0x7FFFFFFF))
        kpos = ks + kiota
        adm = ((kpos // CHUNK) <= qchunk) & (kpos < n_keys)
        key_ref[j] = jnp.where(adm, key, jnp.int32(INT_MIN))
        return carry

    lax.fori_loop(0, nblk, score_body, 0)

    def count(pred):
        def cb(j, acc):
            c = jnp.where(pred(key_ref[j]), 1.0, 0.0)
            return acc + functools.reduce(jnp.add, [c[:, s:s + LANES] for s in range(0, tk, LANES)])
        acc = lax.fori_loop(0, nblk, cb, jnp.zeros((tq, LANES), F32))
        return jnp.sum(acc, axis=1, keepdims=True)

    def bit_body(t, prefix):
        cand = prefix + jnp.left_shift(jnp.int32(1), 31 - t)
        cnt = count(lambda kb: kb >= cand)
        return jnp.where(cnt >= topk_f, cand, prefix)

    thr = lax.fori_loop(0, 32, bit_body, jnp.full((tq, 1), INT_MIN, jnp.int32))
    thr = jnp.maximum(thr, jnp.int32(INT_MIN + 1))
    n_ge = count(lambda kb: kb >= thr)
    need = topk_f - count(lambda kb: kb > thr)

    @pl.when(jnp.max(n_ge) > topk_f)
    def _():
        def fix(j, seen):
            kb = key_ref[j]
            eq = jnp.where(kb == thr, 1.0, 0.0)
            rank = _dot(eq.astype(BF), prefix_ref[...]) + seen
            key_ref[j] = jnp.where(eq * rank > need, jnp.int32(INT_MIN), kb)
            return seen + jnp.sum(eq, axis=1, keepdims=True)
        lax.fori_loop(0, nblk, fix, jnp.zeros((tq, 1), F32))

    for h in range(N_HEADS):
        cols = slice(h * HEAD_DIM, (h + 1) * HEAD_DIM)
        qh = q_ref[0, :, cols]

        def body(j, carry, qh=qh, cols=cols):
            m, l, acc = carry
            ks = pl.multiple_of(j * tk, tk)
            s = jnp.where(key_ref[j] >= thr, _dot_nt(qh, k_ref[0, pl.ds(ks, tk), cols]), NEG)
            m_new = jnp.maximum(m, jnp.max(s, axis=1, keepdims=True))
            alpha = jnp.exp(m - m_new)
            p = jnp.exp(s - m_new)
            l = alpha * l + jnp.sum(p, axis=1, keepdims=True)
            acc = alpha * acc + _dot(p.astype(BF), v_ref[0, pl.ds(ks, tk), cols])
            return m_new, l, acc

        _, l, acc = lax.fori_loop(0, nblk, body, (jnp.full((tq, 1), NEG, F32), jnp.zeros((tq, 1), F32),
                                                  jnp.zeros((tq, HEAD_DIM), F32)))
        o_ref[0, :, cols] = (acc / l).astype(o_ref.dtype)


def _mixer_c(q_all, iq, iw, ik_arr, k_arr, v_arr, kv_col, tq, tk, q_start, n_keys):
    b, l, _ = q_all.shape
    kp = k_arr.shape[1]
    topk = min(TOPK_MAX, n_keys // 4)
    prefix = (np.arange(tk)[:, None] <= np.arange(tk)[None, :]).astype(np.float32)
    return pl.pallas_call(
        functools.partial(_mixer_c_kernel, tq=tq, tk=tk, q_start=q_start, n_keys=n_keys, topk=topk),
        grid=(b, l // tq),
        in_specs=[pl.BlockSpec((1, tq, BRANCH_W), lambda bi, i: (bi, i, 2)),
                  pl.BlockSpec((1, tq, H_IDX * D_IDX), lambda bi, i: (bi, i, 0)),
                  pl.BlockSpec((1, tq, H_IDX), lambda bi, i: (bi, i, 0)),
                  pl.BlockSpec((1, kp, D_IDX), lambda bi, i: (bi, 0, 0)),
                  _kv_spec(kp, kv_col), _kv_spec(kp, kv_col),
                  pl.BlockSpec((tk, tk), lambda bi, i: (0, 0))],
        out_specs=pl.BlockSpec((1, tq, BRANCH_W), lambda bi, i: (bi, i, 0)),
        out_shape=jax.ShapeDtypeStruct((b, l, BRANCH_W), BF),
        scratch_shapes=[pltpu.VMEM((kp // tk, tq, tk), jnp.int32)],
        compiler_params=_cparams("parallel", "parallel"), name="mixer_c",
    )(q_all, iq, iw, ik_arr, k_arr, v_arr, jnp.asarray(prefix, BF))


def _mixer_d_kernel(q_ref, k_ref, v_ref, fq_ref, fk_ref, o_ref, *, tq, tk, q_start, n_kblk):
    q0 = q_start + pl.program_id(1) * tq
    nblk = (q0 + tq - 1) // tk + 1
    qpos = q0 + lax.broadcasted_iota(jnp.int32, (tq, tk), 0)
    kiota = lax.broadcasted_iota(jnp.int32, (tq, tk), 1)
    for h in range(N_HEADS):
        cols = slice(h * HEAD_DIM, (h + 1) * HEAD_DIM)
        qh = q_ref[0, :, cols]
        fq = fq_ref[0, :, h:h + 1]

        def body(j, carry, qh=qh, cols=cols, fq=fq, h=h):
            m, l, acc = carry
            ks = pl.multiple_of(j * tk, tk)
            s = _dot_nt(qh, k_ref[0, pl.ds(ks, tk), cols]) + fq - fk_ref[0, pl.ds(h * n_kblk + j, 1), :]
            s = jnp.where((ks + kiota) <= qpos, s, NEG)
            m_new = jnp.maximum(m, jnp.max(s, axis=1, keepdims=True))
            alpha = jnp.exp(m - m_new)
            p = jnp.exp(s - m_new)
            l = alpha * l + jnp.sum(p, axis=1, keepdims=True)
            acc = alpha * acc + _dot(p.astype(BF), v_ref[0, pl.ds(ks, tk), cols])
            return m_new, l, acc

        _, l, acc = lax.fori_loop(0, nblk, body, (jnp.full((tq, 1), NEG, F32), jnp.zeros((tq, 1), F32),
                                                  jnp.zeros((tq, HEAD_DIM), F32)))
        o_ref[0, :, cols] = (acc / l).astype(o_ref.dtype)


def _mixer_d(q_all, k_arr, v_arr, kv_col, fq, fk, tq, tk, q_start):
    b, l, _ = q_all.shape
    kp = k_arr.shape[1]
    n_kblk = kp // tk
    return pl.pallas_call(
        functools.partial(_mixer_d_kernel, tq=tq, tk=tk, q_start=q_start, n_kblk=n_kblk),
        grid=(b, l // tq),
        in_specs=[pl.BlockSpec((1, tq, BRANCH_W), lambda bi, i: (bi, i, 3)), _kv_spec(kp, kv_col),
                  _kv_spec(kp, kv_col),
                  pl.BlockSpec((1, tq, N_HEADS), lambda bi, i: (bi, i, 0)),
                  pl.BlockSpec((1, N_HEADS * n_kblk, tk), lambda bi, i: (bi, 0, 0))],
        out_specs=pl.BlockSpec((1, tq, BRANCH_W), lambda bi, i: (bi, i, 0)),
        out_shape=jax.ShapeDtypeStruct((b, l, BRANCH_W), BF),
        compiler_params=_cparams("parallel", "parallel"), name="mixer_d",
    )(q_all, k_arr, v_arr, fq, fk)


def _merge_kernel(x_ref, g1_ref, oa_ref, ob_ref, oc_ref, od_ref, wg_ref, wb_ref, wo_ref, y_ref):
    x = x_ref[...]
    h = _rms(x, g1_ref[...]).astype(BF)
    merged = None
    for g, o_ref in enumerate((oa_ref, ob_ref, oc_ref, od_ref)):
        gate = 1.0 / (1.0 + jnp.exp(-_dot(h, wg_ref[:, g * D_MODEL:(g + 1) * D_MODEL])))
        term = gate * _dot(o_ref[...], wb_ref[g])
        merged = term if merged is None else merged + term
    y_ref[...] = x + _dot(merged.astype(BF), wo_ref[...])


def _merge(x2, outs, lp, tm):
    m_rows = x2.shape[0]
    row = lambda i: (i, 0)
    const = lambda i: (0, 0)
    return pl.pallas_call(
        _merge_kernel, grid=(m_rows // tm,),
        in_specs=[pl.BlockSpec((tm, D_MODEL), row), pl.BlockSpec((1, D_MODEL), const)]
                 + [pl.BlockSpec((tm, BRANCH_W), row)] * N_MIXERS
                 + [pl.BlockSpec((D_MODEL, N_MIXERS * D_MODEL), const),
                    pl.BlockSpec((N_MIXERS, BRANCH_W, D_MODEL), lambda i: (0, 0, 0)),
                    pl.BlockSpec((D_MODEL, D_MODEL), const)],
        out_specs=pl.BlockSpec((tm, D_MODEL), row),
        out_shape=jax.ShapeDtypeStruct((m_rows, D_MODEL), F32),
        compiler_params=_cparams("parallel"), name="merge",
    )(x2, lp["g1"], *outs, lp["w_gate"], lp["w_branch"], lp["w_o"])


def _gelu_tanh(x):
    return x * (0.5 * (1.0 + jnp.tanh(np.sqrt(2.0 / np.pi).astype(np.float32) * (x + 0.044715 * (x * x * x)))))


def _ffn_kernel(x_ref, xp_ref, st_ref, g2_ref, wu_ref, wgt_ref, wc_ref, bc_ref, wd_ref, y_ref, ut_ref,
                h_ref, acc_ref, *, tm, tiles_per_seq):
    i = pl.program_id(0)
    f = pl.program_id(1)

    @pl.when(f == 0)
    def _():
        g2 = g2_ref[...]
        h_ref[:CTX_ROWS, :] = _rms(xp_ref[...], g2).astype(BF)
        h_ref[CTX_ROWS:, :] = _rms(x_ref[...], g2).astype(BF)
        acc_ref[...] = x_ref[...]

    h = h_ref[...]
    u = _dot(h, wu_ref[...])
    gt = _dot(h, wgt_ref[...])[CTX_ROWS:]
    ctx = jnp.where(i % tiles_per_seq == 0, st_ref[0], u[:CTX_ROWS])
    ue = jnp.concatenate([ctx, u[CTX_ROWS:]], axis=0)
    wc = wc_ref[...]
    conv = bc_ref[...] + ((ue[CTX_ROWS - 2:CTX_ROWS - 2 + tm] * wc[0:1] + ue[CTX_ROWS - 1:CTX_ROWS - 1 + tm] * wc[1:2])
                          + ue[CTX_ROWS:] * wc[2:3])
    a = _gelu_tanh(conv) * gt
    acc_ref[...] += _dot(a.astype(BF), wd_ref[...])
    ut_ref[0] = ue[tm:]

    @pl.when(f == pl.num_programs(1) - 1)
    def _():
        y_ref[...] = acc_ref[...]


def _ffn(x2, state8, lp, tm, tf, seq_len):
    m_rows = x2.shape[0]
    b = state8.shape[0]
    tiles_per_seq = seq_len // tm
    n_f = D_FF // tf
    return pl.pallas_call(
        functools.partial(_ffn_kernel, tm=tm, tiles_per_seq=tiles_per_seq),
        grid=(m_rows // tm, n_f),
        in_specs=[pl.BlockSpec((tm, D_MODEL), lambda i, f: (i, 0)),
                  pl.BlockSpec((CTX_ROWS, D_MODEL), lambda i, f: (jnp.maximum(i * (tm // CTX_ROWS) - 1, 0), 0)),
                  pl.BlockSpec((1, CTX_ROWS, tf), lambda i, f: (i // tiles_per_seq, 0, f)),
                  pl.BlockSpec((1, D_MODEL), lambda i, f: (0, 0)),
                  pl.BlockSpec((D_MODEL, tf), lambda i, f: (0, f)),
                  pl.BlockSpec((D_MODEL, tf), lambda i, f: (0, n_f + f)),
                  pl.BlockSpec((CONV_W, tf), lambda i, f: (0, f)),
                  pl.BlockSpec((1, tf), lambda i, f: (0, f)),
                  pl.BlockSpec((tf, D_MODEL), lambda i, f: (f, 0))],
        out_specs=[pl.BlockSpec((tm, D_MODEL), lambda i, f: (i, 0)),
                   pl.BlockSpec((1, CTX_ROWS, tf), lambda i, f: (i, 0, f))],
        out_shape=[jax.ShapeDtypeStruct((m_rows, D_MODEL), F32),
                   jax.ShapeDtypeStruct((m_rows // tm, CTX_ROWS, D_FF), F32)],
        scratch_shapes=[pltpu.VMEM((tm + CTX_ROWS, D_MODEL), BF), pltpu.VMEM((tm, D_MODEL), F32)],
        compiler_params=_cparams("parallel", "arbitrary"), name="ffn",
    )(x2, x2, state8, lp["g2"], lp["w_ffn_in"], lp["w_ffn_in"], lp["w_conv"], lp["b_conv"], lp["w_down"])


def _rope_tables(q_start, length):
    half = HEAD_DIM // 2
    inv = ROPE_THETA ** (-jnp.arange(half, dtype=F32) / half)
    ang = (q_start + jnp.arange(length)).astype(F32)[:, None] * inv[None, :]
    cos = jnp.cos(ang)
    sin = jnp.sin(ang)
    cos_h = jnp.concatenate([cos, cos], axis=1)
    sin_h = jnp.concatenate([-sin, sin], axis=1)
    n_rot = H_IDX + 1
    pad = MISC_W - n_rot * HEAD_DIM
    return {
        "cq": jnp.tile(cos_h, (1, N_HEADS)), "sq": jnp.tile(sin_h, (1, N_HEADS)),
        "cm": jnp.concatenate([jnp.tile(cos_h, (1, n_rot)), jnp.ones((length, pad), F32)], axis=1),
        "sm": jnp.concatenate([jnp.tile(sin_h, (1, n_rot)), jnp.zeros((length, pad), F32)], axis=1),
    }


def _layer_params(norm1_g, w_in, b_forget, qk_g, rel_table, w_branch, w_o, norm2_g, w_ffn_in, w_conv, b_conv,
                  w_down):
    a = 3 * QKV_W
    i1 = a + H_IDX * D_IDX + D_IDX + H_IDX
    d1 = i1 + QKV_W
    f1 = d1 + N_HEADS
    n_misc = (i1 - a) + N_HEADS
    w_misc = jnp.concatenate([w_in[:, a:i1], w_in[:, d1:f1], jnp.zeros((D_MODEL, MISC_W - n_misc), w_in.dtype)],
                             axis=1)
    f_lane = D_IDX + H_IDX
    bf_row = jnp.zeros((1, LANES), F32).at[0, f_lane:f_lane + N_HEADS].set(b_forget.astype(F32))
    head_ones = (np.arange(BRANCH_W)[:, None] // HEAD_DIM == np.arange(BRANCH_W)[None, :] // HEAD_DIM)
    return {
        "g1": norm1_g.astype(F32)[None, :], "g2": norm2_g.astype(F32)[None, :],
        "w_qkv": jnp.concatenate([w_in[:, :a], w_in[:, i1:d1]], axis=1).astype(BF),
        "w_misc": w_misc.astype(BF), "w_gate": w_in[:, f1:].astype(BF),
        "qkg": jnp.tile(qk_g.astype(F32), (1, N_HEADS)), "bf_row": bf_row,
        "head_ones": jnp.asarray(head_ones.astype(np.float32), BF),
        "rel_table": rel_table, "w_branch": w_branch.astype(BF), "w_o": w_o.astype(BF),
        "w_ffn_in": w_ffn_in.astype(BF), "w_conv": w_conv.astype(F32), "b_conv": b_conv.astype(F32)[None, :],
        "w_down": w_down.astype(BF),
    }


def _pad_rows(a, rows):
    return jnp.pad(a, ((0, 0), (0, rows - a.shape[1])) + ((0, 0),) * (a.ndim - 2))


def _round_up(n, m):
    return -(-n // m) * m


def _layer(x, past, lp, tabs):
    b, l, _ = x.shape
    m_rows = b * l
    q_start = 0 if past is None else past[2].shape[1]
    n_keys = q_start + l
    tq = min(256, l)
    tq_c = min(128, l)
    tk = 256
    kp = _round_up(n_keys, tk)
    tm = min(256, m_rows)

    x2 = x.reshape(m_rows, D_MODEL)
    (q_all, kbf, vbf, ka, va, kb, vb, kc, vc, kd, vd, iq, misc) = _projection(x2, lp, tabs, tm)
    ik = misc[:, :D_IDX].reshape(b, l, D_IDX)
    iw = misc[:, D_IDX:D_IDX + H_IDX].reshape(b, l, H_IDX)
    logf = misc[:, D_IDX + H_IDX:D_IDX + H_IDX + N_HEADS].reshape(b, l, N_HEADS)
    q_all = q_all.reshape(b, l, N_MIXERS * BRANCH_W)
    iq = iq.reshape(b, l, H_IDX * D_IDX)
    heads = lambda t: t.reshape(b, l, N_HEADS, HEAD_DIM)
    flat = lambda t: t.reshape(t.shape[0], t.shape[1], BRANCH_W)

    if past is None:
        kbf3 = kbf.reshape(b, l, N_MIXERS * BRANCH_W)
        vbf3 = vbf.reshape(b, l, N_MIXERS * BRANCH_W)
        kv = [(kbf3, vbf3, c) for c in range(N_MIXERS)]
        ik_all = ik.astype(BF)
        logf_all = logf
        a_args = (kbf3, vbf3, 0, _band_bias(lp["rel_table"], tq, BAND_PAST + tq, BAND_PAST + tq), tq, 3, tq)
        conv_state = jnp.zeros((b, CONV_W - 1, D_FF), F32)
    else:
        (pa_k, pa_v, pb_k, pb_v, pc_k, pc_v, pc_ki, pd_k, pd_v, pd_f, conv_state) = past
        new_bf = lambda t, c: t.reshape(b, l, N_MIXERS, BRANCH_W)[:, :, c]
        cat = lambda p, c, src, rows: _pad_rows(jnp.concatenate([flat(p).astype(BF), new_bf(src, c)], axis=1), rows)
        kv = [None] + [(cat(pk, c, kbf, kp), cat(pv, c, vbf, kp), 0)
                       for c, (pk, pv) in ((1, (pb_k, pb_v)), (2, (pc_k, pc_v)), (3, (pd_k, pd_v)))]
        ik_all = jnp.concatenate([pc_ki.astype(BF), ik.astype(BF)], axis=1)
        logf_all = jnp.concatenate([pd_f.astype(F32), logf], axis=1)
        a_keys = pa_k.shape[1] + l
        wa = _round_up(BAND_PAST + l, LANES)
        a_args = (cat(pa_k, 0, kbf, wa), cat(pa_v, 0, vbf, wa), 0,
                  _band_bias(lp["rel_table"], l, wa, a_keys), l, 1, wa)

    f_cum = jnp.cumsum(logf_all, axis=1)
    fq = f_cum[:, q_start:]
    fk = jnp.moveaxis(_pad_rows(f_cum, kp), 1, 2).reshape(b, N_HEADS * (kp // tk), tk)

    o_a = _mixer_a(q_all, *a_args)
    o_b = _mixer_b(q_all, *kv[1], tq, tk, q_start)
    o_c = _mixer_c(q_all, iq, iw, _pad_rows(ik_all, kp), *kv[2], tq_c, tk, q_start, n_keys)
    o_d = _mixer_d(q_all, *kv[3], fq, fk, tq, tk, q_start)
    x2 = _merge(x2, [o.reshape(m_rows, BRANCH_W) for o in (o_a, o_b, o_c, o_d)], lp, tm)

    state8 = jnp.pad(conv_state.astype(F32), ((0, 0), (CTX_ROWS - (CONV_W - 1), 0), (0, 0)))
    x2, u_tail = _ffn(x2, state8, lp, min(512, l), 256, l)
    u_tail = u_tail.reshape(b, -1, CTX_ROWS, D_FF)[:, -1]

    if past is None:
        a_keep = min(BAND_PAST, l)
        new_a = (heads(ka)[:, -a_keep:], heads(va)[:, -a_keep:])
    else:
        a_keep = pa_k.shape[1]
        new_a = (jnp.concatenate([pa_k, heads(ka)], axis=1)[:, -a_keep:],
                 jnp.concatenate([pa_v, heads(va)], axis=1)[:, -a_keep:])
    new = new_a + (heads(kb), heads(vb), heads(kc), heads(vc), ik, heads(kd), heads(vd), logf,
                   u_tail[:, -(CONV_W - 1):])
    return x2.reshape(b, l, D_MODEL), new


def kernel(x_prompt, x_sample, cache_a_k, cache_a_v, cache_b_k, cache_b_v, cache_c_k, cache_c_v, cache_c_kidx, cache_d_k, cache_d_v, cache_d_logf, state_ffn_conv, norm1_g, w_in, b_forget, qk_norm_g, rel_bias, w_branch, w_o, norm2_g, w_ffn_in, w_conv, b_conv, w_down):
    depth = w_in.shape[0]
    past_len = cache_b_k.shape[2]
    tabs_p = _rope_tables(0, x_prompt.shape[1])
    tabs_s = {k: jnp.tile(v, (x_sample.shape[0], 1)) for k, v in _rope_tables(past_len, x_sample.shape[1]).items()}
    y_p, y_s = x_prompt, x_sample
    p_states, s_states = [], []
    for d in range(depth):
        lp = _layer_params(norm1_g[d], w_in[d], b_forget[d], qk_norm_g[d], rel_bias[d], w_branch[d], w_o[d],
                           norm2_g[d], w_ffn_in[d], w_conv[d], b_conv[d], w_down[d])
        y_p, st_p = _layer(y_p, None, lp, tabs_p)
        past = (cache_a_k[d], cache_a_v[d], cache_b_k[d], cache_b_v[d], cache_c_k[d], cache_c_v[d],
                cache_c_kidx[d], cache_d_k[d], cache_d_v[d], cache_d_logf[d], state_ffn_conv[d])
        y_s, st_s = _layer(y_s, past, lp, tabs_s)
        p_states.append(st_p)
        s_states.append(st_s)
    p_out = [jnp.stack(t) for t in zip(*p_states)]
    s_out = [jnp.stack(t) for t in zip(*s_states)]
    return (y_p, y_s, *p_out, *s_out)
```

```python
import functools

import numpy as np
import jax
import jax.numpy as jnp
from jax import lax
from jax.experimental import pallas as pl
from jax.experimental.pallas import tpu as pltpu

D_MODEL = 1024
CHUNK = 64
N_MIXERS = 4
N_HEADS = 4
HEAD_DIM = 64
BRANCH_W = N_HEADS * HEAD_DIM
BAND_PAST = 8 * CHUNK
REL_CLIP = 128
H_IDX = 4
D_IDX = 64
TOPK_MAX = 256
ROPE_THETA = 10000.0
D_FF = 2816
CONV_W = 3
EPS = 1e-6

BF = jnp.bfloat16
F32 = jnp.float32
NEG = -1e30
LOG2E = float(np.log2(np.e))
INT_MIN = -2 ** 31
LANES = 128
SUBLANES = 8
CTX_ROWS = 2 * SUBLANES
QKV_W = 3 * BRANCH_W
MISC_W = 3 * LANES
VMEM_LIMIT = 56 * 1024 * 1024


def _cparams(*sem):
    return pltpu.CompilerParams(dimension_semantics=sem, vmem_limit_bytes=VMEM_LIMIT)


def _dot(a, b):
    return jnp.dot(a, b, preferred_element_type=F32)


def _dot_nt(a, b):
    return lax.dot_general(a, b, (((1,), (1,)), ((), ())), preferred_element_type=F32)


def _split_dot(x, m, terms):
    acc = None
    r = x
    for t in range(terms):
        hi = r.astype(BF)
        d = _dot(hi, m)
        acc = d if acc is None else acc + d
        if t + 1 < terms:
            r = r - hi.astype(F32)
    return acc


def _rms(x, g):
    ms = jnp.mean(x * x, axis=-1, keepdims=True)
    return x * lax.rsqrt(ms + EPS) * g


def _head_rms(t, g, head_ones):
    ms = _split_dot(t * t, head_ones, 3) * (1.0 / HEAD_DIM)
    return t * lax.rsqrt(ms + EPS) * g


def _swap_halves(y):
    n = y.shape[-1]
    lane = lax.broadcasted_iota(jnp.int32, y.shape, 1)
    first = (lane & (HEAD_DIM - 1)) < (HEAD_DIM // 2)
    return jnp.where(first, pltpu.roll(y, n - HEAD_DIM // 2, 1), pltpu.roll(y, HEAD_DIM // 2, 1))


def _softplus2(z):
    neg_abs = pltpu.bitcast(pltpu.bitcast(z, jnp.int32) | jnp.int32(INT_MIN), F32)
    return jnp.maximum(z, 0.0) + jnp.log2(1.0 + jnp.exp2(neg_abs))


def _low_half(shape):
    return lax.broadcasted_iota(jnp.int32, shape, 1) < HEAD_DIM


def _masked_pair_heads(ref):
    out = []
    for pr in range(ref.shape[-1] // LANES):
        slab = ref[0, :, pr * LANES:(pr + 1) * LANES].astype(F32)
        low = _low_half(slab.shape)
        out.append(jnp.where(low, slab, 0.0).astype(BF))
        out.append(jnp.where(low, 0.0, slab).astype(BF))
    return out


def _proj_kernel(x_ref, g1_ref, wqkv_ref, wmisc_ref, qkg_ref, hones_ref, cq_ref, sq_ref, cm_ref, sm_ref,
                 bf_ref, q_ref, kbf_ref, vbf_ref, ka_ref, va_ref, kb_ref, vb_ref, kc_ref, vc_ref,
                 kd_ref, vd_ref, iq_ref, misc_ref):
    h = _rms(x_ref[...], g1_ref[...]).astype(BF)
    hones = hones_ref[...]
    k32 = (ka_ref, kb_ref, kc_ref, kd_ref)
    v32 = (va_ref, vb_ref, vc_ref, vd_ref)
    norm_row = (0, None, 2, 4)
    for m in range(N_MIXERS):
        y = _dot(h, wqkv_ref[:, m * QKV_W:(m + 1) * QKV_W])
        q, k, v = y[:, :BRANCH_W], y[:, BRANCH_W:2 * BRANCH_W], y[:, 2 * BRANCH_W:]
        if norm_row[m] is not None:
            r = norm_row[m]
            q = _head_rms(q, qkg_ref[r:r + 1, :], hones)
            k = _head_rms(k, qkg_ref[r + 1:r + 2, :], hones)
        if m == 2:
            c, s = cq_ref[...], sq_ref[...]
            q = q * c + _swap_halves(q) * s
            k = k * c + _swap_halves(k) * s
        cols = slice(m * BRANCH_W, (m + 1) * BRANCH_W)
        q_ref[:, cols] = (q * (HEAD_DIM ** -0.5 * LOG2E)).astype(BF)
        kbf_ref[:, cols] = k.astype(BF)
        vbf_ref[:, cols] = v.astype(BF)
        k32[m][...] = k
        v32[m][...] = v
    ym = _dot(h, wmisc_ref[...])
    r = ym * cm_ref[...] + _swap_halves(ym) * sm_ref[...]
    iq_ref[...] = r[:, :H_IDX * D_IDX].astype(BF)
    g2 = r[:, H_IDX * D_IDX:]
    lane = lax.broadcasted_iota(jnp.int32, g2.shape, 1)
    z = g2 + bf_ref[...]
    logf = jnp.minimum(z, 0.0) - jnp.log1p(jnp.exp(-jnp.abs(z)))
    is_w = (lane >= D_IDX) & (lane < D_IDX + H_IDX)
    is_f = (lane >= D_IDX + H_IDX) & (lane < D_IDX + H_IDX + N_HEADS)
    misc_ref[...] = jnp.where(is_f, logf, jnp.where(is_w, g2 * (H_IDX ** -0.5 * D_IDX ** -0.5), g2))


def _projection(x2, lp, tabs, tm):
    m_rows = x2.shape[0]
    n_tab = tabs["cq"].shape[0] // tm
    row = lambda i: (i, 0)
    const = lambda i: (0, 0)
    tab = lambda i: (i % n_tab, 0)
    wide = pl.BlockSpec((tm, N_MIXERS * BRANCH_W), row)
    head = pl.BlockSpec((tm, BRANCH_W), row)
    in_specs = [
        pl.BlockSpec((tm, D_MODEL), row),
        pl.BlockSpec((1, D_MODEL), const),
        pl.BlockSpec((D_MODEL, N_MIXERS * QKV_W), const),
        pl.BlockSpec((D_MODEL, MISC_W), const),
        pl.BlockSpec((6, BRANCH_W), const),
        pl.BlockSpec((BRANCH_W, BRANCH_W), const),
        pl.BlockSpec((tm, BRANCH_W), tab),
        pl.BlockSpec((tm, BRANCH_W), tab),
        pl.BlockSpec((tm, MISC_W), tab),
        pl.BlockSpec((tm, MISC_W), tab),
        pl.BlockSpec((1, LANES), const),
    ]
    out_shape = ([jax.ShapeDtypeStruct((m_rows, N_MIXERS * BRANCH_W), BF)] * 3
                 + [jax.ShapeDtypeStruct((m_rows, BRANCH_W), F32)] * 8
                 + [jax.ShapeDtypeStruct((m_rows, BRANCH_W), BF),
                    jax.ShapeDtypeStruct((m_rows, LANES), F32)])
    out_specs = [wide] * 3 + [head] * 8 + [head, pl.BlockSpec((tm, LANES), row)]
    return pl.pallas_call(
        _proj_kernel, grid=(m_rows // tm,), in_specs=in_specs, out_specs=out_specs, out_shape=out_shape,
        compiler_params=_cparams("parallel"), name="projection",
    )(x2, lp["g1"], lp["w_qkv"], lp["w_misc"], lp["qkg"], lp["head_ones"],
      tabs["cq"], tabs["sq"], tabs["cm"], tabs["sm"], lp["bf_row"])


def _mixer_a_kernel(*refs, nwb, wb):
    q_ref = refs[0]
    k_refs = refs[1:1 + nwb]
    v_refs = refs[1 + nwb:1 + 2 * nwb]
    bias_ref = refs[1 + 2 * nwb]
    o_ref = refs[2 + 2 * nwb]
    i = pl.program_id(1)
    for h in range(N_HEADS):
        cols = slice(h * HEAD_DIM, (h + 1) * HEAD_DIM)
        qh = q_ref[0, :, cols]
        logits = []
        for r in range(nwb):
            s = _dot_nt(qh, k_refs[r][0, :, cols]) + bias_ref[h, :, r * wb:(r + 1) * wb]
            logits.append(jnp.where(i - (nwb - 1) + r >= 0, s, NEG))
        m = functools.reduce(jnp.maximum, [jnp.max(s, axis=1, keepdims=True) for s in logits])
        ps = [jnp.exp2(s - m) for s in logits]
        l = functools.reduce(jnp.add, [jnp.sum(p, axis=1, keepdims=True) for p in ps])
        acc = functools.reduce(jnp.add, [_dot(p.astype(BF), v_refs[r][0, :, cols]) for r, p in enumerate(ps)])
        o_ref[0, :, cols] = (acc / l).astype(o_ref.dtype)


def _mixer_a(q_all, k_arr, v_arr, kv_col, bias, tq, nwb, wb):
    b, l, _ = q_all.shape
    kspec = lambda r: pl.BlockSpec((1, wb, BRANCH_W),
                                   lambda bi, i, r=r: (bi, jnp.maximum(i - (nwb - 1) + r, 0), kv_col))
    in_specs = ([pl.BlockSpec((1, tq, BRANCH_W), lambda bi, i: (bi, i, 0))]
                + [kspec(r) for r in range(nwb)] * 2
                + [pl.BlockSpec(bias.shape, lambda bi, i: (0, 0, 0))])
    return pl.pallas_call(
        functools.partial(_mixer_a_kernel, nwb=nwb, wb=wb),
        grid=(b, l // tq), in_specs=in_specs,
        out_specs=pl.BlockSpec((1, tq, BRANCH_W), lambda bi, i: (bi, i, 0)),
        out_shape=jax.ShapeDtypeStruct((b, l, BRANCH_W), BF),
        compiler_params=_cparams("parallel", "parallel"), name="mixer_a",
    )(q_all, *([k_arr] * nwb), *([v_arr] * nwb), bias)


def _band_bias(rel_table, tq, w, n_valid_cols):
    t = np.arange(tq)[:, None]
    c = np.arange(w)[None, :]
    krel = c - BAND_PAST
    ct = t // CHUNK
    inband = (krel >= CHUNK * ct - BAND_PAST) & (krel < CHUNK * ct + CHUNK) & (c < n_valid_cols)
    rel = np.clip(t - krel, -REL_CLIP, REL_CLIP) + REL_CLIP
    bias = jnp.where(inband[..., None], rel_table[rel].astype(F32) * LOG2E, NEG)
    return jnp.moveaxis(bias, -1, 0)


def _mixer_b_kernel(q_ref, k_ref, v_ref, upper_ref, o_ref, *, tq, tk, q_start):
    q0 = q_start + pl.program_id(1) * tq
    nblk = (q0 + tq - 2) // tk + 1
    upper = upper_ref[...]
    qpos = q0 + lax.broadcasted_iota(jnp.int32, (tq, tk), 0)
    kiota = lax.broadcasted_iota(jnp.int32, (tq, tk), 1)
    n_full = q0 // tk
    qm = _masked_pair_heads(q_ref)
    low = _low_half((tq, LANES))

    slabs = [slice(pr * LANES, (pr + 1) * LANES) for pr in range(N_HEADS // 2)]

    def body(t, carry, masked, first, nsub):
        accs, laters = carry
        starts = [pl.multiple_of((first - t * nsub - u) * tk, tk) for u in range(nsub)]
        masks = [(ks + kiota) < qpos for ks in starts]
        zs = [[_dot_nt(qm[h], k_ref[0, pl.ds(ks, tk), slabs[h // 2]]) for h in range(N_HEADS)] for ks in starts]
        log_betas, leaves = [], []
        for u in range(nsub):
            sps = [_softplus2(z) for z in zs[u]]
            leaves.append([jnp.where(masks[u], sp, 0.0) for sp in sps] if masked else sps)
            log_betas.append([z - sp for z, sp in zip(zs[u], sps)])
        tails = [[_split_dot(s, upper, 2) for s in leaves[u]] for u in range(nsub)]
        laters = list(laters)
        ws = []
        for u in range(nsub):
            ws_u = []
            for h in range(N_HEADS):
                w = jnp.exp2(log_betas[u][h] - (tails[u][h] + laters[h]))
                ws_u.append((jnp.where(masks[u], w, 0.0) if masked else w).astype(BF))
                laters[h] = laters[h] + jnp.sum(leaves[u][h], axis=1, keepdims=True)
            ws.append(ws_u)
        wvs = [functools.reduce(jnp.add, [_dot(ws[u][h], v_ref[0, pl.ds(starts[u], tk), slabs[h // 2]])
                                          for u in range(nsub)]) for h in range(N_HEADS)]
        accs_new = [accs[pr] + jnp.where(low, wvs[2 * pr], wvs[2 * pr + 1]) for pr in range(N_HEADS // 2)]
        return tuple(accs_new), tuple(laters)

    carry = (tuple(jnp.zeros((tq, LANES), F32) for _ in range(N_HEADS // 2)),
             tuple(jnp.zeros((tq, 1), F32) for _ in range(N_HEADS)))
    carry = lax.fori_loop(0, nblk - n_full, functools.partial(body, masked=True, first=nblk - 1, nsub=1), carry)
    carry = lax.fori_loop(0, n_full // 2, functools.partial(body, masked=False, first=n_full - 1, nsub=2), carry)
    accs, _ = lax.fori_loop(0, n_full % 2, functools.partial(body, masked=False, first=0, nsub=1), carry)
    for pr in range(N_HEADS // 2):
        o_ref[0, :, pr * LANES:(pr + 1) * LANES] = accs[pr].astype(o_ref.dtype)


def _kv_spec(kp, col):
    return pl.BlockSpec((1, kp, BRANCH_W), lambda bi, i: (bi, 0, col))


def _mixer_b(q_all, k_arr, v_arr, kv_col, tq, tk, q_start):
    b, l, _ = q_all.shape
    kp = k_arr.shape[1]
    upper = (np.arange(tk)[:, None] > np.arange(tk)[None, :]).astype(np.float32)
    return pl.pallas_call(
        functools.partial(_mixer_b_kernel, tq=tq, tk=tk, q_start=q_start),
        grid=(b, l // tq),
        in_specs=[pl.BlockSpec((1, tq, BRANCH_W), lambda bi, i: (bi, i, 1)), _kv_spec(kp, kv_col),
                  _kv_spec(kp, kv_col), pl.BlockSpec((tk, tk), lambda bi, i: (0, 0))],
        out_specs=pl.BlockSpec((1, tq, BRANCH_W), lambda bi, i: (bi, i, 0)),
        out_shape=jax.ShapeDtypeStruct((b, l, BRANCH_W), BF),
        compiler_params=_cparams("parallel", "parallel"), name="mixer_b",
    )(q_all, k_arr, v_arr, jnp.asarray(upper, BF))


def _mixer_c_kernel(q_ref, iq_ref, iw_ref, ik_ref, k_ref, v_ref, prefix_ref, o_ref, key_ref, iwb_ref,
                    *, tq, tk, q_start, n_keys, topk):
    q0 = q_start + pl.program_id(1) * tq
    last_adm = jnp.minimum(((q0 + tq - 1) // CHUNK) * CHUNK + CHUNK - 1, n_keys - 1)
    nblk = last_adm // tk + 1
    qchunk = (q0 + lax.broadcasted_iota(jnp.int32, (tq, tk), 0)) // CHUNK
    kiota = lax.broadcasted_iota(jnp.int32, (tq, tk), 1)
    n_full = jnp.minimum((q0 // CHUNK + 1) * CHUNK, n_keys) // tk
    topk_f = jnp.float32(topk)
    iqm = _masked_pair_heads(iq_ref)
    for g in range(H_IDX):
        iwb_ref[g] = jnp.broadcast_to(iw_ref[0, :, g:g + 1], (tq, tk))

    def score_body(j, carry, masked):
        ks = pl.multiple_of(j * tk, tk)
        ik_pair = ik_ref[0, pl.ds(ks, tk), :]
        dots = [_dot_nt(iqm[g], ik_pair) for g in range(H_IDX)]
        sc = None
        for g in range(H_IDX):
            term = iwb_ref[g] * jnp.maximum(dots[g], 0.0)
            sc = term if sc is None else sc + term
        sc = jnp.where(sc == 0.0, 0.0, sc)
        bits = pltpu.bitcast(sc, jnp.int32)
        key = bits ^ ((bits >> 31) & jnp.int32(0x7FFFFFFF))
        if masked:
            kpos = ks + kiota
            adm = ((kpos // CHUNK) <= qchunk) & (kpos < n_keys)
            key = jnp.where(adm, key, jnp.int32(INT_MIN))
        key_ref[j] = key
        return carry

    lax.fori_loop(0, n_full, functools.partial(score_body, masked=False), 0)
    lax.fori_loop(n_full, nblk, functools.partial(score_body, masked=True), 0)

    def count(pred):
        def cb(j, acc):
            c = jnp.where(pred(key_ref[j]), 1.0, 0.0)
            return acc + functools.reduce(jnp.add, [c[:, s:s + LANES] for s in range(0, tk, LANES)])
        acc = lax.fori_loop(0, nblk, cb, jnp.zeros((tq, LANES), F32))
        return jnp.sum(acc, axis=1, keepdims=True)

    def bit_body(carry):
        t, prefix, n_prefix = carry
        cand = prefix + jnp.left_shift(jnp.int32(1), 31 - t)
        cnt = count(lambda kb: kb >= cand)
        take = cnt >= topk_f
        return t + 1, jnp.where(take, cand, prefix), jnp.where(take, cnt, n_prefix)

    def unsettled(carry):
        t, _, n_prefix = carry
        return (t < 32) & (jnp.max(jnp.abs(n_prefix - topk_f)) > 0.0)

    _, thr, _ = lax.while_loop(unsettled, bit_body, (jnp.int32(0), jnp.full((tq, 1), INT_MIN, jnp.int32),
                                                     jnp.full((tq, 1), -1.0, F32)))
    thr = jnp.maximum(thr, jnp.int32(INT_MIN + 1))
    n_ge = count(lambda kb: kb >= thr)
    need = topk_f - count(lambda kb: kb > thr)

    @pl.when(jnp.max(n_ge) > topk_f)
    def _():
        def fix(j, seen):
            kb = key_ref[j]
            eq = jnp.where(kb == thr, 1.0, 0.0)
            rank = _dot(eq.astype(BF), prefix_ref[...]) + seen
            key_ref[j] = jnp.where(eq * rank > need, jnp.int32(INT_MIN), kb)
            return seen + jnp.sum(eq, axis=1, keepdims=True)
        lax.fori_loop(0, nblk, fix, jnp.zeros((tq, 1), F32))

    qm = _masked_pair_heads(q_ref)
    low = _low_half((tq, LANES))

    def body(j, carry):
        ms, ls, accs = carry
        ks = pl.multiple_of(j * tk, tk)
        selected = key_ref[j] >= thr
        slabs = [slice(pr * LANES, (pr + 1) * LANES) for pr in range(N_HEADS // 2)]
        logits = [_dot_nt(qm[h], k_ref[0, pl.ds(ks, tk), slabs[h // 2]]) for h in range(N_HEADS)]
        ms_new, ls_new, alphas, ps = [], [], [], []
        for h in range(N_HEADS):
            s = jnp.where(selected, logits[h], NEG)
            m_new = jnp.maximum(ms[h], jnp.max(s, axis=1, keepdims=True))
            alpha = jnp.exp2(ms[h] - m_new)
            p = jnp.exp2(s - m_new)
            ms_new.append(m_new)
            ls_new.append(alpha * ls[h] + jnp.sum(p, axis=1, keepdims=True))
            alphas.append(alpha)
            ps.append(p.astype(BF))
        pvs = [_dot(ps[h], v_ref[0, pl.ds(ks, tk), slabs[h // 2]]) for h in range(N_HEADS)]
        accs_new = [jnp.where(low, alphas[2 * pr], alphas[2 * pr + 1]) * accs[pr]
                    + jnp.where(low, pvs[2 * pr], pvs[2 * pr + 1]) for pr in range(N_HEADS // 2)]
        return tuple(ms_new), tuple(ls_new), tuple(accs_new)

    carry = (tuple(jnp.full((tq, 1), NEG, F32) for _ in range(N_HEADS)),
             tuple(jnp.zeros((tq, 1), F32) for _ in range(N_HEADS)),
             tuple(jnp.zeros((tq, LANES), F32) for _ in range(N_HEADS // 2)))
    _, ls, accs = lax.fori_loop(0, nblk, body, carry)
    for pr in range(N_HEADS // 2):
        l_pair = jnp.where(low, ls[2 * pr], ls[2 * pr + 1])
        o_ref[0, :, pr * LANES:(pr + 1) * LANES] = (accs[pr] / l_pair).astype(o_ref.dtype)


def _mixer_c(q_all, iq, iw, ik_arr, k_arr, v_arr, kv_col, tq, tk, q_start, n_keys):
    b, l, _ = q_all.shape
    kp = k_arr.shape[1]
    topk = min(TOPK_MAX, n_keys // 4)
    prefix = (np.arange(tk)[:, None] <= np.arange(tk)[None, :]).astype(np.float32)
    return pl.pallas_call(
        functools.partial(_mixer_c_kernel, tq=tq, tk=tk, q_start=q_start, n_keys=n_keys, topk=topk),
        grid=(b, l // tq),
        in_specs=[pl.BlockSpec((1, tq, BRANCH_W), lambda bi, i: (bi, i, 2)),
                  pl.BlockSpec((1, tq, H_IDX * D_IDX), lambda bi, i: (bi, i, 0)),
                  pl.BlockSpec((1, tq, H_IDX), lambda bi, i: (bi, i, 0)),
                  pl.BlockSpec((1, kp, 2 * D_IDX), lambda bi, i: (bi, 0, 0)),
                  _kv_spec(kp, kv_col), _kv_spec(kp, kv_col),
                  pl.BlockSpec((tk, tk), lambda bi, i: (0, 0))],
        out_specs=pl.BlockSpec((1, tq, BRANCH_W), lambda bi, i: (bi, i, 0)),
        out_shape=jax.ShapeDtypeStruct((b, l, BRANCH_W), BF),
        scratch_shapes=[pltpu.VMEM((kp // tk, tq, tk), jnp.int32), pltpu.VMEM((H_IDX, tq, tk), F32)],
        compiler_params=_cparams("parallel", "parallel"), name="mixer_c",
    )(q_all, iq, iw, ik_arr, k_arr, v_arr, jnp.asarray(prefix, BF))


def _mixer_d_kernel(q_ref, k_ref, v_ref, fq_ref, fk_ref, o_ref, *, tq, tk, q_start, n_kblk):
    q0 = q_start + pl.program_id(1) * tq
    nblk = (q0 + tq - 1) // tk + 1
    n_full = (q0 + 1) // tk
    qpos = q0 + lax.broadcasted_iota(jnp.int32, (tq, tk), 0)
    kiota = lax.broadcasted_iota(jnp.int32, (tq, tk), 1)
    qm = _masked_pair_heads(q_ref)
    fq = [fq_ref[0, :, h:h + 1] for h in range(N_HEADS)]
    low = _low_half((tq, LANES))

    def body(j, carry, masked):
        ms, ls, accs = carry
        ks = pl.multiple_of(j * tk, tk)
        visible = (ks + kiota) <= qpos
        slabs = [slice(pr * LANES, (pr + 1) * LANES) for pr in range(N_HEADS // 2)]
        logits = [_dot_nt(qm[h], k_ref[0, pl.ds(ks, tk), slabs[h // 2]]) for h in range(N_HEADS)]
        ms_new, ls_new, alphas, ps = [], [], [], []
        for h in range(N_HEADS):
            s = logits[h] + fq[h] - fk_ref[0, pl.ds(h * n_kblk + j, 1), :]
            if masked:
                s = jnp.where(visible, s, NEG)
            m_new = jnp.maximum(ms[h], jnp.max(s, axis=1, keepdims=True))
            alpha = jnp.exp2(ms[h] - m_new)
            p = jnp.exp2(s - m_new)
            ms_new.append(m_new)
            ls_new.append(alpha * ls[h] + jnp.sum(p, axis=1, keepdims=True))
            alphas.append(alpha)
            ps.append(p.astype(BF))
        pvs = [_dot(ps[h], v_ref[0, pl.ds(ks, tk), slabs[h // 2]]) for h in range(N_HEADS)]
        accs_new = [jnp.where(low, alphas[2 * pr], alphas[2 * pr + 1]) * accs[pr]
                    + jnp.where(low, pvs[2 * pr], pvs[2 * pr + 1]) for pr in range(N_HEADS // 2)]
        return tuple(ms_new), tuple(ls_new), tuple(accs_new)

    carry = (tuple(jnp.full((tq, 1), NEG, F32) for _ in range(N_HEADS)),
             tuple(jnp.zeros((tq, 1), F32) for _ in range(N_HEADS)),
             tuple(jnp.zeros((tq, LANES), F32) for _ in range(N_HEADS // 2)))
    carry = lax.fori_loop(0, n_full, functools.partial(body, masked=False), carry)
    _, ls, accs = lax.fori_loop(n_full, nblk, functools.partial(body, masked=True), carry)
    for pr in range(N_HEADS // 2):
        l_pair = jnp.where(low, ls[2 * pr], ls[2 * pr + 1])
        o_ref[0, :, pr * LANES:(pr + 1) * LANES] = (accs[pr] / l_pair).astype(o_ref.dtype)


def _mixer_d(q_all, k_arr, v_arr, kv_col, fq, fk, tq, tk, q_start):
    b, l, _ = q_all.shape
    kp = k_arr.shape[1]
    n_kblk = kp // tk
    return pl.pallas_call(
        functools.partial(_mixer_d_kernel, tq=tq, tk=tk, q_start=q_start, n_kblk=n_kblk),
        grid=(b, l // tq),
        in_specs=[pl.BlockSpec((1, tq, BRANCH_W), lambda bi, i: (bi, i, 3)), _kv_spec(kp, kv_col),
                  _kv_spec(kp, kv_col),
                  pl.BlockSpec((1, tq, N_HEADS), lambda bi, i: (bi, i, 0)),
                  pl.BlockSpec((1, N_HEADS * n_kblk, tk), lambda bi, i: (bi, 0, 0))],
        out_specs=pl.BlockSpec((1, tq, BRANCH_W), lambda bi, i: (bi, i, 0)),
        out_shape=jax.ShapeDtypeStruct((b, l, BRANCH_W), BF),
        compiler_params=_cparams("parallel", "parallel"), name="mixer_d",
    )(q_all, k_arr, v_arr, fq, fk)


def _merge_kernel(x_ref, g1_ref, oa_ref, ob_ref, oc_ref, od_ref, wg_ref, wb_ref, wo_ref, y_ref):
    x = x_ref[...]
    h = _rms(x, g1_ref[...]).astype(BF)
    merged = None
    for g, o_ref in enumerate((oa_ref, ob_ref, oc_ref, od_ref)):
        gate = 1.0 / (1.0 + jnp.exp(-_dot(h, wg_ref[:, g * D_MODEL:(g + 1) * D_MODEL])))
        term = gate * _dot(o_ref[...], wb_ref[g])
        merged = term if merged is None else merged + term
    y_ref[...] = x + _dot(merged.astype(BF), wo_ref[...])


def _merge(x2, outs, lp, tm):
    m_rows = x2.shape[0]
    row = lambda i: (i, 0)
    const = lambda i: (0, 0)
    return pl.pallas_call(
        _merge_kernel, grid=(m_rows // tm,),
        in_specs=[pl.BlockSpec((tm, D_MODEL), row), pl.BlockSpec((1, D_MODEL), const)]
                 + [pl.BlockSpec((tm, BRANCH_W), row)] * N_MIXERS
                 + [pl.BlockSpec((D_MODEL, N_MIXERS * D_MODEL), const),
                    pl.BlockSpec((N_MIXERS, BRANCH_W, D_MODEL), lambda i: (0, 0, 0)),
                    pl.BlockSpec((D_MODEL, D_MODEL), const)],
        out_specs=pl.BlockSpec((tm, D_MODEL), row),
        out_shape=jax.ShapeDtypeStruct((m_rows, D_MODEL), F32),
        compiler_params=_cparams("parallel"), name="merge",
    )(x2, lp["g1"], *outs, lp["w_gate"], lp["w_branch"], lp["w_o"])


def _gelu_tanh(x):
    return x * (0.5 * (1.0 + jnp.tanh(np.sqrt(2.0 / np.pi).astype(np.float32) * (x + 0.044715 * (x * x * x)))))


def _ffn_kernel(x_ref, xp_ref, st_ref, g2_ref, wu_ref, wgt_ref, wc_ref, bc_ref, wd_ref, y_ref, ut_ref,
                h_ref, acc_ref, *, tm, tiles_per_seq):
    i = pl.program_id(0)
    f = pl.program_id(1)

    @pl.when(f == 0)
    def _():
        g2 = g2_ref[...]
        h_ref[:CTX_ROWS, :] = _rms(xp_ref[...], g2).astype(BF)
        h_ref[CTX_ROWS:, :] = _rms(x_ref[...], g2).astype(BF)
        acc_ref[...] = x_ref[...]

    h = h_ref[...]
    u = _dot(h, wu_ref[...])
    gt = _dot(h, wgt_ref[...])[CTX_ROWS:]
    ctx = jnp.where(i % tiles_per_seq == 0, st_ref[0], u[:CTX_ROWS])
    ue = jnp.concatenate([ctx, u[CTX_ROWS:]], axis=0)
    wc = wc_ref[...]
    conv = bc_ref[...] + ((ue[CTX_ROWS - 2:CTX_ROWS - 2 + tm] * wc[0:1] + ue[CTX_ROWS - 1:CTX_ROWS - 1 + tm] * wc[1:2])
                          + ue[CTX_ROWS:] * wc[2:3])
    a = _gelu_tanh(conv) * gt
    acc_ref[...] += _dot(a.astype(BF), wd_ref[...])
    ut_ref[0] = ue[tm:]

    @pl.when(f == pl.num_programs(1) - 1)
    def _():
        y_ref[...] = acc_ref[...]


def _ffn(x2, state8, lp, tm, tf, seq_len):
    m_rows = x2.shape[0]
    b = state8.shape[0]
    tiles_per_seq = seq_len // tm
    n_f = D_FF // tf
    return pl.pallas_call(
        functools.partial(_ffn_kernel, tm=tm, tiles_per_seq=tiles_per_seq),
        grid=(m_rows // tm, n_f),
        in_specs=[pl.BlockSpec((tm, D_MODEL), lambda i, f: (i, 0)),
                  pl.BlockSpec((CTX_ROWS, D_MODEL), lambda i, f: (jnp.maximum(i * (tm // CTX_ROWS) - 1, 0), 0)),
                  pl.BlockSpec((1, CTX_ROWS, tf), lambda i, f: (i // tiles_per_seq, 0, f)),
                  pl.BlockSpec((1, D_MODEL), lambda i, f: (0, 0)),
                  pl.BlockSpec((D_MODEL, tf), lambda i, f: (0, f)),
                  pl.BlockSpec((D_MODEL, tf), lambda i, f: (0, n_f + f)),
                  pl.BlockSpec((CONV_W, tf), lambda i, f: (0, f)),
                  pl.BlockSpec((1, tf), lambda i, f: (0, f)),
                  pl.BlockSpec((tf, D_MODEL), lambda i, f: (f, 0))],
        out_specs=[pl.BlockSpec((tm, D_MODEL), lambda i, f: (i, 0)),
                   pl.BlockSpec((1, CTX_ROWS, tf), lambda i, f: (i, 0, f))],
        out_shape=[jax.ShapeDtypeStruct((m_rows, D_MODEL), F32),
                   jax.ShapeDtypeStruct((m_rows // tm, CTX_ROWS, D_FF), F32)],
        scratch_shapes=[pltpu.VMEM((tm + CTX_ROWS, D_MODEL), BF), pltpu.VMEM((tm, D_MODEL), F32)],
        compiler_params=_cparams("parallel", "arbitrary"), name="ffn",
    )(x2, x2, state8, lp["g2"], lp["w_ffn_in"], lp["w_ffn_in"], lp["w_conv"], lp["b_conv"], lp["w_down"])


def _rope_tables(q_start, length):
    half = HEAD_DIM // 2
    inv = ROPE_THETA ** (-jnp.arange(half, dtype=F32) / half)
    ang = (q_start + jnp.arange(length)).astype(F32)[:, None] * inv[None, :]
    cos = jnp.cos(ang)
    sin = jnp.sin(ang)
    cos_h = jnp.concatenate([cos, cos], axis=1)
    sin_h = jnp.concatenate([-sin, sin], axis=1)
    n_rot = H_IDX + 1
    pad = MISC_W - n_rot * HEAD_DIM
    return {
        "cq": jnp.tile(cos_h, (1, N_HEADS)), "sq": jnp.tile(sin_h, (1, N_HEADS)),
        "cm": jnp.concatenate([jnp.tile(cos_h, (1, n_rot)), jnp.ones((length, pad), F32)], axis=1),
        "sm": jnp.concatenate([jnp.tile(sin_h, (1, n_rot)), jnp.zeros((length, pad), F32)], axis=1),
    }


def _layer_params(norm1_g, w_in, b_forget, qk_g, rel_table, w_branch, w_o, norm2_g, w_ffn_in, w_conv, b_conv,
                  w_down):
    a = 3 * QKV_W
    i1 = a + H_IDX * D_IDX + D_IDX + H_IDX
    d1 = i1 + QKV_W
    f1 = d1 + N_HEADS
    n_misc = (i1 - a) + N_HEADS
    w_misc = jnp.concatenate([w_in[:, a:i1], w_in[:, d1:f1], jnp.zeros((D_MODEL, MISC_W - n_misc), w_in.dtype)],
                             axis=1)
    f_lane = D_IDX + H_IDX
    bf_row = jnp.zeros((1, LANES), F32).at[0, f_lane:f_lane + N_HEADS].set(b_forget.astype(F32))
    head_ones = (np.arange(BRANCH_W)[:, None] // HEAD_DIM == np.arange(BRANCH_W)[None, :] // HEAD_DIM)
    return {
        "g1": norm1_g.astype(F32)[None, :], "g2": norm2_g.astype(F32)[None, :],
        "w_qkv": jnp.concatenate([w_in[:, :a], w_in[:, i1:d1]], axis=1).astype(BF),
        "w_misc": w_misc.astype(BF), "w_gate": w_in[:, f1:].astype(BF),
        "qkg": jnp.tile(qk_g.astype(F32), (1, N_HEADS)), "bf_row": bf_row,
        "head_ones": jnp.asarray(head_ones.astype(np.float32), BF),
        "rel_table": rel_table, "w_branch": w_branch.astype(BF), "w_o": w_o.astype(BF),
        "w_ffn_in": w_ffn_in.astype(BF), "w_conv": w_conv.astype(F32), "b_conv": b_conv.astype(F32)[None, :],
        "w_down": w_down.astype(BF),
    }


def _pad_rows(a, rows):
    return jnp.pad(a, ((0, 0), (0, rows - a.shape[1])) + ((0, 0),) * (a.ndim - 2))


def _round_up(n, m):
    return -(-n // m) * m


def _layer(x, past, lp, tabs):
    b, l, _ = x.shape
    m_rows = b * l
    q_start = 0 if past is None else past[2].shape[1]
    n_keys = q_start + l
    tq = min(256, l)
    tq_c = min(128, l)
    tk = 256
    tk_d = 512
    kp = _round_up(n_keys, max(tk, tk_d))
    tm = min(256, m_rows)

    x2 = x.reshape(m_rows, D_MODEL)
    (q_all, kbf, vbf, ka, va, kb, vb, kc, vc, kd, vd, iq, misc) = _projection(x2, lp, tabs, tm)
    ik = misc[:, :D_IDX].reshape(b, l, D_IDX)
    iw = misc[:, D_IDX:D_IDX + H_IDX].reshape(b, l, H_IDX)
    logf = misc[:, D_IDX + H_IDX:D_IDX + H_IDX + N_HEADS].reshape(b, l, N_HEADS)
    q_all = q_all.reshape(b, l, N_MIXERS * BRANCH_W)
    iq = iq.reshape(b, l, H_IDX * D_IDX)
    heads = lambda t: t.reshape(b, l, N_HEADS, HEAD_DIM)
    flat = lambda t: t.reshape(t.shape[0], t.shape[1], BRANCH_W)

    if past is None:
        kbf3 = kbf.reshape(b, l, N_MIXERS * BRANCH_W)
        vbf3 = vbf.reshape(b, l, N_MIXERS * BRANCH_W)
        kv = [(kbf3, vbf3, c) for c in range(N_MIXERS)]
        ik_all = ik.astype(BF)
        logf_all = logf
        a_args = (kbf3, vbf3, 0, _band_bias(lp["rel_table"], tq, BAND_PAST + tq, BAND_PAST + tq), tq, 3, tq)
        conv_state = jnp.zeros((b, CONV_W - 1, D_FF), F32)
    else:
        (pa_k, pa_v, pb_k, pb_v, pc_k, pc_v, pc_ki, pd_k, pd_v, pd_f, conv_state) = past
        new_bf = lambda t, c: t.reshape(b, l, N_MIXERS, BRANCH_W)[:, :, c]
        cat = lambda p, c, src, rows: _pad_rows(jnp.concatenate([flat(p).astype(BF), new_bf(src, c)], axis=1), rows)
        kv = [None] + [(cat(pk, c, kbf, kp), cat(pv, c, vbf, kp), 0)
                       for c, (pk, pv) in ((1, (pb_k, pb_v)), (2, (pc_k, pc_v)), (3, (pd_k, pd_v)))]
        ik_all = jnp.concatenate([pc_ki.astype(BF), ik.astype(BF)], axis=1)
        logf_all = jnp.concatenate([pd_f.astype(F32), logf], axis=1)
        a_keys = pa_k.shape[1] + l
        wa = _round_up(BAND_PAST + l, LANES)
        a_args = (cat(pa_k, 0, kbf, wa), cat(pa_v, 0, vbf, wa), 0,
                  _band_bias(lp["rel_table"], l, wa, a_keys), l, 1, wa)

    f_cum = jnp.cumsum(logf_all, axis=1) * LOG2E
    fq = f_cum[:, q_start:]
    fk = jnp.moveaxis(_pad_rows(f_cum, kp), 1, 2).reshape(b, N_HEADS * (kp // tk_d), tk_d)

    o_a = _mixer_a(q_all, *a_args)
    o_b = _mixer_b(q_all, *kv[1], tq, tk, q_start)
    ik_pair = _pad_rows(jnp.concatenate([ik_all, ik_all], axis=-1), kp)
    o_c = _mixer_c(q_all, iq, iw, ik_pair, *kv[2], tq_c, tk_d, q_start, n_keys)
    o_d = _mixer_d(q_all, *kv[3], fq, fk, tq, tk_d, q_start)
    x2 = _merge(x2, [o.reshape(m_rows, BRANCH_W) for o in (o_a, o_b, o_c, o_d)], lp, tm)

    state8 = jnp.pad(conv_state.astype(F32), ((0, 0), (CTX_ROWS - (CONV_W - 1), 0), (0, 0)))
    x2, u_tail = _ffn(x2, state8, lp, min(512, l), 256, l)
    u_tail = u_tail.reshape(b, -1, CTX_ROWS, D_FF)[:, -1]

    if past is None:
        a_keep = min(BAND_PAST, l)
        new_a = (heads(ka)[:, -a_keep:], heads(va)[:, -a_keep:])
    else:
        a_keep = pa_k.shape[1]
        new_a = (jnp.concatenate([pa_k, heads(ka)], axis=1)[:, -a_keep:],
                 jnp.concatenate([pa_v, heads(va)], axis=1)[:, -a_keep:])
    new = new_a + (heads(kb), heads(vb), heads(kc), heads(vc), ik, heads(kd), heads(vd), logf,
                   u_tail[:, -(CONV_W - 1):])
    return x2.reshape(b, l, D_MODEL), new


def kernel(x_prompt, x_sample, cache_a_k, cache_a_v, cache_b_k, cache_b_v, cache_c_k, cache_c_v, cache_c_kidx, cache_d_k, cache_d_v, cache_d_logf, state_ffn_conv, norm1_g, w_in, b_forget, qk_norm_g, rel_bias, w_branch, w_o, norm2_g, w_ffn_in, w_conv, b_conv, w_down):
    depth = w_in.shape[0]
    past_len = cache_b_k.shape[2]
    tabs_p = _rope_tables(0, x_prompt.shape[1])
    tabs_s = {k: jnp.tile(v, (x_sample.shape[0], 1)) for k, v in _rope_tables(past_len, x_sample.shape[1]).items()}
    y_p, y_s = x_prompt, x_sample
    p_states, s_states = [], []
    for d in range(depth):
        lp = _layer_params(norm1_g[d], w_in[d], b_forget[d], qk_norm_g[d], rel_bias[d], w_branch[d], w_o[d],
                           norm2_g[d], w_ffn_in[d], w_conv[d], b_conv[d], w_down[d])
        y_p, st_p = _layer(y_p, None, lp, tabs_p)
        past = (cache_a_k[d], cache_a_v[d], cache_b_k[d], cache_b_v[d], cache_c_k[d], cache_c_v[d],
                cache_c_kidx[d], cache_d_k[d], cache_d_v[d], cache_d_logf[d], state_ffn_conv[d])
        y_s, st_s = _layer(y_s, past, lp, tabs_s)
        p_states.append(st_p)
        s_states.append(st_s)
    p_out = [jnp.stack(t) for t in zip(*p_states)]
    s_out = [jnp.stack(t) for t in zip(*s_states)]
    return (y_p, y_s, *p_out, *s_out)
```

```python
import functools

import numpy as np
import jax
import jax.numpy as jnp
from jax import lax
from jax.experimental import pallas as pl
from jax.experimental.pallas import tpu as pltpu

D_MODEL = 1024
CHUNK = 64
N_MIXERS = 4
N_HEADS = 4
HEAD_DIM = 64
BRANCH_W = N_HEADS * HEAD_DIM
BAND_PAST = 8 * CHUNK
REL_CLIP = 128
H_IDX = 4
D_IDX = 64
TOPK_MAX = 256
ROPE_THETA = 10000.0
D_FF = 2816
CONV_W = 3
EPS = 1e-6

BF = jnp.bfloat16
F32 = jnp.float32
NEG = -1e30
LOG2E = float(np.log2(np.e))
INT_MIN = -2 ** 31
LANES = 128
SUBLANES = 8
CTX_ROWS = 2 * SUBLANES
QKV_W = 3 * BRANCH_W
MISC_W = 3 * LANES
VMEM_LIMIT = 56 * 1024 * 1024


def _cparams(*sem):
    return pltpu.CompilerParams(dimension_semantics=sem, vmem_limit_bytes=VMEM_LIMIT)


def _dot(a, b):
    return jnp.dot(a, b, preferred_element_type=F32)


def _dot_nt(a, b):
    return lax.dot_general(a, b, (((1,), (1,)), ((), ())), preferred_element_type=F32)


def _split_dot(x, m, terms):
    acc = None
    r = x
    for t in range(terms):
        hi = r.astype(BF)
        d = _dot(hi, m)
        acc = d if acc is None else acc + d
        if t + 1 < terms:
            r = r - hi.astype(F32)
    return acc


def _rms(x, g):
    ms = jnp.mean(x * x, axis=-1, keepdims=True)
    return x * lax.rsqrt(ms + EPS) * g


def _head_rms(t, g, head_ones):
    ms = _split_dot(t * t, head_ones, 3) * (1.0 / HEAD_DIM)
    return t * lax.rsqrt(ms + EPS) * g


def _swap_halves(y):
    n = y.shape[-1]
    lane = lax.broadcasted_iota(jnp.int32, y.shape, 1)
    first = (lane & (HEAD_DIM - 1)) < (HEAD_DIM // 2)
    return jnp.where(first, pltpu.roll(y, n - HEAD_DIM // 2, 1), pltpu.roll(y, HEAD_DIM // 2, 1))


def _softplus2(z):
    neg_abs = pltpu.bitcast(pltpu.bitcast(z, jnp.int32) | jnp.int32(INT_MIN), F32)
    return jnp.maximum(z, 0.0) + jnp.log2(1.0 + jnp.exp2(neg_abs))


def _low_half(shape):
    return lax.broadcasted_iota(jnp.int32, shape, 1) < HEAD_DIM


def _masked_pair_heads(ref):
    out = []
    for pr in range(ref.shape[-1] // LANES):
        slab = ref[0, :, pr * LANES:(pr + 1) * LANES].astype(F32)
        low = _low_half(slab.shape)
        out.append(jnp.where(low, slab, 0.0).astype(BF))
        out.append(jnp.where(low, 0.0, slab).astype(BF))
    return out


def _softmax_init(tq):
    return (tuple(jnp.full((tq, 1), NEG, F32) for _ in range(N_HEADS)),
            tuple(jnp.zeros((tq, 1), F32) for _ in range(N_HEADS)),
            tuple(jnp.zeros((tq, LANES), F32) for _ in range(N_HEADS // 2)))


def _softmax_step(carry, logits, v_tile, low):
    ms, ls, accs = carry
    tiles = range(len(logits))
    ms_new, ls_new, alphas, ps = [], [], [], []
    for h in range(N_HEADS):
        m_new = functools.reduce(jnp.maximum, [ms[h]] + [jnp.max(logits[u][h], axis=1, keepdims=True) for u in tiles])
        alpha = jnp.exp2(ms[h] - m_new)
        p = [jnp.exp2(logits[u][h] - m_new) for u in tiles]
        ms_new.append(m_new)
        ls_new.append(alpha * ls[h] + functools.reduce(jnp.add, [jnp.sum(pu, axis=1, keepdims=True) for pu in p]))
        alphas.append(alpha)
        ps.append([pu.astype(BF) for pu in p])
    pvs = [functools.reduce(jnp.add, [_dot(ps[h][u], v_tile(u, h // 2)) for u in tiles]) for h in range(N_HEADS)]
    accs_new = [jnp.where(low, alphas[2 * pr], alphas[2 * pr + 1]) * accs[pr]
                + jnp.where(low, pvs[2 * pr], pvs[2 * pr + 1]) for pr in range(N_HEADS // 2)]
    return tuple(ms_new), tuple(ls_new), tuple(accs_new)


def _softmax_finish(carry, o_ref, low):
    _, ls, accs = carry
    for pr in range(N_HEADS // 2):
        l_pair = jnp.where(low, ls[2 * pr], ls[2 * pr + 1])
        o_ref[0, :, pr * LANES:(pr + 1) * LANES] = (accs[pr] / l_pair).astype(o_ref.dtype)


def _proj_kernel(x_ref, g1_ref, wqkv_ref, wmisc_ref, qkg_ref, hones_ref, cq_ref, sq_ref, cm_ref, sm_ref,
                 bf_ref, q_ref, kbf_ref, vbf_ref, ka_ref, va_ref, kb_ref, vb_ref, kc_ref, vc_ref,
                 kd_ref, vd_ref, iq_ref, misc_ref):
    h = _rms(x_ref[...], g1_ref[...]).astype(BF)
    hones = hones_ref[...]
    k32 = (ka_ref, kb_ref, kc_ref, kd_ref)
    v32 = (va_ref, vb_ref, vc_ref, vd_ref)
    norm_row = (0, None, 2, 4)
    for m in range(N_MIXERS):
        y = _dot(h, wqkv_ref[:, m * QKV_W:(m + 1) * QKV_W])
        q, k, v = y[:, :BRANCH_W], y[:, BRANCH_W:2 * BRANCH_W], y[:, 2 * BRANCH_W:]
        if norm_row[m] is not None:
            r = norm_row[m]
            q = _head_rms(q, qkg_ref[r:r + 1, :], hones)
            k = _head_rms(k, qkg_ref[r + 1:r + 2, :], hones)
        if m == 2:
            c, s = cq_ref[...], sq_ref[...]
            q = q * c + _swap_halves(q) * s
            k = k * c + _swap_halves(k) * s
        cols = slice(m * BRANCH_W, (m + 1) * BRANCH_W)
        q_ref[:, cols] = (q * (HEAD_DIM ** -0.5 * LOG2E)).astype(BF)
        kbf_ref[:, cols] = k.astype(BF)
        vbf_ref[:, cols] = v.astype(BF)
        k32[m][...] = k
        v32[m][...] = v
    ym = _dot(h, wmisc_ref[...])
    r = ym * cm_ref[...] + _swap_halves(ym) * sm_ref[...]
    iq_ref[...] = r[:, :H_IDX * D_IDX].astype(BF)
    g2 = r[:, H_IDX * D_IDX:]
    lane = lax.broadcasted_iota(jnp.int32, g2.shape, 1)
    z = g2 + bf_ref[...]
    logf = jnp.minimum(z, 0.0) - jnp.log1p(jnp.exp(-jnp.abs(z)))
    is_w = (lane >= D_IDX) & (lane < D_IDX + H_IDX)
    is_f = (lane >= D_IDX + H_IDX) & (lane < D_IDX + H_IDX + N_HEADS)
    misc_ref[...] = jnp.where(is_f, logf, jnp.where(is_w, g2 * (H_IDX ** -0.5 * D_IDX ** -0.5), g2))


def _projection(x2, lp, tabs, tm):
    m_rows = x2.shape[0]
    n_tab = tabs["cq"].shape[0] // tm
    row = lambda i: (i, 0)
    const = lambda i: (0, 0)
    tab = lambda i: (i % n_tab, 0)
    wide = pl.BlockSpec((tm, N_MIXERS * BRANCH_W), row)
    head = pl.BlockSpec((tm, BRANCH_W), row)
    in_specs = [
        pl.BlockSpec((tm, D_MODEL), row),
        pl.BlockSpec((1, D_MODEL), const),
        pl.BlockSpec((D_MODEL, N_MIXERS * QKV_W), const),
        pl.BlockSpec((D_MODEL, MISC_W), const),
        pl.BlockSpec((6, BRANCH_W), const),
        pl.BlockSpec((BRANCH_W, BRANCH_W), const),
        pl.BlockSpec((tm, BRANCH_W), tab),
        pl.BlockSpec((tm, BRANCH_W), tab),
        pl.BlockSpec((tm, MISC_W), tab),
        pl.BlockSpec((tm, MISC_W), tab),
        pl.BlockSpec((1, LANES), const),
    ]
    out_shape = ([jax.ShapeDtypeStruct((m_rows, N_MIXERS * BRANCH_W), BF)] * 3
                 + [jax.ShapeDtypeStruct((m_rows, BRANCH_W), F32)] * 8
                 + [jax.ShapeDtypeStruct((m_rows, BRANCH_W), BF),
                    jax.ShapeDtypeStruct((m_rows, LANES), F32)])
    out_specs = [wide] * 3 + [head] * 8 + [head, pl.BlockSpec((tm, LANES), row)]
    return pl.pallas_call(
        _proj_kernel, grid=(m_rows // tm,), in_specs=in_specs, out_specs=out_specs, out_shape=out_shape,
        compiler_params=_cparams("parallel"), name="projection",
    )(x2, lp["g1"], lp["w_qkv"], lp["w_misc"], lp["qkg"], lp["head_ones"],
      tabs["cq"], tabs["sq"], tabs["cm"], tabs["sm"], lp["bf_row"])


def _mixer_a_kernel(*refs, nwb, wb):
    q_ref = refs[0]
    k_refs = refs[1:1 + nwb]
    v_refs = refs[1 + nwb:1 + 2 * nwb]
    bias_ref = refs[1 + 2 * nwb]
    o_ref = refs[2 + 2 * nwb]
    i = pl.program_id(1)
    for h in range(N_HEADS):
        cols = slice(h * HEAD_DIM, (h + 1) * HEAD_DIM)
        qh = q_ref[0, :, cols]
        logits = []
        for r in range(nwb):
            s = _dot_nt(qh, k_refs[r][0, :, cols]) + bias_ref[h, :, r * wb:(r + 1) * wb]
            logits.append(jnp.where(i - (nwb - 1) + r >= 0, s, NEG))
        m = functools.reduce(jnp.maximum, [jnp.max(s, axis=1, keepdims=True) for s in logits])
        ps = [jnp.exp2(s - m) for s in logits]
        l = functools.reduce(jnp.add, [jnp.sum(p, axis=1, keepdims=True) for p in ps])
        acc = functools.reduce(jnp.add, [_dot(p.astype(BF), v_refs[r][0, :, cols]) for r, p in enumerate(ps)])
        o_ref[0, :, cols] = (acc / l).astype(o_ref.dtype)


def _mixer_a(q_all, k_arr, v_arr, kv_col, bias, tq, nwb, wb):
    b, l, _ = q_all.shape
    kspec = lambda r: pl.BlockSpec((1, wb, BRANCH_W),
                                   lambda bi, i, r=r: (bi, jnp.maximum(i - (nwb - 1) + r, 0), kv_col))
    in_specs = ([pl.BlockSpec((1, tq, BRANCH_W), lambda bi, i: (bi, i, 0))]
                + [kspec(r) for r in range(nwb)] * 2
                + [pl.BlockSpec(bias.shape, lambda bi, i: (0, 0, 0))])
    return pl.pallas_call(
        functools.partial(_mixer_a_kernel, nwb=nwb, wb=wb),
        grid=(b, l // tq), in_specs=in_specs,
        out_specs=pl.BlockSpec((1, tq, BRANCH_W), lambda bi, i: (bi, i, 0)),
        out_shape=jax.ShapeDtypeStruct((b, l, BRANCH_W), BF),
        compiler_params=_cparams("parallel", "parallel"), name="mixer_a",
    )(q_all, *([k_arr] * nwb), *([v_arr] * nwb), bias)


def _band_bias(rel_table, tq, w, n_valid_cols):
    t = np.arange(tq)[:, None]
    c = np.arange(w)[None, :]
    krel = c - BAND_PAST
    ct = t // CHUNK
    inband = (krel >= CHUNK * ct - BAND_PAST) & (krel < CHUNK * ct + CHUNK) & (c < n_valid_cols)
    rel = np.clip(t - krel, -REL_CLIP, REL_CLIP) + REL_CLIP
    bias = jnp.where(inband[..., None], rel_table[rel].astype(F32) * LOG2E, NEG)
    return jnp.moveaxis(bias, -1, 0)


def _mixer_b_kernel(q_ref, k_ref, v_ref, upper_ref, o_ref, *, tq, tk, q_start):
    q0 = q_start + pl.program_id(1) * tq
    nblk = (q0 + tq - 2) // tk + 1
    upper = upper_ref[...]
    qpos = q0 + lax.broadcasted_iota(jnp.int32, (tq, tk), 0)
    kiota = lax.broadcasted_iota(jnp.int32, (tq, tk), 1)
    n_full = q0 // tk
    qm = _masked_pair_heads(q_ref)
    low = _low_half((tq, LANES))

    slabs = [slice(pr * LANES, (pr + 1) * LANES) for pr in range(N_HEADS // 2)]

    def body(t, carry, masked, first, nsub):
        accs, laters = carry
        starts = [pl.multiple_of((first - t * nsub - u) * tk, tk) for u in range(nsub)]
        masks = [(ks + kiota) < qpos for ks in starts]
        zs = [[_dot_nt(qm[h], k_ref[0, pl.ds(ks, tk), slabs[h // 2]]) for h in range(N_HEADS)] for ks in starts]
        log_betas, leaves = [], []
        for u in range(nsub):
            sps = [_softplus2(z) for z in zs[u]]
            leaves.append([jnp.where(masks[u], sp, 0.0) for sp in sps] if masked else sps)
            log_betas.append([z - sp for z, sp in zip(zs[u], sps)])
        tails = [[_split_dot(s, upper, 2) for s in leaves[u]] for u in range(nsub)]
        laters = list(laters)
        ws = []
        for u in range(nsub):
            ws_u = []
            for h in range(N_HEADS):
                w = jnp.exp2(log_betas[u][h] - (tails[u][h] + laters[h]))
                ws_u.append((jnp.where(masks[u], w, 0.0) if masked else w).astype(BF))
                laters[h] = laters[h] + jnp.sum(leaves[u][h], axis=1, keepdims=True)
            ws.append(ws_u)
        wvs = [functools.reduce(jnp.add, [_dot(ws[u][h], v_ref[0, pl.ds(starts[u], tk), slabs[h // 2]])
                                          for u in range(nsub)]) for h in range(N_HEADS)]
        accs_new = [accs[pr] + jnp.where(low, wvs[2 * pr], wvs[2 * pr + 1]) for pr in range(N_HEADS // 2)]
        return tuple(accs_new), tuple(laters)

    carry = (tuple(jnp.zeros((tq, LANES), F32) for _ in range(N_HEADS // 2)),
             tuple(jnp.zeros((tq, 1), F32) for _ in range(N_HEADS)))
    carry = lax.fori_loop(0, nblk - n_full, functools.partial(body, masked=True, first=nblk - 1, nsub=1), carry)
    carry = lax.fori_loop(0, n_full // 2, functools.partial(body, masked=False, first=n_full - 1, nsub=2), carry)
    accs, _ = lax.fori_loop(0, n_full % 2, functools.partial(body, masked=False, first=0, nsub=1), carry)
    for pr in range(N_HEADS // 2):
        o_ref[0, :, pr * LANES:(pr + 1) * LANES] = accs[pr].astype(o_ref.dtype)


def _resident_spec(kp, width, col):
    return pl.BlockSpec((1, kp, width), lambda bi, i: (bi, 0, col), pipeline_mode=pl.Buffered(1))


def _kv_spec(kp, col):
    return _resident_spec(kp, BRANCH_W, col)


def _mixer_b(q_all, k_arr, v_arr, kv_col, tq, tk, q_start):
    b, l, _ = q_all.shape
    kp = k_arr.shape[1]
    upper = (np.arange(tk)[:, None] > np.arange(tk)[None, :]).astype(np.float32)
    return pl.pallas_call(
        functools.partial(_mixer_b_kernel, tq=tq, tk=tk, q_start=q_start),
        grid=(b, l // tq),
        in_specs=[pl.BlockSpec((1, tq, BRANCH_W), lambda bi, i: (bi, i, 1)), _kv_spec(kp, kv_col),
                  _kv_spec(kp, kv_col), pl.BlockSpec((tk, tk), lambda bi, i: (0, 0))],
        out_specs=pl.BlockSpec((1, tq, BRANCH_W), lambda bi, i: (bi, i, 0)),
        out_shape=jax.ShapeDtypeStruct((b, l, BRANCH_W), BF),
        compiler_params=_cparams("parallel", "parallel"), name="mixer_b",
    )(q_all, k_arr, v_arr, jnp.asarray(upper, BF))


def _mixer_c_kernel(q_ref, iq_ref, iw_ref, ik_ref, k_ref, v_ref, prefix_ref, o_ref, key_ref, iwb_ref, plane_ref,
                    tied_ref,
                    *, tq, tk, q_start, n_keys, topk):
    q0 = q_start + pl.program_id(1) * tq
    last_adm = jnp.minimum(((q0 + tq - 1) // CHUNK) * CHUNK + CHUNK - 1, n_keys - 1)
    nblk = last_adm // tk + 1
    qchunk = (q0 + lax.broadcasted_iota(jnp.int32, (tq, tk), 0)) // CHUNK
    kiota = lax.broadcasted_iota(jnp.int32, (tq, tk), 1)
    n_full = jnp.minimum((q0 // CHUNK + 1) * CHUNK, n_keys) // tk
    topk_f = jnp.float32(topk)
    iqm = _masked_pair_heads(iq_ref)
    for g in range(H_IDX):
        iwb_ref[g] = jnp.broadcast_to(iw_ref[0, :, g:g + 1], (tq, tk))

    def score_body(t, carry, masked, first, nsub):
        blocks = [first + t * nsub + u for u in range(nsub)]
        starts = [pl.multiple_of(j * tk, tk) for j in blocks]
        dots = [[_dot_nt(iqm[g], ik_ref[0, pl.ds(ks, tk), :]) for g in range(H_IDX)] for ks in starts]
        for u in range(nsub):
            sc = None
            for g in range(H_IDX):
                term = iwb_ref[g] * jnp.maximum(dots[u][g], 0.0)
                sc = term if sc is None else sc + term
            sc = jnp.where(sc == 0.0, 0.0, sc)
            bits = pltpu.bitcast(sc, jnp.int32)
            key = bits ^ ((bits >> 31) & jnp.int32(0x7FFFFFFF))
            if masked:
                kpos = starts[u] + kiota
                adm = ((kpos // CHUNK) <= qchunk) & (kpos < n_keys)
                key = jnp.where(adm, key, jnp.int32(INT_MIN))
            key_ref[blocks[u]] = key
        return carry

    lax.fori_loop(0, n_full // 2, functools.partial(score_body, masked=False, first=0, nsub=2), 0)
    lax.fori_loop(0, n_full % 2, functools.partial(score_body, masked=False, first=n_full - 1, nsub=1), 0)
    lax.fori_loop(0, nblk - n_full, functools.partial(score_body, masked=True, first=n_full, nsub=1), 0)

    def count(pred):
        def cb(j, acc):
            c = jnp.where(pred(key_ref[j]), 1.0, 0.0)
            return acc + functools.reduce(jnp.add, [c[:, s:s + LANES] for s in range(0, tk, LANES)])
        acc = lax.fori_loop(0, nblk, cb, jnp.zeros((tq, LANES), F32))
        return jnp.sum(acc, axis=1, keepdims=True)

    cols_per_blk = tk // LANES
    blk_per_grp = 32 // cols_per_blk
    n_grp = (nblk + blk_per_grp - 1) // blk_per_grp

    def fill(j, carry):
        key_ref[j] = jnp.full((tq, tk), INT_MIN, jnp.int32)
        return carry

    lax.fori_loop(nblk, n_grp * blk_per_grp, fill, 0)

    def transpose_group(g, carry):
        w = [key_ref[g * blk_per_grp + c // cols_per_blk, :, (c % cols_per_blk) * LANES:(c % cols_per_blk + 1) * LANES]
             ^ jnp.int32(INT_MIN) for c in range(32)]
        j, m = 16, 0x0000FFFF
        while j:
            k = 0
            while k < 32:
                t = (w[k] ^ (w[k + j] >> j)) & jnp.int32(m)
                w[k] = w[k] ^ t
                w[k + j] = w[k + j] ^ (t << j)
                k = (k + j + 1) & ~j
            j >>= 1
            m ^= (m << j) & 0xFFFFFFFF
        for t in range(32):
            plane_ref[g, t] = w[t]
        tied_ref[g] = jnp.full((tq, LANES), -1, jnp.int32)
        return carry

    lax.fori_loop(0, n_grp, transpose_group, 0)

    def bit_body(t, carry):
        prefix, n_above = carry

        def tally(g, acc):
            return acc + lax.population_count(tied_ref[g] & plane_ref[g, t])

        ones = lax.fori_loop(0, n_grp, tally, jnp.zeros((tq, LANES), jnp.int32))
        cnt = n_above + jnp.sum(ones.astype(F32), axis=1, keepdims=True)
        take = cnt >= topk_f

        def narrow(g, c):
            tied = tied_ref[g]
            hit = tied & plane_ref[g, t]
            tied_ref[g] = jnp.where(take, hit, tied ^ hit)
            return c

        lax.fori_loop(0, n_grp, narrow, 0)
        bit = jnp.left_shift(jnp.int32(1), 31 - t)
        return jnp.where(take, prefix | bit, prefix), jnp.where(take, n_above, cnt)

    prefix, _ = lax.fori_loop(0, 32, bit_body, (jnp.zeros((tq, 1), jnp.int32), jnp.zeros((tq, 1), F32)))
    thr = jnp.maximum(prefix ^ jnp.int32(INT_MIN), jnp.int32(INT_MIN + 1))
    n_ge = count(lambda kb: kb >= thr)
    need = topk_f - count(lambda kb: kb > thr)

    @pl.when(jnp.max(n_ge) > topk_f)
    def _():
        def fix(j, seen):
            kb = key_ref[j]
            eq = jnp.where(kb == thr, 1.0, 0.0)
            rank = _dot(eq.astype(BF), prefix_ref[...]) + seen
            key_ref[j] = jnp.where(eq * rank > need, jnp.int32(INT_MIN), kb)
            return seen + jnp.sum(eq, axis=1, keepdims=True)
        lax.fori_loop(0, nblk, fix, jnp.zeros((tq, 1), F32))

    qm = _masked_pair_heads(q_ref)
    low = _low_half((tq, LANES))

    slabs = [slice(pr * LANES, (pr + 1) * LANES) for pr in range(N_HEADS // 2)]

    def body(t, carry, first, nsub):
        blocks = [first + t * nsub + u for u in range(nsub)]
        starts = [pl.multiple_of(j * tk, tk) for j in blocks]
        raw = [[_dot_nt(qm[h], k_ref[0, pl.ds(ks, tk), slabs[h // 2]]) for h in range(N_HEADS)] for ks in starts]
        selected = [key_ref[j] >= thr for j in blocks]
        logits = [[jnp.where(selected[u], raw[u][h], NEG) for h in range(N_HEADS)] for u in range(nsub)]
        return _softmax_step(carry, logits, lambda u, pr: v_ref[0, pl.ds(starts[u], tk), slabs[pr]], low)

    carry = lax.fori_loop(0, nblk // 2, functools.partial(body, first=0, nsub=2), _softmax_init(tq))
    carry = lax.fori_loop(0, nblk % 2, functools.partial(body, first=nblk - 1, nsub=1), carry)
    _softmax_finish(carry, o_ref, low)


def _mixer_c(q_all, iq, iw, ik_arr, k_arr, v_arr, kv_col, tq, tk, q_start, n_keys):
    b, l, _ = q_all.shape
    kp = k_arr.shape[1]
    topk = min(TOPK_MAX, n_keys // 4)
    prefix = (np.arange(tk)[:, None] <= np.arange(tk)[None, :]).astype(np.float32)
    blk_per_grp = 32 // (tk // LANES)
    n_grp = -(-(kp // tk) // blk_per_grp)
    return pl.pallas_call(
        functools.partial(_mixer_c_kernel, tq=tq, tk=tk, q_start=q_start, n_keys=n_keys, topk=topk),
        grid=(b, l // tq),
        in_specs=[pl.BlockSpec((1, tq, BRANCH_W), lambda bi, i: (bi, i, 2)),
                  pl.BlockSpec((1, tq, H_IDX * D_IDX), lambda bi, i: (bi, i, 0)),
                  pl.BlockSpec((1, tq, H_IDX), lambda bi, i: (bi, i, 0)),
                  _resident_spec(kp, 2 * D_IDX, 0),
                  _kv_spec(kp, kv_col), _kv_spec(kp, kv_col),
                  pl.BlockSpec((tk, tk), lambda bi, i: (0, 0))],
        out_specs=pl.BlockSpec((1, tq, BRANCH_W), lambda bi, i: (bi, i, 0)),
        out_shape=jax.ShapeDtypeStruct((b, l, BRANCH_W), BF),
        scratch_shapes=[pltpu.VMEM((n_grp * blk_per_grp, tq, tk), jnp.int32), pltpu.VMEM((H_IDX, tq, tk), F32),
                        pltpu.VMEM((n_grp, 32, tq, LANES), jnp.int32), pltpu.VMEM((n_grp, tq, LANES), jnp.int32)],
        compiler_params=_cparams("parallel", "parallel"), name="mixer_c",
    )(q_all, iq, iw, ik_arr, k_arr, v_arr, jnp.asarray(prefix, BF))


def _mixer_d_kernel(q_ref, k_ref, v_ref, fq_ref, fk_ref, o_ref, *, tq, tk, q_start, n_kblk, step):
    q0 = q_start + pl.program_id(1) * tq
    nblk = (q0 + tq - 1) // tk + 1
    n_full = (q0 + 1) // tk
    qpos = q0 + lax.broadcasted_iota(jnp.int32, (tq, tk), 0)
    kiota = lax.broadcasted_iota(jnp.int32, (tq, tk), 1)
    qm = _masked_pair_heads(q_ref)
    fq = [fq_ref[0, :, h:h + 1] for h in range(N_HEADS)]
    low = _low_half((tq, LANES))

    slabs = [slice(pr * LANES, (pr + 1) * LANES) for pr in range(N_HEADS // 2)]

    def body(t, carry, masked, first, nsub):
        blocks = [first + t * nsub + u for u in range(nsub)]
        starts = [pl.multiple_of(j * tk, tk) for j in blocks]
        raw = [[_dot_nt(qm[h], k_ref[0, pl.ds(ks, tk), slabs[h // 2]]) for h in range(N_HEADS)] for ks in starts]
        logits = []
        for u in range(nsub):
            row = [raw[u][h] + fq[h] - fk_ref[0, pl.ds(h * n_kblk + blocks[u], 1), :] for h in range(N_HEADS)]
            if masked:
                visible = (starts[u] + kiota) <= qpos
                row = [jnp.where(visible, s, NEG) for s in row]
            logits.append(row)
        return _softmax_step(carry, logits, lambda u, pr: v_ref[0, pl.ds(starts[u], tk), slabs[pr]], low)

    carry = lax.fori_loop(0, n_full // step, functools.partial(body, masked=False, first=0, nsub=step),
                          _softmax_init(tq))
    carry = lax.fori_loop(0, n_full % step, functools.partial(body, masked=False, first=n_full - n_full % step,
                                                              nsub=1), carry)
    carry = lax.fori_loop(0, nblk - n_full, functools.partial(body, masked=True, first=n_full, nsub=1), carry)
    _softmax_finish(carry, o_ref, low)


def _mixer_d(q_all, k_arr, v_arr, kv_col, fq, fk, tq, tk, q_start, step):
    b, l, _ = q_all.shape
    kp = k_arr.shape[1]
    n_kblk = kp // tk
    return pl.pallas_call(
        functools.partial(_mixer_d_kernel, tq=tq, tk=tk, q_start=q_start, n_kblk=n_kblk, step=step),
        grid=(b, l // tq),
        in_specs=[pl.BlockSpec((1, tq, BRANCH_W), lambda bi, i: (bi, i, 3)), _kv_spec(kp, kv_col),
                  _kv_spec(kp, kv_col),
                  pl.BlockSpec((1, tq, N_HEADS), lambda bi, i: (bi, i, 0)),
                  pl.BlockSpec((1, N_HEADS * n_kblk, tk), lambda bi, i: (bi, 0, 0))],
        out_specs=pl.BlockSpec((1, tq, BRANCH_W), lambda bi, i: (bi, i, 0)),
        out_shape=jax.ShapeDtypeStruct((b, l, BRANCH_W), BF),
        compiler_params=_cparams("parallel", "parallel"), name="mixer_d",
    )(q_all, k_arr, v_arr, fq, fk)


def _merge_kernel(x_ref, g1_ref, oa_ref, ob_ref, oc_ref, od_ref, wg_ref, wb_ref, wo_ref, y_ref):
    x = x_ref[...]
    h = _rms(x, g1_ref[...]).astype(BF)
    merged = None
    for g, o_ref in enumerate((oa_ref, ob_ref, oc_ref, od_ref)):
        gate = 1.0 / (1.0 + jnp.exp(-_dot(h, wg_ref[:, g * D_MODEL:(g + 1) * D_MODEL])))
        term = gate * _dot(o_ref[...], wb_ref[g])
        merged = term if merged is None else merged + term
    y_ref[...] = x + _dot(merged.astype(BF), wo_ref[...])


def _merge(x2, outs, lp, tm):
    m_rows = x2.shape[0]
    row = lambda i: (i, 0)
    const = lambda i: (0, 0)
    return pl.pallas_call(
        _merge_kernel, grid=(m_rows // tm,),
        in_specs=[pl.BlockSpec((tm, D_MODEL), row), pl.BlockSpec((1, D_MODEL), const)]
                 + [pl.BlockSpec((tm, BRANCH_W), row)] * N_MIXERS
                 + [pl.BlockSpec((D_MODEL, N_MIXERS * D_MODEL), const),
                    pl.BlockSpec((N_MIXERS, BRANCH_W, D_MODEL), lambda i: (0, 0, 0)),
                    pl.BlockSpec((D_MODEL, D_MODEL), const)],
        out_specs=pl.BlockSpec((tm, D_MODEL), row),
        out_shape=jax.ShapeDtypeStruct((m_rows, D_MODEL), F32),
        compiler_params=_cparams("parallel"), name="merge",
    )(x2, lp["g1"], *outs, lp["w_gate"], lp["w_branch"], lp["w_o"])


def _gelu_tanh(x):
    return x * (0.5 * (1.0 + jnp.tanh(np.sqrt(2.0 / np.pi).astype(np.float32) * (x + 0.044715 * (x * x * x)))))


def _ffn_kernel(x_ref, xp_ref, st_ref, g2_ref, wu_ref, wgt_ref, wc_ref, bc_ref, wd_ref, y_ref, ut_ref,
                h_ref, acc_ref, *, tm, tiles_per_seq):
    i = pl.program_id(0)
    f = pl.program_id(1)

    @pl.when(f == 0)
    def _():
        g2 = g2_ref[...]
        h_ref[:CTX_ROWS, :] = _rms(xp_ref[...], g2).astype(BF)
        h_ref[CTX_ROWS:, :] = _rms(x_ref[...], g2).astype(BF)
        acc_ref[...] = x_ref[...]

    h = h_ref[...]
    u = _dot(h, wu_ref[...])
    gt = _dot(h, wgt_ref[...])[CTX_ROWS:]
    ctx = jnp.where(i % tiles_per_seq == 0, st_ref[0], u[:CTX_ROWS])
    ue = jnp.concatenate([ctx, u[CTX_ROWS:]], axis=0)
    wc = wc_ref[...]
    conv = bc_ref[...] + ((ue[CTX_ROWS - 2:CTX_ROWS - 2 + tm] * wc[0:1] + ue[CTX_ROWS - 1:CTX_ROWS - 1 + tm] * wc[1:2])
                          + ue[CTX_ROWS:] * wc[2:3])
    a = _gelu_tanh(conv) * gt
    acc_ref[...] += _dot(a.astype(BF), wd_ref[...])
    ut_ref[0] = ue[tm:]

    @pl.when(f == pl.num_programs(1) - 1)
    def _():
        y_ref[...] = acc_ref[...]


def _ffn(x2, state8, lp, tm, tf, seq_len):
    m_rows = x2.shape[0]
    b = state8.shape[0]
    tiles_per_seq = seq_len // tm
    n_f = D_FF // tf
    return pl.pallas_call(
        functools.partial(_ffn_kernel, tm=tm, tiles_per_seq=tiles_per_seq),
        grid=(m_rows // tm, n_f),
        in_specs=[pl.BlockSpec((tm, D_MODEL), lambda i, f: (i, 0)),
                  pl.BlockSpec((CTX_ROWS, D_MODEL), lambda i, f: (jnp.maximum(i * (tm // CTX_ROWS) - 1, 0), 0)),
                  pl.BlockSpec((1, CTX_ROWS, tf), lambda i, f: (i // tiles_per_seq, 0, f)),
                  pl.BlockSpec((1, D_MODEL), lambda i, f: (0, 0)),
                  pl.BlockSpec((D_MODEL, tf), lambda i, f: (0, f)),
                  pl.BlockSpec((D_MODEL, tf), lambda i, f: (0, n_f + f)),
                  pl.BlockSpec((CONV_W, tf), lambda i, f: (0, f)),
                  pl.BlockSpec((1, tf), lambda i, f: (0, f)),
                  pl.BlockSpec((tf, D_MODEL), lambda i, f: (f, 0))],
        out_specs=[pl.BlockSpec((tm, D_MODEL), lambda i, f: (i, 0)),
                   pl.BlockSpec((1, CTX_ROWS, tf), lambda i, f: (i, 0, f))],
        out_shape=[jax.ShapeDtypeStruct((m_rows, D_MODEL), F32),
                   jax.ShapeDtypeStruct((m_rows // tm, CTX_ROWS, D_FF), F32)],
        scratch_shapes=[pltpu.VMEM((tm + CTX_ROWS, D_MODEL), BF), pltpu.VMEM((tm, D_MODEL), F32)],
        compiler_params=_cparams("parallel", "arbitrary"), name="ffn",
    )(x2, x2, state8, lp["g2"], lp["w_ffn_in"], lp["w_ffn_in"], lp["w_conv"], lp["b_conv"], lp["w_down"])


def _rope_tables(q_start, length):
    half = HEAD_DIM // 2
    inv = ROPE_THETA ** (-jnp.arange(half, dtype=F32) / half)
    ang = (q_start + jnp.arange(length)).astype(F32)[:, None] * inv[None, :]
    cos = jnp.cos(ang)
    sin = jnp.sin(ang)
    cos_h = jnp.concatenate([cos, cos], axis=1)
    sin_h = jnp.concatenate([-sin, sin], axis=1)
    n_rot = H_IDX + 1
    pad = MISC_W - n_rot * HEAD_DIM
    return {
        "cq": jnp.tile(cos_h, (1, N_HEADS)), "sq": jnp.tile(sin_h, (1, N_HEADS)),
        "cm": jnp.concatenate([jnp.tile(cos_h, (1, n_rot)), jnp.ones((length, pad), F32)], axis=1),
        "sm": jnp.concatenate([jnp.tile(sin_h, (1, n_rot)), jnp.zeros((length, pad), F32)], axis=1),
    }


def _layer_params(norm1_g, w_in, b_forget, qk_g, rel_table, w_branch, w_o, norm2_g, w_ffn_in, w_conv, b_conv,
                  w_down):
    a = 3 * QKV_W
    i1 = a + H_IDX * D_IDX + D_IDX + H_IDX
    d1 = i1 + QKV_W
    f1 = d1 + N_HEADS
    n_misc = (i1 - a) + N_HEADS
    w_misc = jnp.concatenate([w_in[:, a:i1], w_in[:, d1:f1], jnp.zeros((D_MODEL, MISC_W - n_misc), w_in.dtype)],
                             axis=1)
    f_lane = D_IDX + H_IDX
    bf_row = jnp.zeros((1, LANES), F32).at[0, f_lane:f_lane + N_HEADS].set(b_forget.astype(F32))
    head_ones = (np.arange(BRANCH_W)[:, None] // HEAD_DIM == np.arange(BRANCH_W)[None, :] // HEAD_DIM)
    return {
        "g1": norm1_g.astype(F32)[None, :], "g2": norm2_g.astype(F32)[None, :],
        "w_qkv": jnp.concatenate([w_in[:, :a], w_in[:, i1:d1]], axis=1).astype(BF),
        "w_misc": w_misc.astype(BF), "w_gate": w_in[:, f1:].astype(BF),
        "qkg": jnp.tile(qk_g.astype(F32), (1, N_HEADS)), "bf_row": bf_row,
        "head_ones": jnp.asarray(head_ones.astype(np.float32), BF),
        "rel_table": rel_table, "w_branch": w_branch.astype(BF), "w_o": w_o.astype(BF),
        "w_ffn_in": w_ffn_in.astype(BF), "w_conv": w_conv.astype(F32), "b_conv": b_conv.astype(F32)[None, :],
        "w_down": w_down.astype(BF),
    }


def _pad_rows(a, rows):
    return jnp.pad(a, ((0, 0), (0, rows - a.shape[1])) + ((0, 0),) * (a.ndim - 2))


def _round_up(n, m):
    return -(-n // m) * m


def _layer(x, past, lp, tabs):
    b, l, _ = x.shape
    m_rows = b * l
    q_start = 0 if past is None else past[2].shape[1]
    n_keys = q_start + l
    tq = min(256, l)
    tq_c = min(128, l)
    tk = 256
    tk_d = 512
    kp = _round_up(n_keys, max(tk, tk_d))
    tm = min(256, m_rows)

    x2 = x.reshape(m_rows, D_MODEL)
    (q_all, kbf, vbf, ka, va, kb, vb, kc, vc, kd, vd, iq, misc) = _projection(x2, lp, tabs, tm)
    ik = misc[:, :D_IDX].reshape(b, l, D_IDX)
    iw = misc[:, D_IDX:D_IDX + H_IDX].reshape(b, l, H_IDX)
    logf = misc[:, D_IDX + H_IDX:D_IDX + H_IDX + N_HEADS].reshape(b, l, N_HEADS)
    q_all = q_all.reshape(b, l, N_MIXERS * BRANCH_W)
    iq = iq.reshape(b, l, H_IDX * D_IDX)
    heads = lambda t: t.reshape(b, l, N_HEADS, HEAD_DIM)
    flat = lambda t: t.reshape(t.shape[0], t.shape[1], BRANCH_W)

    if past is None:
        kbf3 = kbf.reshape(b, l, N_MIXERS * BRANCH_W)
        vbf3 = vbf.reshape(b, l, N_MIXERS * BRANCH_W)
        kv = [(kbf3, vbf3, c) for c in range(N_MIXERS)]
        ik_all = ik.astype(BF)
        logf_all = logf
        a_args = (kbf3, vbf3, 0, _band_bias(lp["rel_table"], tq, BAND_PAST + tq, BAND_PAST + tq), tq, 3, tq)
        conv_state = jnp.zeros((b, CONV_W - 1, D_FF), F32)
    else:
        (pa_k, pa_v, pb_k, pb_v, pc_k, pc_v, pc_ki, pd_k, pd_v, pd_f, conv_state) = past
        new_bf = lambda t, c: t.reshape(b, l, N_MIXERS, BRANCH_W)[:, :, c]
        cat = lambda p, c, src, rows: _pad_rows(jnp.concatenate([flat(p).astype(BF), new_bf(src, c)], axis=1), rows)
        kv = [None] + [(cat(pk, c, kbf, kp), cat(pv, c, vbf, kp), 0)
                       for c, (pk, pv) in ((1, (pb_k, pb_v)), (2, (pc_k, pc_v)), (3, (pd_k, pd_v)))]
        ik_all = jnp.concatenate([pc_ki.astype(BF), ik.astype(BF)], axis=1)
        logf_all = jnp.concatenate([pd_f.astype(F32), logf], axis=1)
        a_keys = pa_k.shape[1] + l
        wa = _round_up(BAND_PAST + l, LANES)
        a_args = (cat(pa_k, 0, kbf, wa), cat(pa_v, 0, vbf, wa), 0,
                  _band_bias(lp["rel_table"], l, wa, a_keys), l, 1, wa)

    f_cum = jnp.cumsum(logf_all, axis=1) * LOG2E
    fq = f_cum[:, q_start:]
    fk = jnp.moveaxis(_pad_rows(f_cum, kp), 1, 2).reshape(b, N_HEADS * (kp // tk_d), tk_d)

    o_a = _mixer_a(q_all, *a_args)
    o_b = _mixer_b(q_all, *kv[1], tq, tk, q_start)
    ik_pair = _pad_rows(jnp.concatenate([ik_all, ik_all], axis=-1), kp)
    o_c = _mixer_c(q_all, iq, iw, ik_pair, *kv[2], tq_c, tk_d, q_start, n_keys)
    o_d = _mixer_d(q_all, *kv[3], fq, fk, tq, tk_d, q_start, 2)
    x2 = _merge(x2, [o.reshape(m_rows, BRANCH_W) for o in (o_a, o_b, o_c, o_d)], lp, tm)

    state8 = jnp.pad(conv_state.astype(F32), ((0, 0), (CTX_ROWS - (CONV_W - 1), 0), (0, 0)))
    x2, u_tail = _ffn(x2, state8, lp, min(512, l), 256, l)
    u_tail = u_tail.reshape(b, -1, CTX_ROWS, D_FF)[:, -1]

    if past is None:
        a_keep = min(BAND_PAST, l)
        new_a = (heads(ka)[:, -a_keep:], heads(va)[:, -a_keep:])
    else:
        a_keep = pa_k.shape[1]
        new_a = (jnp.concatenate([pa_k, heads(ka)], axis=1)[:, -a_keep:],
                 jnp.concatenate([pa_v, heads(va)], axis=1)[:, -a_keep:])
    new = new_a + (heads(kb), heads(vb), heads(kc), heads(vc), ik, heads(kd), heads(vd), logf,
                   u_tail[:, -(CONV_W - 1):])
    return x2.reshape(b, l, D_MODEL), new


def kernel(x_prompt, x_sample, cache_a_k, cache_a_v, cache_b_k, cache_b_v, cache_c_k, cache_c_v, cache_c_kidx, cache_d_k, cache_d_v, cache_d_logf, state_ffn_conv, norm1_g, w_in, b_forget, qk_norm_g, rel_bias, w_branch, w_o, norm2_g, w_ffn_in, w_conv, b_conv, w_down):
    depth = w_in.shape[0]
    past_len = cache_b_k.shape[2]
    tabs_p = _rope_tables(0, x_prompt.shape[1])
    tabs_s = {k: jnp.tile(v, (x_sample.shape[0], 1)) for k, v in _rope_tables(past_len, x_sample.shape[1]).items()}
    y_p, y_s = x_prompt, x_sample
    p_states, s_states = [], []
    for d in range(depth):
        lp = _layer_params(norm1_g[d], w_in[d], b_forget[d], qk_norm_g[d], rel_bias[d], w_branch[d], w_o[d],
                           norm2_g[d], w_ffn_in[d], w_conv[d], b_conv[d], w_down[d])
        y_p, st_p = _layer(y_p, None, lp, tabs_p)
        past = (cache_a_k[d], cache_a_v[d], cache_b_k[d], cache_b_v[d], cache_c_k[d], cache_c_v[d],
                cache_c_kidx[d], cache_d_k[d], cache_d_v[d], cache_d_logf[d], state_ffn_conv[d])
        y_s, st_s = _layer(y_s, past, lp, tabs_s)
        p_states.append(st_p)
        s_states.append(st_s)
    p_out = [jnp.stack(t) for t in zip(*p_states)]
    s_out = [jnp.stack(t) for t in zip(*s_states)]
    return (y_p, y_s, *p_out, *s_out)
```

```python
import functools

import numpy as np
import jax
import jax.numpy as jnp
from jax import lax
from jax.experimental import pallas as pl
from jax.experimental.pallas import tpu as pltpu

D_MODEL = 1024
CHUNK = 64
N_MIXERS = 4
N_HEADS = 4
HEAD_DIM = 64
BRANCH_W = N_HEADS * HEAD_DIM
BAND_PAST = 8 * CHUNK
REL_CLIP = 128
H_IDX = 4
D_IDX = 64
TOPK_MAX = 256
ROPE_THETA = 10000.0
D_FF = 2816
CONV_W = 3
EPS = 1e-6

BF = jnp.bfloat16
F32 = jnp.float32
NEG = -1e30
LOG2E = float(np.log2(np.e))
F32_UNDERFLOW_LOG2 = 160.0
INT_MIN = -2 ** 31
LANES = 128
SUBLANES = 8
CTX_ROWS = 2 * SUBLANES
QKV_W = 3 * BRANCH_W
MISC_W = 3 * LANES
VMEM_LIMIT = 56 * 1024 * 1024


def _cparams(*sem):
    return pltpu.CompilerParams(dimension_semantics=sem, vmem_limit_bytes=VMEM_LIMIT)


def _dot(a, b):
    return jnp.dot(a, b, preferred_element_type=F32)


def _dot_nt(a, b):
    return lax.dot_general(a, b, (((1,), (1,)), ((), ())), preferred_element_type=F32)


def _split_dot(x, m, terms):
    acc = None
    r = x
    for t in range(terms):
        hi = r.astype(BF)
        d = _dot(hi, m)
        acc = d if acc is None else acc + d
        if t + 1 < terms:
            r = r - hi.astype(F32)
    return acc


def _rms(x, g):
    ms = jnp.mean(x * x, axis=-1, keepdims=True)
    return x * lax.rsqrt(ms + EPS) * g


def _head_rms(t, g, head_ones):
    ms = _split_dot(t * t, head_ones, 3) * (1.0 / HEAD_DIM)
    return t * lax.rsqrt(ms + EPS) * g


def _swap_halves(y):
    n = y.shape[-1]
    lane = lax.broadcasted_iota(jnp.int32, y.shape, 1)
    first = (lane & (HEAD_DIM - 1)) < (HEAD_DIM // 2)
    return jnp.where(first, pltpu.roll(y, n - HEAD_DIM // 2, 1), pltpu.roll(y, HEAD_DIM // 2, 1))


def _softplus2(z):
    neg_abs = pltpu.bitcast(pltpu.bitcast(z, jnp.int32) | jnp.int32(INT_MIN), F32)
    return jnp.maximum(z, 0.0) + jnp.log2(1.0 + jnp.exp2(neg_abs))


def _low_half(shape):
    return lax.broadcasted_iota(jnp.int32, shape, 1) < HEAD_DIM


def _masked_pair_heads(ref):
    out = []
    for pr in range(ref.shape[-1] // LANES):
        slab = ref[0, :, pr * LANES:(pr + 1) * LANES].astype(F32)
        low = _low_half(slab.shape)
        out.append(jnp.where(low, slab, 0.0).astype(BF))
        out.append(jnp.where(low, 0.0, slab).astype(BF))
    return out


def _softmax_init(tq):
    return (tuple(jnp.full((tq, 1), NEG, F32) for _ in range(N_HEADS)),
            tuple(jnp.zeros((tq, 1), F32) for _ in range(N_HEADS)),
            tuple(jnp.zeros((tq, LANES), F32) for _ in range(N_HEADS // 2)))


def _softmax_step(carry, logits, v_tile, low):
    ms, ls, accs = carry
    tiles = range(len(logits))
    ms_new, ls_new, alphas, ps = [], [], [], []
    for h in range(N_HEADS):
        m_new = functools.reduce(jnp.maximum, [ms[h]] + [jnp.max(logits[u][h], axis=1, keepdims=True) for u in tiles])
        alpha = jnp.exp2(ms[h] - m_new)
        p = [jnp.exp2(logits[u][h] - m_new) for u in tiles]
        ms_new.append(m_new)
        ls_new.append(alpha * ls[h] + functools.reduce(jnp.add, [jnp.sum(pu, axis=1, keepdims=True) for pu in p]))
        alphas.append(alpha)
        ps.append([pu.astype(BF) for pu in p])
    pvs = [functools.reduce(jnp.add, [_dot(ps[h][u], v_tile(u, h // 2)) for u in tiles]) for h in range(N_HEADS)]
    accs_new = [jnp.where(low, alphas[2 * pr], alphas[2 * pr + 1]) * accs[pr]
                + jnp.where(low, pvs[2 * pr], pvs[2 * pr + 1]) for pr in range(N_HEADS // 2)]
    return tuple(ms_new), tuple(ls_new), tuple(accs_new)


def _softmax_finish(carry, o_ref, low):
    _, ls, accs = carry
    for pr in range(N_HEADS // 2):
        l_pair = jnp.where(low, ls[2 * pr], ls[2 * pr + 1])
        o_ref[0, :, pr * LANES:(pr + 1) * LANES] = (accs[pr] / l_pair).astype(o_ref.dtype)


def _proj_kernel(x_ref, g1_ref, wqkv_ref, wmisc_ref, qkg_ref, hones_ref, cq_ref, sq_ref, cm_ref, sm_ref,
                 bf_ref, q_ref, kbf_ref, vbf_ref, ka_ref, va_ref, kb_ref, vb_ref, kc_ref, vc_ref,
                 kd_ref, vd_ref, iq_ref, misc_ref):
    h = _rms(x_ref[...], g1_ref[...]).astype(BF)
    hones = hones_ref[...]
    k32 = (ka_ref, kb_ref, kc_ref, kd_ref)
    v32 = (va_ref, vb_ref, vc_ref, vd_ref)
    norm_row = (0, None, 2, 4)
    for m in range(N_MIXERS):
        y = _dot(h, wqkv_ref[:, m * QKV_W:(m + 1) * QKV_W])
        q, k, v = y[:, :BRANCH_W], y[:, BRANCH_W:2 * BRANCH_W], y[:, 2 * BRANCH_W:]
        if norm_row[m] is not None:
            r = norm_row[m]
            q = _head_rms(q, qkg_ref[r:r + 1, :], hones)
            k = _head_rms(k, qkg_ref[r + 1:r + 2, :], hones)
        if m == 2:
            c, s = cq_ref[...], sq_ref[...]
            q = q * c + _swap_halves(q) * s
            k = k * c + _swap_halves(k) * s
        cols = slice(m * BRANCH_W, (m + 1) * BRANCH_W)
        q_ref[:, cols] = (q * (HEAD_DIM ** -0.5 * LOG2E)).astype(BF)
        kbf_ref[:, cols] = k.astype(BF)
        vbf_ref[:, cols] = v.astype(BF)
        k32[m][...] = k
        v32[m][...] = v
    ym = _dot(h, wmisc_ref[...])
    r = ym * cm_ref[...] + _swap_halves(ym) * sm_ref[...]
    iq_ref[...] = r[:, :H_IDX * D_IDX].astype(BF)
    g2 = r[:, H_IDX * D_IDX:]
    lane = lax.broadcasted_iota(jnp.int32, g2.shape, 1)
    z = g2 + bf_ref[...]
    logf = jnp.minimum(z, 0.0) - jnp.log1p(jnp.exp(-jnp.abs(z)))
    is_w = (lane >= D_IDX) & (lane < D_IDX + H_IDX)
    is_f = (lane >= D_IDX + H_IDX) & (lane < D_IDX + H_IDX + N_HEADS)
    misc_ref[...] = jnp.where(is_f, logf, jnp.where(is_w, g2 * (H_IDX ** -0.5 * D_IDX ** -0.5), g2))


def _projection(x2, lp, tabs, tm):
    m_rows = x2.shape[0]
    n_tab = tabs["cq"].shape[0] // tm
    row = lambda i: (i, 0)
    const = lambda i: (0, 0)
    tab = lambda i: (i % n_tab, 0)
    wide = pl.BlockSpec((tm, N_MIXERS * BRANCH_W), row)
    head = pl.BlockSpec((tm, BRANCH_W), row)
    in_specs = [
        pl.BlockSpec((tm, D_MODEL), row),
        pl.BlockSpec((1, D_MODEL), const),
        pl.BlockSpec((D_MODEL, N_MIXERS * QKV_W), const),
        pl.BlockSpec((D_MODEL, MISC_W), const),
        pl.BlockSpec((6, BRANCH_W), const),
        pl.BlockSpec((BRANCH_W, BRANCH_W), const),
        pl.BlockSpec((tm, BRANCH_W), tab),
        pl.BlockSpec((tm, BRANCH_W), tab),
        pl.BlockSpec((tm, MISC_W), tab),
        pl.BlockSpec((tm, MISC_W), tab),
        pl.BlockSpec((1, LANES), const),
    ]
    out_shape = ([jax.ShapeDtypeStruct((m_rows, N_MIXERS * BRANCH_W), BF)] * 3
                 + [jax.ShapeDtypeStruct((m_rows, BRANCH_W), F32)] * 8
                 + [jax.ShapeDtypeStruct((m_rows, BRANCH_W), BF),
                    jax.ShapeDtypeStruct((m_rows, LANES), F32)])
    out_specs = [wide] * 3 + [head] * 8 + [head, pl.BlockSpec((tm, LANES), row)]
    return pl.pallas_call(
        _proj_kernel, grid=(m_rows // tm,), in_specs=in_specs, out_specs=out_specs, out_shape=out_shape,
        compiler_params=_cparams("parallel"), name="projection",
    )(x2, lp["g1"], lp["w_qkv"], lp["w_misc"], lp["qkg"], lp["head_ones"],
      tabs["cq"], tabs["sq"], tabs["cm"], tabs["sm"], lp["bf_row"])


def _mixer_a_kernel(*refs, nwb, wb):
    q_ref = refs[0]
    k_refs = refs[1:1 + nwb]
    v_refs = refs[1 + nwb:1 + 2 * nwb]
    bias_ref = refs[1 + 2 * nwb]
    o_ref = refs[2 + 2 * nwb]
    i = pl.program_id(1)
    for h in range(N_HEADS):
        cols = slice(h * HEAD_DIM, (h + 1) * HEAD_DIM)
        qh = q_ref[0, :, cols]
        logits = []
        for r in range(nwb):
            s = _dot_nt(qh, k_refs[r][0, :, cols]) + bias_ref[h, :, r * wb:(r + 1) * wb]
            logits.append(jnp.where(i - (nwb - 1) + r >= 0, s, NEG))
        m = functools.reduce(jnp.maximum, [jnp.max(s, axis=1, keepdims=True) for s in logits])
        ps = [jnp.exp2(s - m) for s in logits]
        l = functools.reduce(jnp.add, [jnp.sum(p, axis=1, keepdims=True) for p in ps])
        acc = functools.reduce(jnp.add, [_dot(p.astype(BF), v_refs[r][0, :, cols]) for r, p in enumerate(ps)])
        o_ref[0, :, cols] = (acc / l).astype(o_ref.dtype)


def _mixer_a(q_all, k_arr, v_arr, kv_col, bias, tq, nwb, wb):
    b, l, _ = q_all.shape
    kspec = lambda r: pl.BlockSpec((1, wb, BRANCH_W),
                                   lambda bi, i, r=r: (bi, jnp.maximum(i - (nwb - 1) + r, 0), kv_col))
    in_specs = ([pl.BlockSpec((1, tq, BRANCH_W), lambda bi, i: (bi, i, 0))]
                + [kspec(r) for r in range(nwb)] * 2
                + [pl.BlockSpec(bias.shape, lambda bi, i: (0, 0, 0))])
    return pl.pallas_call(
        functools.partial(_mixer_a_kernel, nwb=nwb, wb=wb),
        grid=(b, l // tq), in_specs=in_specs,
        out_specs=pl.BlockSpec((1, tq, BRANCH_W), lambda bi, i: (bi, i, 0)),
        out_shape=jax.ShapeDtypeStruct((b, l, BRANCH_W), BF),
        compiler_params=_cparams("parallel", "parallel"), name="mixer_a",
    )(q_all, *([k_arr] * nwb), *([v_arr] * nwb), bias)


def _band_bias(rel_table, tq, w, n_valid_cols):
    t = np.arange(tq)[:, None]
    c = np.arange(w)[None, :]
    krel = c - BAND_PAST
    ct = t // CHUNK
    inband = (krel >= CHUNK * ct - BAND_PAST) & (krel < CHUNK * ct + CHUNK) & (c < n_valid_cols)
    n = tq + w - 1
    idx = np.clip(np.arange(n) - (w - 1 - BAND_PAST), -REL_CLIP, REL_CLIP) + REL_CLIP
    r0, r1 = int((idx == idx[0]).sum()), int((idx == idx[-1]).sum())
    tab = rel_table.astype(F32) * LOG2E
    f = jnp.concatenate([jnp.broadcast_to(tab[idx[0]], (r0, N_HEADS)), tab[idx[r0]:idx[n - r1 - 1] + 1],
                         jnp.broadcast_to(tab[idx[-1]], (r1, N_HEADS))], axis=0).T
    hankel = jnp.tile(f, (1, tq + 1))[:, :tq * (n + 1)].reshape(N_HEADS, tq, n + 1)[:, :, :w]
    return jnp.where(inband[None], hankel[:, :, ::-1], NEG)


def _mixer_b_kernel(q_ref, k_ref, v_ref, upper_ref, o_ref, *, tq, tk, q_start):
    q0 = q_start + pl.program_id(1) * tq
    nblk = (q0 + tq - 2) // tk + 1
    upper = upper_ref[...]
    qpos = q0 + lax.broadcasted_iota(jnp.int32, (tq, tk), 0)
    kiota = lax.broadcasted_iota(jnp.int32, (tq, tk), 1)
    n_full = q0 // tk
    qm = _masked_pair_heads(q_ref)
    low = _low_half((tq, LANES))

    slabs = [slice(pr * LANES, (pr + 1) * LANES) for pr in range(N_HEADS // 2)]

    def body(t, carry, masked, first, nsub):
        accs, laters = carry
        starts = [pl.multiple_of((first - t * nsub - u) * tk, tk) for u in range(nsub)]
        masks = [(ks + kiota) < qpos for ks in starts]
        zs = [[_dot_nt(qm[h], k_ref[0, pl.ds(ks, tk), slabs[h // 2]]) for h in range(N_HEADS)] for ks in starts]
        log_betas, leaves = [], []
        for u in range(nsub):
            sps = [_softplus2(z) for z in zs[u]]
            leaves.append([jnp.where(masks[u], sp, 0.0) for sp in sps] if masked else sps)
            log_betas.append([z - sp for z, sp in zip(zs[u], sps)])
        tails = [[_dot(s.astype(BF), upper) for s in leaves[u]] for u in range(nsub)]
        laters = list(laters)
        ws = []
        for u in range(nsub):
            ws_u = []
            for h in range(N_HEADS):
                w = jnp.exp2(log_betas[u][h] - (tails[u][h] + laters[h]))
                ws_u.append((jnp.where(masks[u], w, 0.0) if masked else w).astype(BF))
                laters[h] = laters[h] + jnp.sum(leaves[u][h], axis=1, keepdims=True)
            ws.append(ws_u)
        wvs = [functools.reduce(jnp.add, [_dot(ws[u][h], v_ref[0, pl.ds(starts[u], tk), slabs[h // 2]])
                                          for u in range(nsub)]) for h in range(N_HEADS)]
        accs_new = [accs[pr] + jnp.where(low, wvs[2 * pr], wvs[2 * pr + 1]) for pr in range(N_HEADS // 2)]
        return tuple(accs_new), tuple(laters)

    carry = (tuple(jnp.zeros((tq, LANES), F32) for _ in range(N_HEADS // 2)),
             tuple(jnp.zeros((tq, 1), F32) for _ in range(N_HEADS)))
    carry = lax.fori_loop(0, nblk - n_full, functools.partial(body, masked=True, first=nblk - 1, nsub=1), carry)

    def live(carry):
        return jnp.min(functools.reduce(jnp.minimum, carry[1])) < F32_UNDERFLOW_LOG2

    pair = functools.partial(body, masked=False, first=n_full - 1, nsub=2)
    t_end, carry = lax.while_loop(lambda c: (c[0] < n_full // 2) & live(c[1]),
                                  lambda c: (c[0] + 1, pair(c[0], c[1])), (jnp.int32(0), carry))
    last = jnp.where((t_end == n_full // 2) & live(carry), n_full % 2, 0)
    accs, _ = lax.fori_loop(0, last, functools.partial(body, masked=False, first=0, nsub=1), carry)
    for pr in range(N_HEADS // 2):
        o_ref[0, :, pr * LANES:(pr + 1) * LANES] = accs[pr].astype(o_ref.dtype)


def _resident_spec(kp, width, col):
    return pl.BlockSpec((1, kp, width), lambda bi, i: (bi, 0, col), pipeline_mode=pl.Buffered(1))


def _kv_spec(kp, col):
    return _resident_spec(kp, BRANCH_W, col)


def _mixer_b(q_all, k_arr, v_arr, kv_col, tq, tk, q_start):
    b, l, _ = q_all.shape
    kp = k_arr.shape[1]
    upper = (np.arange(tk)[:, None] > np.arange(tk)[None, :]).astype(np.float32)
    return pl.pallas_call(
        functools.partial(_mixer_b_kernel, tq=tq, tk=tk, q_start=q_start),
        grid=(b, l // tq),
        in_specs=[pl.BlockSpec((1, tq, BRANCH_W), lambda bi, i: (bi, i, 1)), _kv_spec(kp, kv_col),
                  _kv_spec(kp, kv_col), pl.BlockSpec((tk, tk), lambda bi, i: (0, 0))],
        out_specs=pl.BlockSpec((1, tq, BRANCH_W), lambda bi, i: (bi, i, 0)),
        out_shape=jax.ShapeDtypeStruct((b, l, BRANCH_W), BF),
        compiler_params=_cparams("parallel", "parallel"), name="mixer_b",
    )(q_all, k_arr, v_arr, jnp.asarray(upper, BF))


def _mixer_c_kernel(q_ref, iq_ref, iw_ref, ik_ref, k_ref, v_ref, prefix_ref, o_ref, key_ref, iwb_ref, plane_ref,
                    tied_ref,
                    *, tq, tk, q_start, n_keys, topk):
    q0 = q_start + pl.program_id(1) * tq
    last_adm = jnp.minimum(((q0 + tq - 1) // CHUNK) * CHUNK + CHUNK - 1, n_keys - 1)
    nblk = last_adm // tk + 1
    qchunk = (q0 + lax.broadcasted_iota(jnp.int32, (tq, tk), 0)) // CHUNK
    kiota = lax.broadcasted_iota(jnp.int32, (tq, tk), 1)
    n_full = jnp.minimum((q0 // CHUNK + 1) * CHUNK, n_keys) // tk
    topk_f = jnp.float32(topk)
    iqm = _masked_pair_heads(iq_ref)
    for g in range(H_IDX):
        iwb_ref[g] = jnp.broadcast_to(iw_ref[0, :, g:g + 1], (tq, tk))

    def score_body(t, carry, masked, first, nsub):
        blocks = [first + t * nsub + u for u in range(nsub)]
        starts = [pl.multiple_of(j * tk, tk) for j in blocks]
        dots = [[_dot_nt(iqm[g], ik_ref[0, pl.ds(ks, tk), :]) for g in range(H_IDX)] for ks in starts]
        for u in range(nsub):
            sc = None
            for g in range(H_IDX):
                term = iwb_ref[g] * jnp.maximum(dots[u][g], 0.0)
                sc = term if sc is None else sc + term
            sc = jnp.where(sc == 0.0, 0.0, sc)
            bits = pltpu.bitcast(sc, jnp.int32)
            key = bits ^ ((bits >> 31) & jnp.int32(0x7FFFFFFF))
            if masked:
                kpos = starts[u] + kiota
                adm = ((kpos // CHUNK) <= qchunk) & (kpos < n_keys)
                key = jnp.where(adm, key, jnp.int32(INT_MIN))
            key_ref[blocks[u]] = key
        return carry

    lax.fori_loop(0, n_full // 2, functools.partial(score_body, masked=False, first=0, nsub=2), 0)
    lax.fori_loop(0, n_full % 2, functools.partial(score_body, masked=False, first=n_full - 1, nsub=1), 0)
    lax.fori_loop(0, nblk - n_full, functools.partial(score_body, masked=True, first=n_full, nsub=1), 0)

    cols_per_blk = tk // LANES
    blk_per_grp = 32 // cols_per_blk
    n_grp = (nblk + blk_per_grp - 1) // blk_per_grp

    def fill(j, carry):
        key_ref[j] = jnp.full((tq, tk), INT_MIN, jnp.int32)
        return carry

    lax.fori_loop(nblk, n_grp * blk_per_grp, fill, 0)

    def transpose_group(g, carry):
        w = [key_ref[g * blk_per_grp + c // cols_per_blk, :, (c % cols_per_blk) * LANES:(c % cols_per_blk + 1) * LANES]
             ^ jnp.int32(INT_MIN) for c in range(32)]
        j, m = 16, 0x0000FFFF
        while j:
            k = 0
            while k < 32:
                t = (w[k] ^ (w[k + j] >> j)) & jnp.int32(m)
                w[k] = w[k] ^ t
                w[k + j] = w[k + j] ^ (t << j)
                k = (k + j + 1) & ~j
            j >>= 1
            m ^= (m << j) & 0xFFFFFFFF
        for t in range(32):
            plane_ref[g, t] = w[t]
        tied_ref[g] = jnp.full((tq, LANES), -1, jnp.int32)
        return carry

    lax.fori_loop(0, n_grp, transpose_group, 0)

    def bit_body(t, carry):
        prefix, n_above = carry

        def tally(g, acc):
            return acc + lax.population_count(tied_ref[g] & plane_ref[g, t])

        ones = lax.fori_loop(0, n_grp, tally, jnp.zeros((tq, LANES), jnp.int32))
        cnt = n_above + jnp.sum(ones.astype(F32), axis=1, keepdims=True)
        take = cnt >= topk_f

        def narrow(g, c):
            tied = tied_ref[g]
            hit = tied & plane_ref[g, t]
            tied_ref[g] = jnp.where(take, hit, tied ^ hit)
            return c

        lax.fori_loop(0, n_grp, narrow, 0)
        bit = jnp.left_shift(jnp.int32(1), 31 - t)
        return jnp.where(take, prefix | bit, prefix), jnp.where(take, n_above, cnt)

    prefix, n_above = lax.fori_loop(0, 32, bit_body, (jnp.zeros((tq, 1), jnp.int32), jnp.zeros((tq, 1), F32)))
    thr = jnp.maximum(prefix ^ jnp.int32(INT_MIN), jnp.int32(INT_MIN + 1))
    n_tied = lax.fori_loop(0, n_grp, lambda g, acc: acc + lax.population_count(tied_ref[g]),
                           jnp.zeros((tq, LANES), jnp.int32))
    n_tied = jnp.sum(n_tied.astype(F32), axis=1, keepdims=True)
    n_ge = jnp.where(prefix == 0, 0.0, n_above + n_tied)
    need = topk_f - n_above

    @pl.when(jnp.max(n_ge) > topk_f)
    def _():
        def fix(j, seen):
            kb = key_ref[j]
            eq = jnp.where(kb == thr, 1.0, 0.0)
            rank = _dot(eq.astype(BF), prefix_ref[...]) + seen
            key_ref[j] = jnp.where(eq * rank > need, jnp.int32(INT_MIN), kb)
            return seen + jnp.sum(eq, axis=1, keepdims=True)
        lax.fori_loop(0, nblk, fix, jnp.zeros((tq, 1), F32))

    qm = _masked_pair_heads(q_ref)
    low = _low_half((tq, LANES))

    slabs = [slice(pr * LANES, (pr + 1) * LANES) for pr in range(N_HEADS // 2)]

    def body(t, carry, first, nsub):
        blocks = [first + t * nsub + u for u in range(nsub)]
        starts = [pl.multiple_of(j * tk, tk) for j in blocks]
        raw = [[_dot_nt(qm[h], k_ref[0, pl.ds(ks, tk), slabs[h // 2]]) for h in range(N_HEADS)] for ks in starts]
        selected = [key_ref[j] >= thr for j in blocks]
        logits = [[jnp.where(selected[u], raw[u][h], NEG) for h in range(N_HEADS)] for u in range(nsub)]
        return _softmax_step(carry, logits, lambda u, pr: v_ref[0, pl.ds(starts[u], tk), slabs[pr]], low)

    carry = lax.fori_loop(0, nblk // 2, functools.partial(body, first=0, nsub=2), _softmax_init(tq))
    carry = lax.fori_loop(0, nblk % 2, functools.partial(body, first=nblk - 1, nsub=1), carry)
    _softmax_finish(carry, o_ref, low)


def _mixer_c(q_all, iq, iw, ik_arr, k_arr, v_arr, kv_col, tq, tk, q_start, n_keys):
    b, l, _ = q_all.shape
    kp = k_arr.shape[1]
    topk = min(TOPK_MAX, n_keys // 4)
    prefix = (np.arange(tk)[:, None] <= np.arange(tk)[None, :]).astype(np.float32)
    blk_per_grp = 32 // (tk // LANES)
    n_grp = -(-(kp // tk) // blk_per_grp)
    return pl.pallas_call(
        functools.partial(_mixer_c_kernel, tq=tq, tk=tk, q_start=q_start, n_keys=n_keys, topk=topk),
        grid=(b, l // tq),
        in_specs=[pl.BlockSpec((1, tq, BRANCH_W), lambda bi, i: (bi, i, 2)),
                  pl.BlockSpec((1, tq, H_IDX * D_IDX), lambda bi, i: (bi, i, 0)),
                  pl.BlockSpec((1, tq, H_IDX), lambda bi, i: (bi, i, 0)),
                  _resident_spec(kp, 2 * D_IDX, 0),
                  _kv_spec(kp, kv_col), _kv_spec(kp, kv_col),
                  pl.BlockSpec((tk, tk), lambda bi, i: (0, 0))],
        out_specs=pl.BlockSpec((1, tq, BRANCH_W), lambda bi, i: (bi, i, 0)),
        out_shape=jax.ShapeDtypeStruct((b, l, BRANCH_W), BF),
        scratch_shapes=[pltpu.VMEM((n_grp * blk_per_grp, tq, tk), jnp.int32), pltpu.VMEM((H_IDX, tq, tk), F32),
                        pltpu.VMEM((n_grp, 32, tq, LANES), jnp.int32), pltpu.VMEM((n_grp, tq, LANES), jnp.int32)],
        compiler_params=_cparams("parallel", "parallel"), name="mixer_c",
    )(q_all, iq, iw, ik_arr, k_arr, v_arr, jnp.asarray(prefix, BF))


def _mixer_d_kernel(q_ref, k_ref, v_ref, fq_ref, fk_ref, kbound_ref, fbound_ref, o_ref,
                    *, tq, tk, q_start, n_kblk, step):
    q0 = q_start + pl.program_id(1) * tq
    nblk = (q0 + tq - 1) // tk + 1
    n_full = (q0 + 1) // tk
    qpos = q0 + lax.broadcasted_iota(jnp.int32, (tq, tk), 0)
    kiota = lax.broadcasted_iota(jnp.int32, (tq, tk), 1)
    qm = _masked_pair_heads(q_ref)
    fq = [fq_ref[0, :, h:h + 1] for h in range(N_HEADS)]
    low = _low_half((tq, LANES))

    slabs = [slice(pr * LANES, (pr + 1) * LANES) for pr in range(N_HEADS // 2)]

    def body(t, carry, masked, first, nsub, sign):
        blocks = [first + sign * (t * nsub + u) for u in range(nsub)]
        starts = [pl.multiple_of(j * tk, tk) for j in blocks]
        raw = [[_dot_nt(qm[h], k_ref[0, pl.ds(ks, tk), slabs[h // 2]]) for h in range(N_HEADS)] for ks in starts]
        logits = []
        for u in range(nsub):
            row = [raw[u][h] + fq[h] - fk_ref[0, pl.ds(h * n_kblk + blocks[u], 1), :] for h in range(N_HEADS)]
            if masked:
                visible = (starts[u] + kiota) <= qpos
                row = [jnp.where(visible, s, NEG) for s in row]
            logits.append(row)
        return _softmax_step(carry, logits, lambda u, pr: v_ref[0, pl.ds(starts[u], tk), slabs[pr]], low)

    q_norm = [1.001 * jnp.sqrt(jnp.sum(jnp.square(qm[h].astype(F32)), axis=1, keepdims=True)) for h in range(N_HEADS)]
    bi = pl.program_id(0)

    def live(carry, j):
        gap = [q_norm[h] * kbound_ref[bi, h * n_kblk + j] + fq[h] + fbound_ref[bi, h * n_kblk + j] - carry[0][h]
               for h in range(N_HEADS)]
        return jnp.max(functools.reduce(jnp.maximum, gap)) > -(F32_UNDERFLOW_LOG2 + 8.0)

    carry = lax.fori_loop(0, nblk - n_full, functools.partial(body, masked=True, first=n_full, nsub=1, sign=1),
                          _softmax_init(tq))
    group = functools.partial(body, masked=False, first=n_full - 1, nsub=step, sign=-1)
    t_end, carry = lax.while_loop(
        lambda c: (c[0] < n_full // step) & live(c[1], jnp.maximum(n_full - 1 - c[0] * step, 0)),
        lambda c: (c[0] + 1, group(c[0], c[1])), (jnp.int32(0), carry))
    n_rest = n_full % step
    rest = jnp.where((t_end == n_full // step) & live(carry, jnp.maximum(n_rest - 1, 0)), n_rest, 0)
    carry = lax.fori_loop(0, rest, functools.partial(body, masked=False, first=n_rest - 1, nsub=1, sign=-1), carry)
    _softmax_finish(carry, o_ref, low)


def _mixer_d(q_all, k_arr, v_arr, kv_col, fq, fk, tq, tk, q_start, step):
    b, l, _ = q_all.shape
    kp = k_arr.shape[1]
    n_kblk = kp // tk
    kd = k_arr[:, :, kv_col * BRANCH_W:(kv_col + 1) * BRANCH_W].astype(F32).reshape(b, n_kblk, tk, N_HEADS, HEAD_DIM)
    k_norm = 1.001 * jnp.sqrt(jnp.max(jnp.sum(kd * kd, axis=-1), axis=2))
    kbound = jnp.moveaxis(lax.cummax(k_norm, axis=1), 1, 2).reshape(b, N_HEADS * n_kblk)
    fbound = lax.cummax(jnp.max(-fk, axis=-1).reshape(b, N_HEADS, n_kblk), axis=2).reshape(b, N_HEADS * n_kblk)
    smem = pl.BlockSpec(memory_space=pltpu.SMEM)
    return pl.pallas_call(
        functools.partial(_mixer_d_kernel, tq=tq, tk=tk, q_start=q_start, n_kblk=n_kblk, step=step),
        grid=(b, l // tq),
        in_specs=[pl.BlockSpec((1, tq, BRANCH_W), lambda bi, i: (bi, i, 3)), _kv_spec(kp, kv_col),
                  _kv_spec(kp, kv_col),
                  pl.BlockSpec((1, tq, N_HEADS), lambda bi, i: (bi, i, 0)),
                  pl.BlockSpec((1, N_HEADS * n_kblk, tk), lambda bi, i: (bi, 0, 0)), smem, smem],
        out_specs=pl.BlockSpec((1, tq, BRANCH_W), lambda bi, i: (bi, i, 0)),
        out_shape=jax.ShapeDtypeStruct((b, l, BRANCH_W), BF),
        compiler_params=_cparams("parallel", "parallel"), name="mixer_d",
    )(q_all, k_arr, v_arr, fq, fk, kbound, fbound)


def _merge_kernel(x_ref, g1_ref, oa_ref, ob_ref, oc_ref, od_ref, wg_ref, wb_ref, wo_ref, y_ref):
    x = x_ref[...]
    h = _rms(x, g1_ref[...]).astype(BF)
    merged = None
    for g, o_ref in enumerate((oa_ref, ob_ref, oc_ref, od_ref)):
        gate = 1.0 / (1.0 + jnp.exp(-_dot(h, wg_ref[:, g * D_MODEL:(g + 1) * D_MODEL])))
        term = gate * _dot(o_ref[...], wb_ref[g])
        merged = term if merged is None else merged + term
    y_ref[...] = x + _dot(merged.astype(BF), wo_ref[...])


def _merge(x2, outs, lp, tm):
    m_rows = x2.shape[0]
    row = lambda i: (i, 0)
    const = lambda i: (0, 0)
    return pl.pallas_call(
        _merge_kernel, grid=(m_rows // tm,),
        in_specs=[pl.BlockSpec((tm, D_MODEL), row), pl.BlockSpec((1, D_MODEL), const)]
                 + [pl.BlockSpec((tm, BRANCH_W), row)] * N_MIXERS
                 + [pl.BlockSpec((D_MODEL, N_MIXERS * D_MODEL), const),
                    pl.BlockSpec((N_MIXERS, BRANCH_W, D_MODEL), lambda i: (0, 0, 0)),
                    pl.BlockSpec((D_MODEL, D_MODEL), const)],
        out_specs=pl.BlockSpec((tm, D_MODEL), row),
        out_shape=jax.ShapeDtypeStruct((m_rows, D_MODEL), F32),
        compiler_params=_cparams("parallel"), name="merge",
    )(x2, lp["g1"], *outs, lp["w_gate"], lp["w_branch"], lp["w_o"])


def _gelu_tanh(x):
    return x * (0.5 * (1.0 + jnp.tanh(np.sqrt(2.0 / np.pi).astype(np.float32) * (x + 0.044715 * (x * x * x)))))


def _ffn_kernel(x_ref, xp_ref, st_ref, g2_ref, win_ref, wc_ref, bc_ref, wd_ref, y_ref, ut_ref, h_ref, a_ref,
                *, tm, tf, tiles_per_seq):
    first_tile = pl.program_id(0) % tiles_per_seq == 0
    g2 = g2_ref[...]
    x = x_ref[...]
    h_ref[:CTX_ROWS, :] = _rms(xp_ref[...], g2).astype(BF)
    h_ref[CTX_ROWS:, :] = _rms(x, g2).astype(BF)
    h = h_ref[...]
    for f in range(0, D_FF, tf):
        u = _dot(h, win_ref[:, f:f + tf])
        gt = _dot(h, win_ref[:, D_FF + f:D_FF + f + tf])[CTX_ROWS:]
        ctx = jnp.where(first_tile, st_ref[0, :, f:f + tf], u[:CTX_ROWS])
        ue = jnp.concatenate([ctx, u[CTX_ROWS:]], axis=0)
        wc = wc_ref[:, f:f + tf]
        conv = bc_ref[:, f:f + tf] + ((ue[CTX_ROWS - 2:CTX_ROWS - 2 + tm] * wc[0:1]
                                       + ue[CTX_ROWS - 1:CTX_ROWS - 1 + tm] * wc[1:2]) + ue[CTX_ROWS:] * wc[2:3])
        a_ref[:, f:f + tf] = (_gelu_tanh(conv) * gt).astype(BF)
        ut_ref[0, :, f:f + tf] = ue[tm:]
    y_ref[...] = x + _dot(a_ref[...], wd_ref[...])


def _ffn(x2, state8, lp, tm, tf, seq_len):
    m_rows = x2.shape[0]
    tiles_per_seq = seq_len // tm
    const = lambda i: (0, 0)
    once = pl.Buffered(1)
    return pl.pallas_call(
        functools.partial(_ffn_kernel, tm=tm, tf=tf, tiles_per_seq=tiles_per_seq),
        grid=(m_rows // tm,),
        in_specs=[pl.BlockSpec((tm, D_MODEL), lambda i: (i, 0)),
                  pl.BlockSpec((CTX_ROWS, D_MODEL), lambda i: (jnp.maximum(i * (tm // CTX_ROWS) - 1, 0), 0)),
                  pl.BlockSpec((1, CTX_ROWS, D_FF), lambda i: (i // tiles_per_seq, 0, 0)),
                  pl.BlockSpec((1, D_MODEL), const),
                  pl.BlockSpec((D_MODEL, 2 * D_FF), const, pipeline_mode=once),
                  pl.BlockSpec((CONV_W, D_FF), const),
                  pl.BlockSpec((1, D_FF), const),
                  pl.BlockSpec((D_FF, D_MODEL), const, pipeline_mode=once)],
        out_specs=[pl.BlockSpec((tm, D_MODEL), lambda i: (i, 0)),
                   pl.BlockSpec((1, CTX_ROWS, D_FF), lambda i: (i, 0, 0))],
        out_shape=[jax.ShapeDtypeStruct((m_rows, D_MODEL), F32),
                   jax.ShapeDtypeStruct((m_rows // tm, CTX_ROWS, D_FF), F32)],
        scratch_shapes=[pltpu.VMEM((tm + CTX_ROWS, D_MODEL), BF), pltpu.VMEM((tm, D_FF), BF)],
        compiler_params=_cparams("parallel"), name="ffn",
    )(x2, x2, state8, lp["g2"], lp["w_ffn_in"], lp["w_conv"], lp["b_conv"], lp["w_down"])


def _rope_tables(q_start, length):
    half = HEAD_DIM // 2
    inv = ROPE_THETA ** (-jnp.arange(half, dtype=F32) / half)
    ang = (q_start + jnp.arange(length)).astype(F32)[:, None] * inv[None, :]
    cos = jnp.cos(ang)
    sin = jnp.sin(ang)
    cos_h = jnp.concatenate([cos, cos], axis=1)
    sin_h = jnp.concatenate([-sin, sin], axis=1)
    n_rot = H_IDX + 1
    pad = MISC_W - n_rot * HEAD_DIM
    return {
        "cq": jnp.tile(cos_h, (1, N_HEADS)), "sq": jnp.tile(sin_h, (1, N_HEADS)),
        "cm": jnp.concatenate([jnp.tile(cos_h, (1, n_rot)), jnp.ones((length, pad), F32)], axis=1),
        "sm": jnp.concatenate([jnp.tile(sin_h, (1, n_rot)), jnp.zeros((length, pad), F32)], axis=1),
    }


def _layer_params(norm1_g, w_in, b_forget, qk_g, rel_table, w_branch, w_o, norm2_g, w_ffn_in, w_conv, b_conv,
                  w_down):
    a = 3 * QKV_W
    i1 = a + H_IDX * D_IDX + D_IDX + H_IDX
    d1 = i1 + QKV_W
    f1 = d1 + N_HEADS
    n_misc = (i1 - a) + N_HEADS
    w_misc = jnp.concatenate([w_in[:, a:i1], w_in[:, d1:f1], jnp.zeros((D_MODEL, MISC_W - n_misc), w_in.dtype)],
                             axis=1)
    f_lane = D_IDX + H_IDX
    bf_row = jnp.zeros((1, LANES), F32).at[0, f_lane:f_lane + N_HEADS].set(b_forget.astype(F32))
    head_ones = (np.arange(BRANCH_W)[:, None] // HEAD_DIM == np.arange(BRANCH_W)[None, :] // HEAD_DIM)
    return {
        "g1": norm1_g.astype(F32)[None, :], "g2": norm2_g.astype(F32)[None, :],
        "w_qkv": jnp.concatenate([w_in[:, :a], w_in[:, i1:d1]], axis=1).astype(BF),
        "w_misc": w_misc.astype(BF), "w_gate": w_in[:, f1:].astype(BF),
        "qkg": jnp.tile(qk_g.astype(F32), (1, N_HEADS)), "bf_row": bf_row,
        "head_ones": jnp.asarray(head_ones.astype(np.float32), BF),
        "rel_table": rel_table, "w_branch": w_branch.astype(BF), "w_o": w_o.astype(BF),
        "w_ffn_in": w_ffn_in.astype(BF), "w_conv": w_conv.astype(F32), "b_conv": b_conv.astype(F32)[None, :],
        "w_down": w_down.astype(BF),
    }


def _pad_rows(a, rows):
    return jnp.pad(a, ((0, 0), (0, rows - a.shape[1])) + ((0, 0),) * (a.ndim - 2))


def _round_up(n, m):
    return -(-n // m) * m


def _layer(x, past, lp, tabs):
    b, l, _ = x.shape
    m_rows = b * l
    q_start = 0 if past is None else past[2].shape[1]
    n_keys = q_start + l
    tq = min(256, l)
    tq_c = min(128, l)
    tk = 256
    tk_d = 512
    kp = _round_up(n_keys, max(tk, tk_d))
    tm = min(256, m_rows)

    x2 = x.reshape(m_rows, D_MODEL)
    (q_all, kbf, vbf, ka, va, kb, vb, kc, vc, kd, vd, iq, misc) = _projection(x2, lp, tabs, tm)
    ik = misc[:, :D_IDX].reshape(b, l, D_IDX)
    iw = misc[:, D_IDX:D_IDX + H_IDX].reshape(b, l, H_IDX)
    logf = misc[:, D_IDX + H_IDX:D_IDX + H_IDX + N_HEADS].reshape(b, l, N_HEADS)
    q_all = q_all.reshape(b, l, N_MIXERS * BRANCH_W)
    iq = iq.reshape(b, l, H_IDX * D_IDX)
    heads = lambda t: t.reshape(b, l, N_HEADS, HEAD_DIM)
    flat = lambda t: t.reshape(t.shape[0], t.shape[1], BRANCH_W)

    if past is None:
        kbf3 = kbf.reshape(b, l, N_MIXERS * BRANCH_W)
        vbf3 = vbf.reshape(b, l, N_MIXERS * BRANCH_W)
        kv = [(kbf3, vbf3, c) for c in range(N_MIXERS)]
        ik_all = ik.astype(BF)
        logf_all = logf
        a_args = (kbf3, vbf3, 0, _band_bias(lp["rel_table"], tq, BAND_PAST + tq, BAND_PAST + tq), tq, 3, tq)
        conv_state = jnp.zeros((b, CONV_W - 1, D_FF), F32)
    else:
        (pa_k, pa_v, pb_k, pb_v, pc_k, pc_v, pc_ki, pd_k, pd_v, pd_f, conv_state) = past
        new_bf = lambda t, c: t.reshape(b, l, N_MIXERS, BRANCH_W)[:, :, c]
        cat = lambda p, c, src, rows: _pad_rows(jnp.concatenate([flat(p).astype(BF), new_bf(src, c)], axis=1), rows)
        kv = [None] + [(cat(pk, c, kbf, kp), cat(pv, c, vbf, kp), 0)
                       for c, (pk, pv) in ((1, (pb_k, pb_v)), (2, (pc_k, pc_v)), (3, (pd_k, pd_v)))]
        ik_all = jnp.concatenate([pc_ki.astype(BF), ik.astype(BF)], axis=1)
        logf_all = jnp.concatenate([pd_f.astype(F32), logf], axis=1)
        a_keys = pa_k.shape[1] + l
        wa = _round_up(BAND_PAST + l, LANES)
        a_args = (cat(pa_k, 0, kbf, wa), cat(pa_v, 0, vbf, wa), 0,
                  _band_bias(lp["rel_table"], l, wa, a_keys), l, 1, wa)

    f_cum = jnp.cumsum(logf_all, axis=1) * LOG2E
    fq = f_cum[:, q_start:]
    fk = jnp.moveaxis(_pad_rows(f_cum, kp), 1, 2).reshape(b, N_HEADS * (kp // tk_d), tk_d)

    o_a = _mixer_a(q_all, *a_args)
    o_b = _mixer_b(q_all, *kv[1], tq, tk, q_start)
    ik_pair = _pad_rows(jnp.concatenate([ik_all, ik_all], axis=-1), kp)
    o_c = _mixer_c(q_all, iq, iw, ik_pair, *kv[2], tq_c, tk_d, q_start, n_keys)
    o_d = _mixer_d(q_all, *kv[3], fq, fk, tq, tk_d, q_start, 2)
    x2 = _merge(x2, [o.reshape(m_rows, BRANCH_W) for o in (o_a, o_b, o_c, o_d)], lp, tm)

    state8 = jnp.pad(conv_state.astype(F32), ((0, 0), (CTX_ROWS - (CONV_W - 1), 0), (0, 0)))
    x2, u_tail = _ffn(x2, state8, lp, min(512, l), 256, l)
    u_tail = u_tail.reshape(b, -1, CTX_ROWS, D_FF)[:, -1]

    if past is None:
        a_keep = min(BAND_PAST, l)
        new_a = (heads(ka)[:, -a_keep:], heads(va)[:, -a_keep:])
    else:
        a_keep = pa_k.shape[1]
        new_a = (jnp.concatenate([pa_k, heads(ka)], axis=1)[:, -a_keep:],
                 jnp.concatenate([pa_v, heads(va)], axis=1)[:, -a_keep:])
    new = new_a + (heads(kb), heads(vb), heads(kc), heads(vc), ik, heads(kd), heads(vd), logf,
                   u_tail[:, -(CONV_W - 1):])
    return x2.reshape(b, l, D_MODEL), new


def kernel(x_prompt, x_sample, cache_a_k, cache_a_v, cache_b_k, cache_b_v, cache_c_k, cache_c_v, cache_c_kidx, cache_d_k, cache_d_v, cache_d_logf, state_ffn_conv, norm1_g, w_in, b_forget, qk_norm_g, rel_bias, w_branch, w_o, norm2_g, w_ffn_in, w_conv, b_conv, w_down):
    depth = w_in.shape[0]
    past_len = cache_b_k.shape[2]
    tabs_p = _rope_tables(0, x_prompt.shape[1])
    tabs_s = {k: jnp.tile(v, (x_sample.shape[0], 1)) for k, v in _rope_tables(past_len, x_sample.shape[1]).items()}
    y_p, y_s = x_prompt, x_sample
    p_states, s_states = [], []
    for d in range(depth):
        lp = _layer_params(norm1_g[d], w_in[d], b_forget[d], qk_norm_g[d], rel_bias[d], w_branch[d], w_o[d],
                           norm2_g[d], w_ffn_in[d], w_conv[d], b_conv[d], w_down[d])
        y_p, st_p = _layer(y_p, None, lp, tabs_p)
        past = (cache_a_k[d], cache_a_v[d], cache_b_k[d], cache_b_v[d], cache_c_k[d], cache_c_v[d],
                cache_c_kidx[d], cache_d_k[d], cache_d_v[d], cache_d_logf[d], state_ffn_conv[d])
        y_s, st_s = _layer(y_s, past, lp, tabs_s)
        p_states.append(st_p)
        s_states.append(st_s)
    p_out = [jnp.stack(t) for t in zip(*p_states)]
    s_out = [jnp.stack(t) for t in zip(*s_states)]
    return (y_p, y_s, *p_out, *s_out)
```

```python
import functools

import numpy as np
import jax
import jax.numpy as jnp
from jax import lax
from jax.experimental import pallas as pl
from jax.experimental.pallas import tpu as pltpu

D_MODEL = 1024
CHUNK = 64
N_MIXERS = 4
N_HEADS = 4
HEAD_DIM = 64
BRANCH_W = N_HEADS * HEAD_DIM
BAND_PAST = 8 * CHUNK
REL_CLIP = 128
H_IDX = 4
D_IDX = 64
TOPK_MAX = 256
ROPE_THETA = 10000.0
D_FF = 2816
CONV_W = 3
EPS = 1e-6

BF = jnp.bfloat16
F32 = jnp.float32
NEG = -1e30
LOG2E = float(np.log2(np.e))
F32_UNDERFLOW_LOG2 = 160.0
INT_MIN = -2 ** 31
LANES = 128
SUBLANES = 8
CTX_ROWS = 2 * SUBLANES
QKV_W = 3 * BRANCH_W
MISC_W = 3 * LANES
VMEM_LIMIT = 56 * 1024 * 1024


def _cparams(*sem):
    return pltpu.CompilerParams(dimension_semantics=sem, vmem_limit_bytes=VMEM_LIMIT)


def _dot(a, b):
    return jnp.dot(a, b, preferred_element_type=F32)


def _dot_nt(a, b):
    return lax.dot_general(a, b, (((1,), (1,)), ((), ())), preferred_element_type=F32)


def _split_dot(x, m, terms):
    acc = None
    r = x
    for t in range(terms):
        hi = r.astype(BF)
        d = _dot(hi, m)
        acc = d if acc is None else acc + d
        if t + 1 < terms:
            r = r - hi.astype(F32)
    return acc


def _rms(x, g):
    ms = jnp.mean(x * x, axis=-1, keepdims=True)
    return x * lax.rsqrt(ms + EPS) * g


def _head_rms(t, g, head_ones):
    ms = _split_dot(t * t, head_ones, 3) * (1.0 / HEAD_DIM)
    return t * lax.rsqrt(ms + EPS) * g


def _swap_halves(y):
    n = y.shape[-1]
    lane = lax.broadcasted_iota(jnp.int32, y.shape, 1)
    first = (lane & (HEAD_DIM - 1)) < (HEAD_DIM // 2)
    return jnp.where(first, pltpu.roll(y, n - HEAD_DIM // 2, 1), pltpu.roll(y, HEAD_DIM // 2, 1))


def _softplus2(z):
    neg_abs = pltpu.bitcast(pltpu.bitcast(z, jnp.int32) | jnp.int32(INT_MIN), F32)
    return jnp.maximum(z, 0.0) + jnp.log2(1.0 + jnp.exp2(neg_abs))


def _tree_sum(xs):
    while len(xs) > 1:
        xs = [xs[i] + xs[i + 1] for i in range(0, len(xs) - 1, 2)] + ([xs[-1]] if len(xs) % 2 else [])
    return xs[0]


def _low_half(shape):
    return lax.broadcasted_iota(jnp.int32, shape, 1) < HEAD_DIM


def _masked_pair_heads(ref):
    out = []
    for pr in range(ref.shape[-1] // LANES):
        slab = ref[0, :, pr * LANES:(pr + 1) * LANES].astype(F32)
        low = _low_half(slab.shape)
        out.append(jnp.where(low, slab, 0.0).astype(BF))
        out.append(jnp.where(low, 0.0, slab).astype(BF))
    return out


def _softmax_init(tq):
    return (tuple(jnp.full((tq, 1), NEG, F32) for _ in range(N_HEADS)),
            tuple(jnp.zeros((tq, 1), F32) for _ in range(N_HEADS)),
            tuple(jnp.zeros((tq, LANES), F32) for _ in range(N_HEADS // 2)))


def _softmax_step(carry, logits, v_tile, low):
    ms, ls, accs = carry
    tiles = range(len(logits))
    ms_new, ls_new, alphas, ps = [], [], [], []
    for h in range(N_HEADS):
        m_new = functools.reduce(jnp.maximum, [ms[h]] + [jnp.max(logits[u][h], axis=1, keepdims=True) for u in tiles])
        alpha = jnp.exp2(ms[h] - m_new)
        p = [jnp.exp2(logits[u][h] - m_new) for u in tiles]
        ms_new.append(m_new)
        ls_new.append(alpha * ls[h] + functools.reduce(jnp.add, [jnp.sum(pu, axis=1, keepdims=True) for pu in p]))
        alphas.append(alpha)
        ps.append([pu.astype(BF) for pu in p])
    pvs = [functools.reduce(jnp.add, [_dot(ps[h][u], v_tile(u, h // 2)) for u in tiles]) for h in range(N_HEADS)]
    accs_new = [jnp.where(low, alphas[2 * pr], alphas[2 * pr + 1]) * accs[pr]
                + jnp.where(low, pvs[2 * pr], pvs[2 * pr + 1]) for pr in range(N_HEADS // 2)]
    return tuple(ms_new), tuple(ls_new), tuple(accs_new)


def _softmax_finish(carry, o_ref, low):
    _, ls, accs = carry
    for pr in range(N_HEADS // 2):
        l_pair = jnp.where(low, ls[2 * pr], ls[2 * pr + 1])
        o_ref[0, :, pr * LANES:(pr + 1) * LANES] = (accs[pr] / l_pair).astype(o_ref.dtype)


def _proj_kernel(x_ref, g1_ref, wqkv_ref, wmisc_ref, qkg_ref, hones_ref, cq_ref, sq_ref, cm_ref, sm_ref,
                 bf_ref, q_ref, kbf_ref, vbf_ref, ka_ref, va_ref, kb_ref, vb_ref, kc_ref, vc_ref,
                 kd_ref, vd_ref, iq_ref, misc_ref):
    h = _rms(x_ref[...], g1_ref[...]).astype(BF)
    hones = hones_ref[...]
    k32 = (ka_ref, kb_ref, kc_ref, kd_ref)
    v32 = (va_ref, vb_ref, vc_ref, vd_ref)
    norm_row = (0, None, 2, 4)
    for m in range(N_MIXERS):
        y = _dot(h, wqkv_ref[:, m * QKV_W:(m + 1) * QKV_W])
        q, k, v = y[:, :BRANCH_W], y[:, BRANCH_W:2 * BRANCH_W], y[:, 2 * BRANCH_W:]
        if norm_row[m] is not None:
            r = norm_row[m]
            q = _head_rms(q, qkg_ref[r:r + 1, :], hones)
            k = _head_rms(k, qkg_ref[r + 1:r + 2, :], hones)
        if m == 2:
            c, s = cq_ref[...], sq_ref[...]
            q = q * c + _swap_halves(q) * s
            k = k * c + _swap_halves(k) * s
        cols = slice(m * BRANCH_W, (m + 1) * BRANCH_W)
        q_ref[:, cols] = (q * (HEAD_DIM ** -0.5 * LOG2E)).astype(BF)
        kbf_ref[:, cols] = k.astype(BF)
        vbf_ref[:, cols] = v.astype(BF)
        k32[m][...] = k
        v32[m][...] = v
    ym = _dot(h, wmisc_ref[...])
    r = ym * cm_ref[...] + _swap_halves(ym) * sm_ref[...]
    iq_ref[...] = r[:, :H_IDX * D_IDX].astype(BF)
    g2 = r[:, H_IDX * D_IDX:]
    lane = lax.broadcasted_iota(jnp.int32, g2.shape, 1)
    z = g2 + bf_ref[...]
    logf = jnp.minimum(z, 0.0) - jnp.log1p(jnp.exp(-jnp.abs(z)))
    is_w = (lane >= D_IDX) & (lane < D_IDX + H_IDX)
    is_f = (lane >= D_IDX + H_IDX) & (lane < D_IDX + H_IDX + N_HEADS)
    misc_ref[...] = jnp.where(is_f, logf, jnp.where(is_w, g2 * (H_IDX ** -0.5 * D_IDX ** -0.5), g2))


def _projection(x2, lp, tabs, tm):
    m_rows = x2.shape[0]
    n_tab = tabs["cq"].shape[0] // tm
    row = lambda i: (i, 0)
    const = lambda i: (0, 0)
    tab = lambda i: (i % n_tab, 0)
    wide = pl.BlockSpec((tm, N_MIXERS * BRANCH_W), row)
    head = pl.BlockSpec((tm, BRANCH_W), row)
    in_specs = [
        pl.BlockSpec((tm, D_MODEL), row),
        pl.BlockSpec((1, D_MODEL), const),
        pl.BlockSpec((D_MODEL, N_MIXERS * QKV_W), const),
        pl.BlockSpec((D_MODEL, MISC_W), const),
        pl.BlockSpec((6, BRANCH_W), const),
        pl.BlockSpec((BRANCH_W, BRANCH_W), const),
        pl.BlockSpec((tm, BRANCH_W), tab),
        pl.BlockSpec((tm, BRANCH_W), tab),
        pl.BlockSpec((tm, MISC_W), tab),
        pl.BlockSpec((tm, MISC_W), tab),
        pl.BlockSpec((1, LANES), const),
    ]
    out_shape = ([jax.ShapeDtypeStruct((m_rows, N_MIXERS * BRANCH_W), BF)] * 3
                 + [jax.ShapeDtypeStruct((m_rows, BRANCH_W), F32)] * 8
                 + [jax.ShapeDtypeStruct((m_rows, BRANCH_W), BF),
                    jax.ShapeDtypeStruct((m_rows, LANES), F32)])
    out_specs = [wide] * 3 + [head] * 8 + [head, pl.BlockSpec((tm, LANES), row)]
    return pl.pallas_call(
        _proj_kernel, grid=(m_rows // tm,), in_specs=in_specs, out_specs=out_specs, out_shape=out_shape,
        compiler_params=_cparams("parallel"), name="projection",
    )(x2, lp["g1"], lp["w_qkv"], lp["w_misc"], lp["qkg"], lp["head_ones"],
      tabs["cq"], tabs["sq"], tabs["cm"], tabs["sm"], lp["bf_row"])


def _mixer_a_kernel(*refs, nwb, wb):
    q_ref = refs[0]
    k_refs = refs[1:1 + nwb]
    v_refs = refs[1 + nwb:1 + 2 * nwb]
    bias_ref = refs[1 + 2 * nwb]
    o_ref = refs[2 + 2 * nwb]
    i = pl.program_id(1)
    for h in range(N_HEADS):
        cols = slice(h * HEAD_DIM, (h + 1) * HEAD_DIM)
        qh = q_ref[0, :, cols]
        logits = []
        for r in range(nwb):
            s = _dot_nt(qh, k_refs[r][0, :, cols]) + bias_ref[h, :, r * wb:(r + 1) * wb]
            logits.append(jnp.where(i - (nwb - 1) + r >= 0, s, NEG))
        m = functools.reduce(jnp.maximum, [jnp.max(s, axis=1, keepdims=True) for s in logits])
        ps = [jnp.exp2(s - m) for s in logits]
        l = functools.reduce(jnp.add, [jnp.sum(p, axis=1, keepdims=True) for p in ps])
        acc = functools.reduce(jnp.add, [_dot(p.astype(BF), v_refs[r][0, :, cols]) for r, p in enumerate(ps)])
        o_ref[0, :, cols] = (acc / l).astype(o_ref.dtype)


def _mixer_a(q_all, k_arr, v_arr, kv_col, bias, tq, nwb, wb):
    b, l, _ = q_all.shape
    kspec = lambda r: pl.BlockSpec((1, wb, BRANCH_W),
                                   lambda bi, i, r=r: (bi, jnp.maximum(i - (nwb - 1) + r, 0), kv_col))
    in_specs = ([pl.BlockSpec((1, tq, BRANCH_W), lambda bi, i: (bi, i, 0))]
                + [kspec(r) for r in range(nwb)] * 2
                + [pl.BlockSpec(bias.shape, lambda bi, i: (0, 0, 0))])
    return pl.pallas_call(
        functools.partial(_mixer_a_kernel, nwb=nwb, wb=wb),
        grid=(b, l // tq), in_specs=in_specs,
        out_specs=pl.BlockSpec((1, tq, BRANCH_W), lambda bi, i: (bi, i, 0)),
        out_shape=jax.ShapeDtypeStruct((b, l, BRANCH_W), BF),
        compiler_params=_cparams("parallel", "parallel"), name="mixer_a",
    )(q_all, *([k_arr] * nwb), *([v_arr] * nwb), bias)


def _band_bias(rel_table, tq, w, n_valid_cols):
    t = np.arange(tq)[:, None]
    c = np.arange(w)[None, :]
    krel = c - BAND_PAST
    ct = t // CHUNK
    inband = (krel >= CHUNK * ct - BAND_PAST) & (krel < CHUNK * ct + CHUNK) & (c < n_valid_cols)
    n = tq + w - 1
    idx = np.clip(np.arange(n) - (w - 1 - BAND_PAST), -REL_CLIP, REL_CLIP) + REL_CLIP
    r0, r1 = int((idx == idx[0]).sum()), int((idx == idx[-1]).sum())
    tab = rel_table.astype(F32) * LOG2E
    f = jnp.concatenate([jnp.broadcast_to(tab[idx[0]], (r0, N_HEADS)), tab[idx[r0]:idx[n - r1 - 1] + 1],
                         jnp.broadcast_to(tab[idx[-1]], (r1, N_HEADS))], axis=0).T
    hankel = jnp.tile(f, (1, tq + 1))[:, :tq * (n + 1)].reshape(N_HEADS, tq, n + 1)[:, :, :w]
    return jnp.where(inband[None], hankel[:, :, ::-1], NEG)


def _mixer_b_kernel(q_ref, k_ref, v_ref, upper_ref, o_ref, *, tq, tk, q_start):
    q0 = q_start + pl.program_id(1) * tq
    nblk = (q0 + tq - 2) // tk + 1
    upper = upper_ref[...]
    qpos = q0 + lax.broadcasted_iota(jnp.int32, (tq, tk), 0)
    kiota = lax.broadcasted_iota(jnp.int32, (tq, tk), 1)
    n_full = q0 // tk
    qm = _masked_pair_heads(q_ref)
    low = _low_half((tq, LANES))

    slabs = [slice(pr * LANES, (pr + 1) * LANES) for pr in range(N_HEADS // 2)]

    def body(t, carry, masked, first, nsub):
        accs, laters = carry
        starts = [pl.multiple_of((first - t * nsub - u) * tk, tk) for u in range(nsub)]
        masks = [(ks + kiota) < qpos for ks in starts]
        zs = [[_dot_nt(qm[h], k_ref[0, pl.ds(ks, tk), slabs[h // 2]]) for h in range(N_HEADS)] for ks in starts]
        log_betas, leaves = [], []
        for u in range(nsub):
            sps = [_softplus2(z) for z in zs[u]]
            leaves.append([jnp.where(masks[u], sp, 0.0) for sp in sps] if masked else sps)
            log_betas.append([z - sp for z, sp in zip(zs[u], sps)])
        tails = [[_dot(s.astype(BF), upper) for s in leaves[u]] for u in range(nsub)]
        laters = list(laters)
        ws = []
        for u in range(nsub):
            ws_u = []
            for h in range(N_HEADS):
                w = jnp.exp2(log_betas[u][h] - (tails[u][h] + laters[h]))
                ws_u.append((jnp.where(masks[u], w, 0.0) if masked else w).astype(BF))
                laters[h] = laters[h] + jnp.sum(leaves[u][h], axis=1, keepdims=True)
            ws.append(ws_u)
        wvs = [functools.reduce(jnp.add, [_dot(ws[u][h], v_ref[0, pl.ds(starts[u], tk), slabs[h // 2]])
                                          for u in range(nsub)]) for h in range(N_HEADS)]
        accs_new = [accs[pr] + jnp.where(low, wvs[2 * pr], wvs[2 * pr + 1]) for pr in range(N_HEADS // 2)]
        return tuple(accs_new), tuple(laters)

    carry = (tuple(jnp.zeros((tq, LANES), F32) for _ in range(N_HEADS // 2)),
             tuple(jnp.zeros((tq, 1), F32) for _ in range(N_HEADS)))
    carry = lax.fori_loop(0, nblk - n_full, functools.partial(body, masked=True, first=nblk - 1, nsub=1), carry)

    def live(carry):
        return jnp.min(functools.reduce(jnp.minimum, carry[1])) < F32_UNDERFLOW_LOG2

    pair = functools.partial(body, masked=False, first=n_full - 1, nsub=2)
    t_end, carry = lax.while_loop(lambda c: (c[0] < n_full // 2) & live(c[1]),
                                  lambda c: (c[0] + 1, pair(c[0], c[1])), (jnp.int32(0), carry))
    last = jnp.where((t_end == n_full // 2) & live(carry), n_full % 2, 0)
    accs, _ = lax.fori_loop(0, last, functools.partial(body, masked=False, first=0, nsub=1), carry)
    for pr in range(N_HEADS // 2):
        o_ref[0, :, pr * LANES:(pr + 1) * LANES] = accs[pr].astype(o_ref.dtype)


def _resident_spec(kp, width, col):
    return pl.BlockSpec((1, kp, width), lambda bi, i: (bi, 0, col), pipeline_mode=pl.Buffered(1))


def _kv_spec(kp, col):
    return _resident_spec(kp, BRANCH_W, col)


def _mixer_b(q_all, k_arr, v_arr, kv_col, tq, tk, q_start):
    b, l, _ = q_all.shape
    kp = k_arr.shape[1]
    upper = (np.arange(tk)[:, None] > np.arange(tk)[None, :]).astype(np.float32)
    return pl.pallas_call(
        functools.partial(_mixer_b_kernel, tq=tq, tk=tk, q_start=q_start),
        grid=(b, l // tq),
        in_specs=[pl.BlockSpec((1, tq, BRANCH_W), lambda bi, i: (bi, i, 1)), _kv_spec(kp, kv_col),
                  _kv_spec(kp, kv_col), pl.BlockSpec((tk, tk), lambda bi, i: (0, 0))],
        out_specs=pl.BlockSpec((1, tq, BRANCH_W), lambda bi, i: (bi, i, 0)),
        out_shape=jax.ShapeDtypeStruct((b, l, BRANCH_W), BF),
        compiler_params=_cparams("parallel", "parallel"), name="mixer_b",
    )(q_all, k_arr, v_arr, jnp.asarray(upper, BF))


def _mixer_c_kernel(q_ref, iq_ref, iw_ref, ik_ref, k_ref, vt_ref, lower_ref, o_ref, key_ref, plane_ref, tied_ref,
                    *, tq, tk, q_start, n_keys, topk):
    q0 = q_start + pl.program_id(1) * tq
    last_adm = jnp.minimum(((q0 + tq - 1) // CHUNK) * CHUNK + CHUNK - 1, n_keys - 1)
    nblk = last_adm // tk + 1
    n_full = jnp.minimum((q0 // CHUNK + 1) * CHUNK, n_keys) // tk
    qchunk = (q0 + lax.broadcasted_iota(jnp.int32, (tk, tq), 1)) // CHUNK
    kiota = lax.broadcasted_iota(jnp.int32, (tk, tq), 0)
    topk_f = jnp.float32(topk)
    iqm = _masked_pair_heads(iq_ref)
    iw_rows = [iw_ref[0, g:g + 1, :] for g in range(H_IDX)]
    slabs = [slice(pr * LANES, (pr + 1) * LANES) for pr in range(N_HEADS // 2)]

    def score_body(t, carry, masked, first, nsub):
        blocks = [first + t * nsub + u for u in range(nsub)]
        starts = [pl.multiple_of(j * tk, tk) for j in blocks]
        dots = [[_dot_nt(ik_ref[0, pl.ds(ks, tk), :], iqm[g]) for g in range(H_IDX)] for ks in starts]
        for u in range(nsub):
            sc = None
            for g in range(H_IDX):
                term = iw_rows[g] * jnp.maximum(dots[u][g], 0.0)
                sc = term if sc is None else sc + term
            sc = jnp.where(sc == 0.0, 0.0, sc)
            bits = pltpu.bitcast(sc, jnp.int32)
            key = bits ^ ((bits >> 31) & jnp.int32(0x7FFFFFFF))
            if masked:
                kpos = starts[u] + kiota
                adm = ((kpos // CHUNK) <= qchunk) & (kpos < n_keys)
                key = jnp.where(adm, key, jnp.int32(INT_MIN))
            key_ref[blocks[u]] = key
        return carry

    lax.fori_loop(0, n_full // 2, functools.partial(score_body, masked=False, first=0, nsub=2), 0)
    lax.fori_loop(0, n_full % 2, functools.partial(score_body, masked=False, first=n_full - 1, nsub=1), 0)
    lax.fori_loop(0, nblk - n_full, functools.partial(score_body, masked=True, first=n_full, nsub=1), 0)

    grp_per_blk = tk // (32 * SUBLANES)
    n_quad = (nblk + BLK_PER_STEP - 1) // BLK_PER_STEP
    grp_per_step = BLK_PER_STEP * grp_per_blk

    def fill(j, carry):
        key_ref[j] = jnp.full((tk, tq), INT_MIN, jnp.int32)
        return carry

    lax.fori_loop(nblk, n_quad * BLK_PER_STEP, fill, 0)

    def transpose_block(j, carry):
        for half in range(grp_per_blk):
            base = half * 32 * SUBLANES
            w = [key_ref[j, base + c * SUBLANES:base + (c + 1) * SUBLANES, :] ^ jnp.int32(INT_MIN) for c in range(32)]
            s, m = 16, 0x0000FFFF
            while s:
                k = 0
                while k < 32:
                    t = (w[k] ^ (w[k + s] >> s)) & jnp.int32(m)
                    w[k] = w[k] ^ t
                    w[k + s] = w[k + s] ^ (t << s)
                    k = (k + s + 1) & ~s
                s >>= 1
                m ^= (m << s) & 0xFFFFFFFF
            g = j * grp_per_blk + half
            for t in range(32):
                plane_ref[g, t] = w[t]
            tied_ref[g] = jnp.full((SUBLANES, tq), -1, jnp.int32)
        return carry

    lax.fori_loop(0, n_quad * BLK_PER_STEP, transpose_block, 0)

    def bit_body(t, carry):
        prefix, n_above = carry

        def tally(i, acc):
            return acc + _tree_sum([lax.population_count(tied_ref[i * grp_per_step + u]
                                                         & plane_ref[i * grp_per_step + u, t])
                                    for u in range(grp_per_step)])

        ones = lax.fori_loop(0, n_quad, tally, jnp.zeros((SUBLANES, tq), jnp.int32))
        cnt = n_above + jnp.sum(ones.astype(F32), axis=0, keepdims=True)
        take = cnt >= topk_f

        def narrow(i, c):
            for u in range(grp_per_step):
                g = i * grp_per_step + u
                tied = tied_ref[g]
                hit = tied & plane_ref[g, t]
                tied_ref[g] = jnp.where(take, hit, tied ^ hit)
            return c

        lax.fori_loop(0, n_quad, narrow, 0)
        bit = jnp.left_shift(jnp.int32(1), 31 - t)
        return jnp.where(take, prefix | bit, prefix), jnp.where(take, n_above, cnt)

    prefix, n_above = lax.fori_loop(0, 32, bit_body, (jnp.zeros((1, tq), jnp.int32), jnp.zeros((1, tq), F32)))
    thr = jnp.maximum(prefix ^ jnp.int32(INT_MIN), jnp.int32(INT_MIN + 1))

    def tally_tied(i, acc):
        return acc + _tree_sum([lax.population_count(tied_ref[i * grp_per_step + u]) for u in range(grp_per_step)])

    n_tied = lax.fori_loop(0, n_quad, tally_tied, jnp.zeros((SUBLANES, tq), jnp.int32))
    n_tied = jnp.sum(n_tied.astype(F32), axis=0, keepdims=True)
    n_ge = jnp.where(prefix == 0, 0.0, n_above + n_tied)
    need = topk_f - n_above

    @pl.when(jnp.max(n_ge) > topk_f)
    def _():
        def fix(j, seen):
            kb = key_ref[j]
            eq = jnp.where(kb == thr, 1.0, 0.0)
            rank = _dot(lower_ref[...], eq.astype(BF)) + seen
            key_ref[j] = jnp.where(eq * rank > need, jnp.int32(INT_MIN), kb)
            return seen + jnp.sum(eq, axis=0, keepdims=True)
        lax.fori_loop(0, nblk, fix, jnp.zeros((1, tq), F32))

    qm = _masked_pair_heads(q_ref)

    def body(t, carry, first, nsub):
        ms, ls, accs = carry
        blocks = [first + t * nsub + u for u in range(nsub)]
        starts = [pl.multiple_of(j * tk, tk) for j in blocks]
        raw = [[_dot_nt(k_ref[0, pl.ds(ks, tk), slabs[h // 2]], qm[h]) for h in range(N_HEADS)] for ks in starts]
        selected = [key_ref[j] >= thr for j in blocks]
        ms_new, ls_new, alphas, ps = [], [], [], []
        for h in range(N_HEADS):
            logits = [jnp.where(selected[u], raw[u][h], NEG) for u in range(nsub)]
            m_new = functools.reduce(jnp.maximum, [ms[h]] + [jnp.max(s, axis=0, keepdims=True) for s in logits])
            alpha = jnp.exp2(ms[h] - m_new)
            p = [jnp.exp2(s - m_new) for s in logits]
            ms_new.append(m_new)
            ls_new.append(alpha * ls[h] + functools.reduce(jnp.add, [jnp.sum(pu, axis=0, keepdims=True) for pu in p]))
            alphas.append(alpha)
            ps.append([pu.astype(BF) for pu in p])
        accs_new = []
        for h in range(N_HEADS):
            pv = functools.reduce(jnp.add, [_dot(vt_ref[0, blocks[u], slabs[h // 2], :], ps[h][u]) for u in range(nsub)])
            rows = slice((h % 2) * HEAD_DIM, (h % 2 + 1) * HEAD_DIM)
            accs_new.append(alphas[h] * accs[h] + pv[rows])
        return tuple(ms_new), tuple(ls_new), tuple(accs_new)

    carry = (tuple(jnp.full((1, tq), NEG, F32) for _ in range(N_HEADS)),
             tuple(jnp.zeros((1, tq), F32) for _ in range(N_HEADS)),
             tuple(jnp.zeros((HEAD_DIM, tq), F32) for _ in range(N_HEADS)))
    carry = lax.fori_loop(0, nblk // 2, functools.partial(body, first=0, nsub=2), carry)
    _, ls, accs = lax.fori_loop(0, nblk % 2, functools.partial(body, first=nblk - 1, nsub=1), carry)
    out_t = jnp.concatenate([accs[h] / ls[h] for h in range(N_HEADS)], axis=0)
    o_ref[0] = out_t.T.astype(o_ref.dtype)


BLK_PER_STEP = 8


def _mixer_c(q_all, iq, iw, ik_arr, k_arr, v_arr, kv_col, tq, tk, q_start, n_keys):
    b, l, _ = q_all.shape
    l_pad = _round_up(l, tq)
    q_all, iq, iw = (_pad_rows(t, l_pad) for t in (q_all, iq, iw))
    kp = k_arr.shape[1]
    n_kblk = kp // tk
    topk = min(TOPK_MAX, n_keys // 4)
    lower = (np.arange(tk)[:, None] >= np.arange(tk)[None, :]).astype(np.float32)
    n_blk_alloc = _round_up(n_kblk, BLK_PER_STEP)
    n_grp = n_blk_alloc * (tk // (32 * SUBLANES))
    vt = v_arr[:, :, kv_col * BRANCH_W:(kv_col + 1) * BRANCH_W].reshape(b, n_kblk, tk, BRANCH_W).transpose(0, 1, 3, 2)
    return pl.pallas_call(
        functools.partial(_mixer_c_kernel, tq=tq, tk=tk, q_start=q_start, n_keys=n_keys, topk=topk),
        grid=(b, l_pad // tq),
        in_specs=[pl.BlockSpec((1, tq, BRANCH_W), lambda bi, i: (bi, i, 2)),
                  pl.BlockSpec((1, tq, H_IDX * D_IDX), lambda bi, i: (bi, i, 0)),
                  pl.BlockSpec((1, H_IDX, tq), lambda bi, i: (bi, 0, i)),
                  _resident_spec(kp, 2 * D_IDX, 0),
                  _kv_spec(kp, kv_col),
                  pl.BlockSpec((1, n_kblk, BRANCH_W, tk), lambda bi, i: (bi, 0, 0, 0), pipeline_mode=pl.Buffered(1)),
                  pl.BlockSpec((tk, tk), lambda bi, i: (0, 0))],
        out_specs=pl.BlockSpec((1, tq, BRANCH_W), lambda bi, i: (bi, i, 0)),
        out_shape=jax.ShapeDtypeStruct((b, l_pad, BRANCH_W), BF),
        scratch_shapes=[pltpu.VMEM((n_blk_alloc, tk, tq), jnp.int32),
                        pltpu.VMEM((n_grp, 32, SUBLANES, tq), jnp.int32), pltpu.VMEM((n_grp, SUBLANES, tq), jnp.int32)],
        compiler_params=_cparams("parallel", "parallel"), name="mixer_c",
    )(q_all, iq, jnp.moveaxis(iw, 1, 2), ik_arr, k_arr, vt, jnp.asarray(lower, BF))[:, :l]


def _mixer_d_kernel(q_ref, k_ref, v_ref, fq_ref, fk_ref, kbound_ref, fbound_ref, o_ref,
                    *, tq, tk, q_start, n_kblk, step):
    q0 = q_start + pl.program_id(1) * tq
    nblk = (q0 + tq - 1) // tk + 1
    n_full = (q0 + 1) // tk
    qpos = q0 + lax.broadcasted_iota(jnp.int32, (tq, tk), 0)
    kiota = lax.broadcasted_iota(jnp.int32, (tq, tk), 1)
    qm = _masked_pair_heads(q_ref)
    fq = [fq_ref[0, :, h:h + 1] for h in range(N_HEADS)]
    low = _low_half((tq, LANES))

    slabs = [slice(pr * LANES, (pr + 1) * LANES) for pr in range(N_HEADS // 2)]

    def body(t, carry, masked, first, nsub, sign):
        blocks = [first + sign * (t * nsub + u) for u in range(nsub)]
        starts = [pl.multiple_of(j * tk, tk) for j in blocks]
        raw = [[_dot_nt(qm[h], k_ref[0, pl.ds(ks, tk), slabs[h // 2]]) for h in range(N_HEADS)] for ks in starts]
        logits = []
        for u in range(nsub):
            row = [raw[u][h] + fq[h] - fk_ref[0, pl.ds(h * n_kblk + blocks[u], 1), :] for h in range(N_HEADS)]
            if masked:
                visible = (starts[u] + kiota) <= qpos
                row = [jnp.where(visible, s, NEG) for s in row]
            logits.append(row)
        return _softmax_step(carry, logits, lambda u, pr: v_ref[0, pl.ds(starts[u], tk), slabs[pr]], low)

    q_norm = [1.001 * jnp.sqrt(jnp.sum(jnp.square(qm[h].astype(F32)), axis=1, keepdims=True)) for h in range(N_HEADS)]
    bi = pl.program_id(0)

    def live(carry, j):
        gap = [q_norm[h] * kbound_ref[bi, h * n_kblk + j] + fq[h] + fbound_ref[bi, h * n_kblk + j] - carry[0][h]
               for h in range(N_HEADS)]
        return jnp.max(functools.reduce(jnp.maximum, gap)) > -(F32_UNDERFLOW_LOG2 + 8.0)

    carry = lax.fori_loop(0, nblk - n_full, functools.partial(body, masked=True, first=n_full, nsub=1, sign=1),
                          _softmax_init(tq))
    group = functools.partial(body, masked=False, first=n_full - 1, nsub=step, sign=-1)
    t_end, carry = lax.while_loop(
        lambda c: (c[0] < n_full // step) & live(c[1], jnp.maximum(n_full - 1 - c[0] * step, 0)),
        lambda c: (c[0] + 1, group(c[0], c[1])), (jnp.int32(0), carry))
    n_rest = n_full % step
    rest = jnp.where((t_end == n_full // step) & live(carry, jnp.maximum(n_rest - 1, 0)), n_rest, 0)
    carry = lax.fori_loop(0, rest, functools.partial(body, masked=False, first=n_rest - 1, nsub=1, sign=-1), carry)
    _softmax_finish(carry, o_ref, low)


def _mixer_d(q_all, k_arr, v_arr, kv_col, fq, fk, tq, tk, q_start, step):
    b, l, _ = q_all.shape
    kp = k_arr.shape[1]
    n_kblk = kp // tk
    kd = k_arr[:, :, kv_col * BRANCH_W:(kv_col + 1) * BRANCH_W].astype(F32).reshape(b, n_kblk, tk, N_HEADS, HEAD_DIM)
    k_norm = 1.001 * jnp.sqrt(jnp.max(jnp.sum(kd * kd, axis=-1), axis=2))
    kbound = jnp.moveaxis(lax.cummax(k_norm, axis=1), 1, 2).reshape(b, N_HEADS * n_kblk)
    fbound = lax.cummax(jnp.max(-fk, axis=-1).reshape(b, N_HEADS, n_kblk), axis=2).reshape(b, N_HEADS * n_kblk)
    smem = pl.BlockSpec(memory_space=pltpu.SMEM)
    return pl.pallas_call(
        functools.partial(_mixer_d_kernel, tq=tq, tk=tk, q_start=q_start, n_kblk=n_kblk, step=step),
        grid=(b, l // tq),
        in_specs=[pl.BlockSpec((1, tq, BRANCH_W), lambda bi, i: (bi, i, 3)), _kv_spec(kp, kv_col),
                  _kv_spec(kp, kv_col),
                  pl.BlockSpec((1, tq, N_HEADS), lambda bi, i: (bi, i, 0)),
                  pl.BlockSpec((1, N_HEADS * n_kblk, tk), lambda bi, i: (bi, 0, 0)), smem, smem],
        out_specs=pl.BlockSpec((1, tq, BRANCH_W), lambda bi, i: (bi, i, 0)),
        out_shape=jax.ShapeDtypeStruct((b, l, BRANCH_W), BF),
        compiler_params=_cparams("parallel", "parallel"), name="mixer_d",
    )(q_all, k_arr, v_arr, fq, fk, kbound, fbound)


def _merge_kernel(x_ref, g1_ref, oa_ref, ob_ref, oc_ref, od_ref, wg_ref, wb_ref, wo_ref, y_ref):
    x = x_ref[...]
    h = _rms(x, g1_ref[...]).astype(BF)
    merged = None
    for g, o_ref in enumerate((oa_ref, ob_ref, oc_ref, od_ref)):
        gate = 1.0 / (1.0 + jnp.exp(-_dot(h, wg_ref[:, g * D_MODEL:(g + 1) * D_MODEL])))
        term = gate * _dot(o_ref[...], wb_ref[g])
        merged = term if merged is None else merged + term
    y_ref[...] = x + _dot(merged.astype(BF), wo_ref[...])


def _merge(x2, outs, lp, tm):
    m_rows = x2.shape[0]
    row = lambda i: (i, 0)
    const = lambda i: (0, 0)
    return pl.pallas_call(
        _merge_kernel, grid=(m_rows // tm,),
        in_specs=[pl.BlockSpec((tm, D_MODEL), row), pl.BlockSpec((1, D_MODEL), const)]
                 + [pl.BlockSpec((tm, BRANCH_W), row)] * N_MIXERS
                 + [pl.BlockSpec((D_MODEL, N_MIXERS * D_MODEL), const),
                    pl.BlockSpec((N_MIXERS, BRANCH_W, D_MODEL), lambda i: (0, 0, 0)),
                    pl.BlockSpec((D_MODEL, D_MODEL), const)],
        out_specs=pl.BlockSpec((tm, D_MODEL), row),
        out_shape=jax.ShapeDtypeStruct((m_rows, D_MODEL), F32),
        compiler_params=_cparams("parallel"), name="merge",
    )(x2, lp["g1"], *outs, lp["w_gate"], lp["w_branch"], lp["w_o"])


def _gelu_tanh(x):
    return x * (0.5 * (1.0 + jnp.tanh(np.sqrt(2.0 / np.pi).astype(np.float32) * (x + 0.044715 * (x * x * x)))))


def _ffn_kernel(x_ref, xp_ref, st_ref, g2_ref, win_ref, wc_ref, bc_ref, wd_ref, y_ref, ut_ref, h_ref, a_ref,
                *, tm, tf, tiles_per_seq):
    first_tile = pl.program_id(0) % tiles_per_seq == 0
    g2 = g2_ref[...]
    x = x_ref[...]
    h_ref[:CTX_ROWS, :] = _rms(xp_ref[...], g2).astype(BF)
    h_ref[CTX_ROWS:, :] = _rms(x, g2).astype(BF)
    h = h_ref[...]
    for f in range(0, D_FF, tf):
        u = _dot(h, win_ref[:, f:f + tf])
        gt = _dot(h, win_ref[:, D_FF + f:D_FF + f + tf])[CTX_ROWS:]
        ctx = jnp.where(first_tile, st_ref[0, :, f:f + tf], u[:CTX_ROWS])
        ue = jnp.concatenate([ctx, u[CTX_ROWS:]], axis=0)
        wc = wc_ref[:, f:f + tf]
        conv = bc_ref[:, f:f + tf] + ((ue[CTX_ROWS - 2:CTX_ROWS - 2 + tm] * wc[0:1]
                                       + ue[CTX_ROWS - 1:CTX_ROWS - 1 + tm] * wc[1:2]) + ue[CTX_ROWS:] * wc[2:3])
        a_ref[:, f:f + tf] = (_gelu_tanh(conv) * gt).astype(BF)
        ut_ref[0, :, f:f + tf] = ue[tm:]
    y_ref[...] = x + _dot(a_ref[...], wd_ref[...])


def _ffn(x2, state8, lp, tm, tf, seq_len):
    m_rows = x2.shape[0]
    tiles_per_seq = seq_len // tm
    const = lambda i: (0, 0)
    once = pl.Buffered(1)
    return pl.pallas_call(
        functools.partial(_ffn_kernel, tm=tm, tf=tf, tiles_per_seq=tiles_per_seq),
        grid=(m_rows // tm,),
        in_specs=[pl.BlockSpec((tm, D_MODEL), lambda i: (i, 0)),
                  pl.BlockSpec((CTX_ROWS, D_MODEL), lambda i: (jnp.maximum(i * (tm // CTX_ROWS) - 1, 0), 0)),
                  pl.BlockSpec((1, CTX_ROWS, D_FF), lambda i: (i // tiles_per_seq, 0, 0)),
                  pl.BlockSpec((1, D_MODEL), const),
                  pl.BlockSpec((D_MODEL, 2 * D_FF), const, pipeline_mode=once),
                  pl.BlockSpec((CONV_W, D_FF), const),
                  pl.BlockSpec((1, D_FF), const),
                  pl.BlockSpec((D_FF, D_MODEL), const, pipeline_mode=once)],
        out_specs=[pl.BlockSpec((tm, D_MODEL), lambda i: (i, 0)),
                   pl.BlockSpec((1, CTX_ROWS, D_FF), lambda i: (i, 0, 0))],
        out_shape=[jax.ShapeDtypeStruct((m_rows, D_MODEL), F32),
                   jax.ShapeDtypeStruct((m_rows // tm, CTX_ROWS, D_FF), F32)],
        scratch_shapes=[pltpu.VMEM((tm + CTX_ROWS, D_MODEL), BF), pltpu.VMEM((tm, D_FF), BF)],
        compiler_params=_cparams("parallel"), name="ffn",
    )(x2, x2, state8, lp["g2"], lp["w_ffn_in"], lp["w_conv"], lp["b_conv"], lp["w_down"])


def _rope_tables(q_start, length):
    half = HEAD_DIM // 2
    inv = ROPE_THETA ** (-jnp.arange(half, dtype=F32) / half)
    ang = (q_start + jnp.arange(length)).astype(F32)[:, None] * inv[None, :]
    cos = jnp.cos(ang)
    sin = jnp.sin(ang)
    cos_h = jnp.concatenate([cos, cos], axis=1)
    sin_h = jnp.concatenate([-sin, sin], axis=1)
    n_rot = H_IDX + 1
    pad = MISC_W - n_rot * HEAD_DIM
    return {
        "cq": jnp.tile(cos_h, (1, N_HEADS)), "sq": jnp.tile(sin_h, (1, N_HEADS)),
        "cm": jnp.concatenate([jnp.tile(cos_h, (1, n_rot)), jnp.ones((length, pad), F32)], axis=1),
        "sm": jnp.concatenate([jnp.tile(sin_h, (1, n_rot)), jnp.zeros((length, pad), F32)], axis=1),
    }


def _layer_params(norm1_g, w_in, b_forget, qk_g, rel_table, w_branch, w_o, norm2_g, w_ffn_in, w_conv, b_conv,
                  w_down):
    a = 3 * QKV_W
    i1 = a + H_IDX * D_IDX + D_IDX + H_IDX
    d1 = i1 + QKV_W
    f1 = d1 + N_HEADS
    n_misc = (i1 - a) + N_HEADS
    w_misc = jnp.concatenate([w_in[:, a:i1], w_in[:, d1:f1], jnp.zeros((D_MODEL, MISC_W - n_misc), w_in.dtype)],
                             axis=1)
    f_lane = D_IDX + H_IDX
    bf_row = jnp.zeros((1, LANES), F32).at[0, f_lane:f_lane + N_HEADS].set(b_forget.astype(F32))
    head_ones = (np.arange(BRANCH_W)[:, None] // HEAD_DIM == np.arange(BRANCH_W)[None, :] // HEAD_DIM)
    return {
        "g1": norm1_g.astype(F32)[None, :], "g2": norm2_g.astype(F32)[None, :],
        "w_qkv": jnp.concatenate([w_in[:, :a], w_in[:, i1:d1]], axis=1).astype(BF),
        "w_misc": w_misc.astype(BF), "w_gate": w_in[:, f1:].astype(BF),
        "qkg": jnp.tile(qk_g.astype(F32), (1, N_HEADS)), "bf_row": bf_row,
        "head_ones": jnp.asarray(head_ones.astype(np.float32), BF),
        "rel_table": rel_table, "w_branch": w_branch.astype(BF), "w_o": w_o.astype(BF),
        "w_ffn_in": w_ffn_in.astype(BF), "w_conv": w_conv.astype(F32), "b_conv": b_conv.astype(F32)[None, :],
        "w_down": w_down.astype(BF),
    }


def _pad_rows(a, rows):
    return jnp.pad(a, ((0, 0), (0, rows - a.shape[1])) + ((0, 0),) * (a.ndim - 2))


def _round_up(n, m):
    return -(-n // m) * m


def _layer(x, past, lp, tabs):
    b, l, _ = x.shape
    m_rows = b * l
    q_start = 0 if past is None else past[2].shape[1]
    n_keys = q_start + l
    tq = min(256, l)
    tq_c = LANES
    tk = 256
    tk_d = 512
    kp = _round_up(n_keys, max(tk, tk_d))
    tm = min(256, m_rows)

    x2 = x.reshape(m_rows, D_MODEL)
    (q_all, kbf, vbf, ka, va, kb, vb, kc, vc, kd, vd, iq, misc) = _projection(x2, lp, tabs, tm)
    ik = misc[:, :D_IDX].reshape(b, l, D_IDX)
    iw = misc[:, D_IDX:D_IDX + H_IDX].reshape(b, l, H_IDX)
    logf = misc[:, D_IDX + H_IDX:D_IDX + H_IDX + N_HEADS].reshape(b, l, N_HEADS)
    q_all = q_all.reshape(b, l, N_MIXERS * BRANCH_W)
    iq = iq.reshape(b, l, H_IDX * D_IDX)
    heads = lambda t: t.reshape(b, l, N_HEADS, HEAD_DIM)
    flat = lambda t: t.reshape(t.shape[0], t.shape[1], BRANCH_W)

    if past is None:
        kbf3 = kbf.reshape(b, l, N_MIXERS * BRANCH_W)
        vbf3 = vbf.reshape(b, l, N_MIXERS * BRANCH_W)
        kv = [(kbf3, vbf3, c) for c in range(N_MIXERS)]
        ik_all = ik.astype(BF)
        logf_all = logf
        a_args = (kbf3, vbf3, 0, _band_bias(lp["rel_table"], tq, BAND_PAST + tq, BAND_PAST + tq), tq, 3, tq)
        conv_state = jnp.zeros((b, CONV_W - 1, D_FF), F32)
    else:
        (pa_k, pa_v, pb_k, pb_v, pc_k, pc_v, pc_ki, pd_k, pd_v, pd_f, conv_state) = past
        new_bf = lambda t, c: t.reshape(b, l, N_MIXERS, BRANCH_W)[:, :, c]
        cat = lambda p, c, src, rows: _pad_rows(jnp.concatenate([flat(p).astype(BF), new_bf(src, c)], axis=1), rows)
        kv = [None] + [(cat(pk, c, kbf, kp), cat(pv, c, vbf, kp), 0)
                       for c, (pk, pv) in ((1, (pb_k, pb_v)), (2, (pc_k, pc_v)), (3, (pd_k, pd_v)))]
        ik_all = jnp.concatenate([pc_ki.astype(BF), ik.astype(BF)], axis=1)
        logf_all = jnp.concatenate([pd_f.astype(F32), logf], axis=1)
        a_keys = pa_k.shape[1] + l
        wa = _round_up(BAND_PAST + l, LANES)
        a_args = (cat(pa_k, 0, kbf, wa), cat(pa_v, 0, vbf, wa), 0,
                  _band_bias(lp["rel_table"], l, wa, a_keys), l, 1, wa)

    f_cum = jnp.cumsum(logf_all, axis=1) * LOG2E
    fq = f_cum[:, q_start:]
    fk = jnp.moveaxis(_pad_rows(f_cum, kp), 1, 2).reshape(b, N_HEADS * (kp // tk_d), tk_d)

    o_a = _mixer_a(q_all, *a_args)
    o_b = _mixer_b(q_all, *kv[1], tq, tk, q_start)
    ik_pair = _pad_rows(jnp.concatenate([ik_all, ik_all], axis=-1), kp)
    o_c = _mixer_c(q_all, iq, iw, ik_pair, *kv[2], tq_c, tk_d, q_start, n_keys)
    o_d = _mixer_d(q_all, *kv[3], fq, fk, tq, tk_d, q_start, 2)
    x2 = _merge(x2, [o.reshape(m_rows, BRANCH_W) for o in (o_a, o_b, o_c, o_d)], lp, tm)

    state8 = jnp.pad(conv_state.astype(F32), ((0, 0), (CTX_ROWS - (CONV_W - 1), 0), (0, 0)))
    x2, u_tail = _ffn(x2, state8, lp, min(512, l), 256, l)
    u_tail = u_tail.reshape(b, -1, CTX_ROWS, D_FF)[:, -1]

    if past is None:
        a_keep = min(BAND_PAST, l)
        new_a = (heads(ka)[:, -a_keep:], heads(va)[:, -a_keep:])
    else:
        a_keep = pa_k.shape[1]
        new_a = (jnp.concatenate([pa_k, heads(ka)], axis=1)[:, -a_keep:],
                 jnp.concatenate([pa_v, heads(va)], axis=1)[:, -a_keep:])
    new = new_a + (heads(kb), heads(vb), heads(kc), heads(vc), ik, heads(kd), heads(vd), logf,
                   u_tail[:, -(CONV_W - 1):])
    return x2.reshape(b, l, D_MODEL), new


def kernel(x_prompt, x_sample, cache_a_k, cache_a_v, cache_b_k, cache_b_v, cache_c_k, cache_c_v, cache_c_kidx, cache_d_k, cache_d_v, cache_d_logf, state_ffn_conv, norm1_g, w_in, b_forget, qk_norm_g, rel_bias, w_branch, w_o, norm2_g, w_ffn_in, w_conv, b_conv, w_down):
    depth = w_in.shape[0]
    past_len = cache_b_k.shape[2]
    tabs_p = _rope_tables(0, x_prompt.shape[1])
    tabs_s = {k: jnp.tile(v, (x_sample.shape[0], 1)) for k, v in _rope_tables(past_len, x_sample.shape[1]).items()}
    y_p, y_s = x_prompt, x_sample
    p_states, s_states = [], []
    for d in range(depth):
        lp = _layer_params(norm1_g[d], w_in[d], b_forget[d], qk_norm_g[d], rel_bias[d], w_branch[d], w_o[d],
                           norm2_g[d], w_ffn_in[d], w_conv[d], b_conv[d], w_down[d])
        y_p, st_p = _layer(y_p, None, lp, tabs_p)
        past = (cache_a_k[d], cache_a_v[d], cache_b_k[d], cache_b_v[d], cache_c_k[d], cache_c_v[d],
                cache_c_kidx[d], cache_d_k[d], cache_d_v[d], cache_d_logf[d], state_ffn_conv[d])
        y_s, st_s = _layer(y_s, past, lp, tabs_s)
        p_states.append(st_p)
        s_states.append(st_s)
    p_out = [jnp.stack(t) for t in zip(*p_states)]
    s_out = [jnp.stack(t) for t in zip(*s_states)]
    return (y_p, y_s, *p_out, *s_out)
```

```python
import functools

import numpy as np
import jax
import jax.numpy as jnp
from jax import lax
from jax.experimental import pallas as pl
from jax.experimental.pallas import tpu as pltpu

D_MODEL = 1024
CHUNK = 64
N_MIXERS = 4
N_HEADS = 4
HEAD_DIM = 64
BRANCH_W = N_HEADS * HEAD_DIM
BAND_PAST = 8 * CHUNK
REL_CLIP = 128
H_IDX = 4
D_IDX = 64
TOPK_MAX = 256
ROPE_THETA = 10000.0
D_FF = 2816
CONV_W = 3
EPS = 1e-6

BF = jnp.bfloat16
F32 = jnp.float32
NEG = -1e30
LOG2E = float(np.log2(np.e))
F32_UNDERFLOW_LOG2 = 160.0
INT_MIN = -2 ** 31
LANES = 128
SUBLANES = 8
CTX_ROWS = 2 * SUBLANES
QKV_W = 3 * BRANCH_W
MISC_W = 3 * LANES
VMEM_LIMIT = 56 * 1024 * 1024


def _cparams(*sem):
    return pltpu.CompilerParams(dimension_semantics=sem, vmem_limit_bytes=VMEM_LIMIT)


def _dot(a, b):
    return jnp.dot(a, b, preferred_element_type=F32)


def _dot_nt(a, b):
    return lax.dot_general(a, b, (((1,), (1,)), ((), ())), preferred_element_type=F32)


def _split_dot(x, m, terms):
    acc = None
    r = x
    for t in range(terms):
        hi = r.astype(BF)
        d = _dot(hi, m)
        acc = d if acc is None else acc + d
        if t + 1 < terms:
            r = r - hi.astype(F32)
    return acc


def _rms(x, g):
    ms = jnp.mean(x * x, axis=-1, keepdims=True)
    return x * lax.rsqrt(ms + EPS) * g


def _head_rms(t, g, head_ones):
    ms = _split_dot(t * t, head_ones, 3) * (1.0 / HEAD_DIM)
    return t * lax.rsqrt(ms + EPS) * g


def _swap_halves(y):
    n = y.shape[-1]
    lane = lax.broadcasted_iota(jnp.int32, y.shape, 1)
    first = (lane & (HEAD_DIM - 1)) < (HEAD_DIM // 2)
    return jnp.where(first, pltpu.roll(y, n - HEAD_DIM // 2, 1), pltpu.roll(y, HEAD_DIM // 2, 1))


def _softplus2(z):
    neg_abs = pltpu.bitcast(pltpu.bitcast(z, jnp.int32) | jnp.int32(INT_MIN), F32)
    return jnp.maximum(z, 0.0) + jnp.log2(1.0 + jnp.exp2(neg_abs))


def _tree_sum(xs):
    while len(xs) > 1:
        xs = [xs[i] + xs[i + 1] for i in range(0, len(xs) - 1, 2)] + ([xs[-1]] if len(xs) % 2 else [])
    return xs[0]


def _low_half(shape):
    return lax.broadcasted_iota(jnp.int32, shape, 1) < HEAD_DIM


def _masked_pair_heads(ref):
    out = []
    for pr in range(ref.shape[-1] // LANES):
        slab = ref[0, :, pr * LANES:(pr + 1) * LANES].astype(F32)
        low = _low_half(slab.shape)
        out.append(jnp.where(low, slab, 0.0).astype(BF))
        out.append(jnp.where(low, 0.0, slab).astype(BF))
    return out


def _stacked_pair_heads(ref):
    masked = _masked_pair_heads(ref)
    return [jnp.concatenate(masked[2 * pr:2 * pr + 2], axis=0) for pr in range(len(masked) // 2)]


def _pair_logits(keys, stacked, tq):
    out = []
    for pr, rhs in enumerate(stacked):
        slab = keys if keys.shape[-1] == LANES else keys[:, pr * LANES:(pr + 1) * LANES]
        both = _dot_nt(slab, rhs)
        out += [both[:, :tq], both[:, tq:]]
    return out


def _softmax_init(tq):
    return (tuple(jnp.full((tq, 1), NEG, F32) for _ in range(N_HEADS)),
            tuple(jnp.zeros((tq, 1), F32) for _ in range(N_HEADS)),
            tuple(jnp.zeros((tq, LANES), F32) for _ in range(N_HEADS // 2)))


def _softmax_step(carry, logits, v_tile, low):
    ms, ls, accs = carry
    tiles = range(len(logits))
    ms_new, ls_new, alphas, ps = [], [], [], []
    for h in range(N_HEADS):
        m_new = functools.reduce(jnp.maximum, [ms[h]] + [jnp.max(logits[u][h], axis=1, keepdims=True) for u in tiles])
        alpha = jnp.exp2(ms[h] - m_new)
        p = [jnp.exp2(logits[u][h] - m_new) for u in tiles]
        ms_new.append(m_new)
        ls_new.append(alpha * ls[h] + functools.reduce(jnp.add, [jnp.sum(pu, axis=1, keepdims=True) for pu in p]))
        alphas.append(alpha)
        ps.append([pu.astype(BF) for pu in p])
    pvs = [functools.reduce(jnp.add, [_dot(ps[h][u], v_tile(u, h // 2)) for u in tiles]) for h in range(N_HEADS)]
    accs_new = [jnp.where(low, alphas[2 * pr], alphas[2 * pr + 1]) * accs[pr]
                + jnp.where(low, pvs[2 * pr], pvs[2 * pr + 1]) for pr in range(N_HEADS // 2)]
    return tuple(ms_new), tuple(ls_new), tuple(accs_new)


def _softmax_finish(carry, o_ref, low):
    _, ls, accs = carry
    for pr in range(N_HEADS // 2):
        l_pair = jnp.where(low, ls[2 * pr], ls[2 * pr + 1])
        o_ref[0, :, pr * LANES:(pr + 1) * LANES] = (accs[pr] / l_pair).astype(o_ref.dtype)


def _proj_kernel(x_ref, g1_ref, wqkv_ref, wmisc_ref, qkg_ref, hones_ref, cq_ref, sq_ref, cm_ref, sm_ref,
                 bf_ref, q_ref, kbf_ref, vbf_ref, ka_ref, va_ref, kb_ref, vb_ref, kc_ref, vc_ref,
                 kd_ref, vd_ref, iq_ref, misc_ref):
    h = _rms(x_ref[...], g1_ref[...]).astype(BF)
    hones = hones_ref[...]
    k32 = (ka_ref, kb_ref, kc_ref, kd_ref)
    v32 = (va_ref, vb_ref, vc_ref, vd_ref)
    norm_row = (0, None, 2, 4)
    for m in range(N_MIXERS):
        y = _dot(h, wqkv_ref[:, m * QKV_W:(m + 1) * QKV_W])
        q, k, v = y[:, :BRANCH_W], y[:, BRANCH_W:2 * BRANCH_W], y[:, 2 * BRANCH_W:]
        if norm_row[m] is not None:
            r = norm_row[m]
            q = _head_rms(q, qkg_ref[r:r + 1, :], hones)
            k = _head_rms(k, qkg_ref[r + 1:r + 2, :], hones)
        if m == 2:
            c, s = cq_ref[...], sq_ref[...]
            q = q * c + _swap_halves(q) * s
            k = k * c + _swap_halves(k) * s
        cols = slice(m * BRANCH_W, (m + 1) * BRANCH_W)
        q_ref[:, cols] = (q * (HEAD_DIM ** -0.5 * LOG2E)).astype(BF)
        kbf_ref[:, cols] = k.astype(BF)
        vbf_ref[:, cols] = v.astype(BF)
        k32[m][...] = k
        v32[m][...] = v
    ym = _dot(h, wmisc_ref[...])
    r = ym * cm_ref[...] + _swap_halves(ym) * sm_ref[...]
    iq_ref[...] = r[:, :H_IDX * D_IDX].astype(BF)
    g2 = r[:, H_IDX * D_IDX:]
    lane = lax.broadcasted_iota(jnp.int32, g2.shape, 1)
    z = g2 + bf_ref[...]
    logf = jnp.minimum(z, 0.0) - jnp.log1p(jnp.exp(-jnp.abs(z)))
    is_w = (lane >= D_IDX) & (lane < D_IDX + H_IDX)
    is_f = (lane >= D_IDX + H_IDX) & (lane < D_IDX + H_IDX + N_HEADS)
    misc_ref[...] = jnp.where(is_f, logf, jnp.where(is_w, g2 * (H_IDX ** -0.5 * D_IDX ** -0.5), g2))


def _projection(x2, lp, tabs, tm):
    m_rows = x2.shape[0]
    n_tab = tabs["cq"].shape[0] // tm
    row = lambda i: (i, 0)
    const = lambda i: (0, 0)
    tab = lambda i: (i % n_tab, 0)
    wide = pl.BlockSpec((tm, N_MIXERS * BRANCH_W), row)
    head = pl.BlockSpec((tm, BRANCH_W), row)
    in_specs = [
        pl.BlockSpec((tm, D_MODEL), row),
        pl.BlockSpec((1, D_MODEL), const),
        pl.BlockSpec((D_MODEL, N_MIXERS * QKV_W), const),
        pl.BlockSpec((D_MODEL, MISC_W), const),
        pl.BlockSpec((6, BRANCH_W), const),
        pl.BlockSpec((BRANCH_W, BRANCH_W), const),
        pl.BlockSpec((tm, BRANCH_W), tab),
        pl.BlockSpec((tm, BRANCH_W), tab),
        pl.BlockSpec((tm, MISC_W), tab),
        pl.BlockSpec((tm, MISC_W), tab),
        pl.BlockSpec((1, LANES), const),
    ]
    out_shape = ([jax.ShapeDtypeStruct((m_rows, N_MIXERS * BRANCH_W), BF)] * 3
                 + [jax.ShapeDtypeStruct((m_rows, BRANCH_W), F32)] * 8
                 + [jax.ShapeDtypeStruct((m_rows, BRANCH_W), BF),
                    jax.ShapeDtypeStruct((m_rows, LANES), F32)])
    out_specs = [wide] * 3 + [head] * 8 + [head, pl.BlockSpec((tm, LANES), row)]
    return pl.pallas_call(
        _proj_kernel, grid=(m_rows // tm,), in_specs=in_specs, out_specs=out_specs, out_shape=out_shape,
        compiler_params=_cparams("parallel"), name="projection",
    )(x2, lp["g1"], lp["w_qkv"], lp["w_misc"], lp["qkg"], lp["head_ones"],
      tabs["cq"], tabs["sq"], tabs["cm"], tabs["sm"], lp["bf_row"])


def _mixer_a_kernel(*refs, nwb, wb):
    q_ref = refs[0]
    k_refs = refs[1:1 + nwb]
    v_refs = refs[1 + nwb:1 + 2 * nwb]
    bias_ref = refs[1 + 2 * nwb]
    o_ref = refs[2 + 2 * nwb]
    i = pl.program_id(1)
    qm = _masked_pair_heads(q_ref)
    low = _low_half((q_ref.shape[1], LANES))
    for pr in range(N_HEADS // 2):
        slab = slice(pr * LANES, (pr + 1) * LANES)
        outs = []
        for h in (2 * pr, 2 * pr + 1):
            logits = []
            for r in range(nwb):
                s = _dot_nt(qm[h], k_refs[r][0, :, slab]) + bias_ref[h, :, r * wb:(r + 1) * wb]
                logits.append(jnp.where(i - (nwb - 1) + r >= 0, s, NEG))
            m = functools.reduce(jnp.maximum, [jnp.max(s, axis=1, keepdims=True) for s in logits])
            ps = [jnp.exp2(s - m) for s in logits]
            l = functools.reduce(jnp.add, [jnp.sum(p, axis=1, keepdims=True) for p in ps])
            acc = functools.reduce(jnp.add, [_dot(p.astype(BF), v_refs[r][0, :, slab]) for r, p in enumerate(ps)])
            outs.append(acc / l)
        o_ref[0, :, slab] = jnp.where(low, outs[0], outs[1]).astype(o_ref.dtype)


def _mixer_a(q_all, k_arr, v_arr, kv_col, bias, tq, nwb, wb):
    b, l, _ = q_all.shape
    kspec = lambda r: pl.BlockSpec((1, wb, BRANCH_W),
                                   lambda bi, i, r=r: (bi, jnp.maximum(i - (nwb - 1) + r, 0), kv_col))
    in_specs = ([pl.BlockSpec((1, tq, BRANCH_W), lambda bi, i: (bi, i, 0))]
                + [kspec(r) for r in range(nwb)] * 2
                + [pl.BlockSpec(bias.shape, lambda bi, i: (0, 0, 0))])
    return pl.pallas_call(
        functools.partial(_mixer_a_kernel, nwb=nwb, wb=wb),
        grid=(b, l // tq), in_specs=in_specs,
        out_specs=pl.BlockSpec((1, tq, BRANCH_W), lambda bi, i: (bi, i, 0)),
        out_shape=jax.ShapeDtypeStruct((b, l, BRANCH_W), BF),
        compiler_params=_cparams("parallel", "parallel"), name="mixer_a",
    )(q_all, *([k_arr] * nwb), *([v_arr] * nwb), bias)


def _band_bias(rel_table, tq, w, n_valid_cols):
    t = np.arange(tq)[:, None]
    c = np.arange(w)[None, :]
    krel = c - BAND_PAST
    ct = t // CHUNK
    inband = (krel >= CHUNK * ct - BAND_PAST) & (krel < CHUNK * ct + CHUNK) & (c < n_valid_cols)
    n = tq + w - 1
    idx = np.clip(np.arange(n) - (w - 1 - BAND_PAST), -REL_CLIP, REL_CLIP) + REL_CLIP
    r0, r1 = int((idx == idx[0]).sum()), int((idx == idx[-1]).sum())
    tab = rel_table.astype(F32) * LOG2E
    f = jnp.concatenate([jnp.broadcast_to(tab[idx[0]], (r0, N_HEADS)), tab[idx[r0]:idx[n - r1 - 1] + 1],
                         jnp.broadcast_to(tab[idx[-1]], (r1, N_HEADS))], axis=0).T
    hankel = jnp.tile(f, (1, tq + 1))[:, :tq * (n + 1)].reshape(N_HEADS, tq, n + 1)[:, :, :w]
    return jnp.where(inband[None], hankel[:, :, ::-1], NEG)


def _mixer_b_kernel(q_ref, k_ref, v_ref, upper_ref, o_ref, *, tq, tk, q_start):
    q0 = q_start + pl.program_id(1) * tq
    nblk = (q0 + tq - 2) // tk + 1
    upper = upper_ref[...]
    qpos = q0 + lax.broadcasted_iota(jnp.int32, (tq, tk), 0)
    kiota = lax.broadcasted_iota(jnp.int32, (tq, tk), 1)
    n_full = q0 // tk
    qm = _masked_pair_heads(q_ref)
    low = _low_half((tq, LANES))

    slabs = [slice(pr * LANES, (pr + 1) * LANES) for pr in range(N_HEADS // 2)]

    def body(t, carry, masked, first, nsub):
        accs, laters = carry
        starts = [pl.multiple_of((first - t * nsub - u) * tk, tk) for u in range(nsub)]
        masks = [(ks + kiota) < qpos for ks in starts]
        zs = [[_dot_nt(qm[h], k_ref[0, pl.ds(ks, tk), slabs[h // 2]]) for h in range(N_HEADS)] for ks in starts]
        log_betas, leaves = [], []
        for u in range(nsub):
            sps = [_softplus2(z) for z in zs[u]]
            leaves.append([jnp.where(masks[u], sp, 0.0) for sp in sps] if masked else sps)
            log_betas.append([z - sp for z, sp in zip(zs[u], sps)])
        tails = [[_dot(s.astype(BF), upper) for s in leaves[u]] for u in range(nsub)]
        laters = list(laters)
        ws = []
        for u in range(nsub):
            ws_u = []
            for h in range(N_HEADS):
                w = jnp.exp2(log_betas[u][h] - (tails[u][h] + laters[h]))
                ws_u.append((jnp.where(masks[u], w, 0.0) if masked else w).astype(BF))
                laters[h] = laters[h] + jnp.sum(leaves[u][h], axis=1, keepdims=True)
            ws.append(ws_u)
        wvs = [functools.reduce(jnp.add, [_dot(ws[u][h], v_ref[0, pl.ds(starts[u], tk), slabs[h // 2]])
                                          for u in range(nsub)]) for h in range(N_HEADS)]
        accs_new = [accs[pr] + jnp.where(low, wvs[2 * pr], wvs[2 * pr + 1]) for pr in range(N_HEADS // 2)]
        return tuple(accs_new), tuple(laters)

    carry = (tuple(jnp.zeros((tq, LANES), F32) for _ in range(N_HEADS // 2)),
             tuple(jnp.zeros((tq, 1), F32) for _ in range(N_HEADS)))
    carry = lax.fori_loop(0, nblk - n_full, functools.partial(body, masked=True, first=nblk - 1, nsub=1), carry)

    def live(carry):
        return jnp.min(functools.reduce(jnp.minimum, carry[1])) < F32_UNDERFLOW_LOG2

    pair = functools.partial(body, masked=False, first=n_full - 1, nsub=2)
    t_end, carry = lax.while_loop(lambda c: (c[0] < n_full // 2) & live(c[1]),
                                  lambda c: (c[0] + 1, pair(c[0], c[1])), (jnp.int32(0), carry))
    last = jnp.where((t_end == n_full // 2) & live(carry), n_full % 2, 0)
    accs, _ = lax.fori_loop(0, last, functools.partial(body, masked=False, first=0, nsub=1), carry)
    for pr in range(N_HEADS // 2):
        o_ref[0, :, pr * LANES:(pr + 1) * LANES] = accs[pr].astype(o_ref.dtype)


def _resident_spec(kp, width, col):
    return pl.BlockSpec((1, kp, width), lambda bi, i: (bi, 0, col), pipeline_mode=pl.Buffered(1))


def _kv_spec(kp, col):
    return _resident_spec(kp, BRANCH_W, col)


def _mixer_b(q_all, k_arr, v_arr, kv_col, tq, tk, q_start):
    b, l, _ = q_all.shape
    kp = k_arr.shape[1]
    upper = (np.arange(tk)[:, None] > np.arange(tk)[None, :]).astype(np.float32)
    return pl.pallas_call(
        functools.partial(_mixer_b_kernel, tq=tq, tk=tk, q_start=q_start),
        grid=(b, l // tq),
        in_specs=[pl.BlockSpec((1, tq, BRANCH_W), lambda bi, i: (bi, i, 1)), _kv_spec(kp, kv_col),
                  _kv_spec(kp, kv_col), pl.BlockSpec((tk, tk), lambda bi, i: (0, 0))],
        out_specs=pl.BlockSpec((1, tq, BRANCH_W), lambda bi, i: (bi, i, 0)),
        out_shape=jax.ShapeDtypeStruct((b, l, BRANCH_W), BF),
        compiler_params=_cparams("parallel", "parallel"), name="mixer_b",
    )(q_all, k_arr, v_arr, jnp.asarray(upper, BF))


def _mixer_c_kernel(q_ref, iq_ref, iw_ref, ik_ref, k_ref, vt_ref, lower_ref, o_ref, key_ref, plane_ref, tied_ref,
                    *, tq, tk, q_start, n_keys, topk):
    q0 = q_start + pl.program_id(1) * tq
    last_adm = jnp.minimum(((q0 + tq - 1) // CHUNK) * CHUNK + CHUNK - 1, n_keys - 1)
    nblk = last_adm // tk + 1
    n_full = jnp.minimum((q0 // CHUNK + 1) * CHUNK, n_keys) // tk
    qchunk = (q0 + lax.broadcasted_iota(jnp.int32, (tk, tq), 1)) // CHUNK
    kiota = lax.broadcasted_iota(jnp.int32, (tk, tq), 0)
    topk_f = jnp.float32(topk)
    iq_pairs = _stacked_pair_heads(iq_ref)
    iw_rows = [iw_ref[0, g:g + 1, :] for g in range(H_IDX)]
    slabs = [slice(pr * LANES, (pr + 1) * LANES) for pr in range(N_HEADS // 2)]

    def score_body(t, carry, masked, first, nsub):
        blocks = [first + t * nsub + u for u in range(nsub)]
        starts = [pl.multiple_of(j * tk, tk) for j in blocks]
        dots = [_pair_logits(ik_ref[0, pl.ds(ks, tk), :], iq_pairs, tq) for ks in starts]
        for u in range(nsub):
            sc = None
            for g in range(H_IDX):
                term = iw_rows[g] * jnp.maximum(dots[u][g], 0.0)
                sc = term if sc is None else sc + term
            sc = jnp.where(sc == 0.0, 0.0, sc)
            bits = pltpu.bitcast(sc, jnp.int32)
            key = bits ^ ((bits >> 31) & jnp.int32(0x7FFFFFFF))
            if masked:
                kpos = starts[u] + kiota
                adm = ((kpos // CHUNK) <= qchunk) & (kpos < n_keys)
                key = jnp.where(adm, key, jnp.int32(INT_MIN))
            key_ref[blocks[u]] = key
        return carry

    lax.fori_loop(0, n_full // 2, functools.partial(score_body, masked=False, first=0, nsub=2), 0)
    lax.fori_loop(0, n_full % 2, functools.partial(score_body, masked=False, first=n_full - 1, nsub=1), 0)
    lax.fori_loop(0, nblk - n_full, functools.partial(score_body, masked=True, first=n_full, nsub=1), 0)

    grp_per_blk = tk // (32 * SUBLANES)
    n_quad = (nblk + BLK_PER_STEP - 1) // BLK_PER_STEP
    grp_per_step = BLK_PER_STEP * grp_per_blk

    def fill(j, carry):
        key_ref[j] = jnp.full((tk, tq), INT_MIN, jnp.int32)
        return carry

    lax.fori_loop(nblk, n_quad * BLK_PER_STEP, fill, 0)

    def transpose_block(j, carry):
        for half in range(grp_per_blk):
            base = half * 32 * SUBLANES
            w = [key_ref[j, base + c * SUBLANES:base + (c + 1) * SUBLANES, :] ^ jnp.int32(INT_MIN) for c in range(32)]
            s, m = 16, 0x0000FFFF
            while s:
                k = 0
                while k < 32:
                    t = (w[k] ^ (w[k + s] >> s)) & jnp.int32(m)
                    w[k] = w[k] ^ t
                    w[k + s] = w[k + s] ^ (t << s)
                    k = (k + s + 1) & ~s
                s >>= 1
                m ^= (m << s) & 0xFFFFFFFF
            g = j * grp_per_blk + half
            for t in range(32):
                plane_ref[g, t] = w[t]
            tied_ref[g] = jnp.full((SUBLANES, tq), -1, jnp.int32)
        return carry

    lax.fori_loop(0, n_quad * BLK_PER_STEP, transpose_block, 0)

    def bit_body(t, carry):
        prefix, n_above = carry

        def tally(i, acc):
            return acc + _tree_sum([lax.population_count(tied_ref[i * grp_per_step + u]
                                                         & plane_ref[i * grp_per_step + u, t])
                                    for u in range(grp_per_step)])

        ones = lax.fori_loop(0, n_quad, tally, jnp.zeros((SUBLANES, tq), jnp.int32))
        cnt = n_above + jnp.sum(ones.astype(F32), axis=0, keepdims=True)
        take = cnt >= topk_f

        def narrow(i, c):
            for u in range(grp_per_step):
                g = i * grp_per_step + u
                tied = tied_ref[g]
                hit = tied & plane_ref[g, t]
                tied_ref[g] = jnp.where(take, hit, tied ^ hit)
            return c

        lax.fori_loop(0, n_quad, narrow, 0)
        bit = jnp.left_shift(jnp.int32(1), 31 - t)
        return jnp.where(take, prefix | bit, prefix), jnp.where(take, n_above, cnt)

    prefix, n_above = lax.fori_loop(0, 32, bit_body, (jnp.zeros((1, tq), jnp.int32), jnp.zeros((1, tq), F32)))
    thr = jnp.maximum(prefix ^ jnp.int32(INT_MIN), jnp.int32(INT_MIN + 1))

    def tally_tied(i, acc):
        return acc + _tree_sum([lax.population_count(tied_ref[i * grp_per_step + u]) for u in range(grp_per_step)])

    n_tied = lax.fori_loop(0, n_quad, tally_tied, jnp.zeros((SUBLANES, tq), jnp.int32))
    n_tied = jnp.sum(n_tied.astype(F32), axis=0, keepdims=True)
    n_ge = jnp.where(prefix == 0, 0.0, n_above + n_tied)
    need = topk_f - n_above

    @pl.when(jnp.max(n_ge) > topk_f)
    def _():
        def fix(j, seen):
            kb = key_ref[j]
            eq = jnp.where(kb == thr, 1.0, 0.0)
            rank = _dot(lower_ref[...], eq.astype(BF)) + seen
            key_ref[j] = jnp.where(eq * rank > need, jnp.int32(INT_MIN), kb)
            return seen + jnp.sum(eq, axis=0, keepdims=True)
        lax.fori_loop(0, nblk, fix, jnp.zeros((1, tq), F32))

    q_pairs = _stacked_pair_heads(q_ref)

    def body(t, carry, first, nsub):
        ms, ls, accs = carry
        blocks = [first + t * nsub + u for u in range(nsub)]
        starts = [pl.multiple_of(j * tk, tk) for j in blocks]
        raw = [_pair_logits(k_ref[0, pl.ds(ks, tk), :], q_pairs, tq) for ks in starts]
        selected = [key_ref[j] >= thr for j in blocks]
        ms_new, ls_new, alphas, ps = [], [], [], []
        for h in range(N_HEADS):
            logits = [jnp.where(selected[u], raw[u][h], NEG) for u in range(nsub)]
            m_new = functools.reduce(jnp.maximum, [ms[h]] + [jnp.max(s, axis=0, keepdims=True) for s in logits])
            alpha = jnp.exp2(ms[h] - m_new)
            p = [jnp.exp2(s - m_new) for s in logits]
            ms_new.append(m_new)
            ls_new.append(alpha * ls[h] + functools.reduce(jnp.add, [jnp.sum(pu, axis=0, keepdims=True) for pu in p]))
            alphas.append(alpha)
            ps.append([pu.astype(BF) for pu in p])
        accs_new = []
        for pr in range(N_HEADS // 2):
            pv = functools.reduce(jnp.add, [_dot(vt_ref[0, blocks[u], slabs[pr], :],
                                                 jnp.concatenate([ps[2 * pr][u], ps[2 * pr + 1][u]], axis=1))
                                            for u in range(nsub)])
            for hh in range(2):
                h = 2 * pr + hh
                accs_new.append(alphas[h] * accs[h] + pv[hh * HEAD_DIM:(hh + 1) * HEAD_DIM, hh * tq:(hh + 1) * tq])
        return tuple(ms_new), tuple(ls_new), tuple(accs_new)

    carry = (tuple(jnp.full((1, tq), NEG, F32) for _ in range(N_HEADS)),
             tuple(jnp.zeros((1, tq), F32) for _ in range(N_HEADS)),
             tuple(jnp.zeros((HEAD_DIM, tq), F32) for _ in range(N_HEADS)))
    carry = lax.fori_loop(0, nblk // 4, functools.partial(body, first=0, nsub=4), carry)
    carry = lax.fori_loop(0, (nblk % 4) // 2, functools.partial(body, first=nblk - nblk % 4, nsub=2), carry)
    _, ls, accs = lax.fori_loop(0, nblk % 2, functools.partial(body, first=nblk - 1, nsub=1), carry)
    out_t = jnp.concatenate([accs[h] / ls[h] for h in range(N_HEADS)], axis=0)
    o_ref[0] = out_t.T.astype(o_ref.dtype)


BLK_PER_STEP = 8


def _mixer_c(q_all, iq, iw, ik_arr, k_arr, v_arr, kv_col, tq, tk, q_start, n_keys):
    b, l, _ = q_all.shape
    l_pad = _round_up(l, tq)
    q_all, iq, iw = (_pad_rows(t, l_pad) for t in (q_all, iq, iw))
    kp = k_arr.shape[1]
    n_kblk = kp // tk
    topk = min(TOPK_MAX, n_keys // 4)
    lower = (np.arange(tk)[:, None] >= np.arange(tk)[None, :]).astype(np.float32)
    n_blk_alloc = _round_up(n_kblk, BLK_PER_STEP)
    n_grp = n_blk_alloc * (tk // (32 * SUBLANES))
    vt = v_arr[:, :, kv_col * BRANCH_W:(kv_col + 1) * BRANCH_W].reshape(b, n_kblk, tk, BRANCH_W).transpose(0, 1, 3, 2)
    return pl.pallas_call(
        functools.partial(_mixer_c_kernel, tq=tq, tk=tk, q_start=q_start, n_keys=n_keys, topk=topk),
        grid=(b, l_pad // tq),
        in_specs=[pl.BlockSpec((1, tq, BRANCH_W), lambda bi, i: (bi, i, 2)),
                  pl.BlockSpec((1, tq, H_IDX * D_IDX), lambda bi, i: (bi, i, 0)),
                  pl.BlockSpec((1, H_IDX, tq), lambda bi, i: (bi, 0, i)),
                  _resident_spec(kp, 2 * D_IDX, 0),
                  _kv_spec(kp, kv_col),
                  pl.BlockSpec((1, n_kblk, BRANCH_W, tk), lambda bi, i: (bi, 0, 0, 0), pipeline_mode=pl.Buffered(1)),
                  pl.BlockSpec((tk, tk), lambda bi, i: (0, 0))],
        out_specs=pl.BlockSpec((1, tq, BRANCH_W), lambda bi, i: (bi, i, 0)),
        out_shape=jax.ShapeDtypeStruct((b, l_pad, BRANCH_W), BF),
        scratch_shapes=[pltpu.VMEM((n_blk_alloc, tk, tq), jnp.int32),
                        pltpu.VMEM((n_grp, 32, SUBLANES, tq), jnp.int32), pltpu.VMEM((n_grp, SUBLANES, tq), jnp.int32)],
        compiler_params=_cparams("parallel", "parallel"), name="mixer_c",
    )(q_all, iq, jnp.moveaxis(iw, 1, 2), ik_arr, k_arr, vt, jnp.asarray(lower, BF))[:, :l]


def _mixer_d_kernel(q_ref, k_ref, v_ref, fq_ref, fk_ref, kbound_ref, fbound_ref, o_ref,
                    *, tq, tk, q_start, n_kblk, step):
    q0 = q_start + pl.program_id(1) * tq
    nblk = (q0 + tq - 1) // tk + 1
    n_full = (q0 + 1) // tk
    qpos = q0 + lax.broadcasted_iota(jnp.int32, (tq, tk), 0)
    kiota = lax.broadcasted_iota(jnp.int32, (tq, tk), 1)
    qm = _masked_pair_heads(q_ref)
    fq = [fq_ref[0, :, h:h + 1] for h in range(N_HEADS)]
    low = _low_half((tq, LANES))

    slabs = [slice(pr * LANES, (pr + 1) * LANES) for pr in range(N_HEADS // 2)]

    def body(t, carry, masked, first, nsub, sign):
        blocks = [first + sign * (t * nsub + u) for u in range(nsub)]
        starts = [pl.multiple_of(j * tk, tk) for j in blocks]
        raw = [[_dot_nt(qm[h], k_ref[0, pl.ds(ks, tk), slabs[h // 2]]) for h in range(N_HEADS)] for ks in starts]
        logits = []
        for u in range(nsub):
            row = [raw[u][h] + fq[h] - fk_ref[0, pl.ds(h * n_kblk + blocks[u], 1), :] for h in range(N_HEADS)]
            if masked:
                visible = (starts[u] + kiota) <= qpos
                row = [jnp.where(visible, s, NEG) for s in row]
            logits.append(row)
        return _softmax_step(carry, logits, lambda u, pr: v_ref[0, pl.ds(starts[u], tk), slabs[pr]], low)

    q_norm = [1.001 * jnp.sqrt(jnp.sum(jnp.square(qm[h].astype(F32)), axis=1, keepdims=True)) for h in range(N_HEADS)]
    bi = pl.program_id(0)

    def live(carry, j):
        gap = [q_norm[h] * kbound_ref[bi, h * n_kblk + j] + fq[h] + fbound_ref[bi, h * n_kblk + j] - carry[0][h]
               for h in range(N_HEADS)]
        return jnp.max(functools.reduce(jnp.maximum, gap)) > -(F32_UNDERFLOW_LOG2 + 8.0)

    carry = lax.fori_loop(0, nblk - n_full, functools.partial(body, masked=True, first=n_full, nsub=1, sign=1),
                          _softmax_init(tq))
    group = functools.partial(body, masked=False, first=n_full - 1, nsub=step, sign=-1)
    t_end, carry = lax.while_loop(
        lambda c: (c[0] < n_full // step) & live(c[1], jnp.maximum(n_full - 1 - c[0] * step, 0)),
        lambda c: (c[0] + 1, group(c[0], c[1])), (jnp.int32(0), carry))
    n_rest = n_full % step
    rest = jnp.where((t_end == n_full // step) & live(carry, jnp.maximum(n_rest - 1, 0)), n_rest, 0)
    carry = lax.fori_loop(0, rest, functools.partial(body, masked=False, first=n_rest - 1, nsub=1, sign=-1), carry)
    _softmax_finish(carry, o_ref, low)


def _mixer_d(q_all, k_arr, v_arr, kv_col, fq, fk, k_sq, tq, tk, q_start, step):
    b, l, _ = q_all.shape
    kp = k_arr.shape[1]
    n_kblk = kp // tk
    k_norm = 1.001 * jnp.sqrt(jnp.max(k_sq.reshape(b, n_kblk, tk, N_HEADS), axis=2))
    kbound = jnp.moveaxis(lax.cummax(k_norm, axis=1), 1, 2).reshape(b, N_HEADS * n_kblk)
    fbound = lax.cummax(jnp.max(-fk, axis=-1).reshape(b, N_HEADS, n_kblk), axis=2).reshape(b, N_HEADS * n_kblk)
    smem = pl.BlockSpec(memory_space=pltpu.SMEM)
    return pl.pallas_call(
        functools.partial(_mixer_d_kernel, tq=tq, tk=tk, q_start=q_start, n_kblk=n_kblk, step=step),
        grid=(b, l // tq),
        in_specs=[pl.BlockSpec((1, tq, BRANCH_W), lambda bi, i: (bi, i, 3)), _kv_spec(kp, kv_col),
                  _kv_spec(kp, kv_col),
                  pl.BlockSpec((1, tq, N_HEADS), lambda bi, i: (bi, i, 0)),
                  pl.BlockSpec((1, N_HEADS * n_kblk, tk), lambda bi, i: (bi, 0, 0)), smem, smem],
        out_specs=pl.BlockSpec((1, tq, BRANCH_W), lambda bi, i: (bi, i, 0)),
        out_shape=jax.ShapeDtypeStruct((b, l, BRANCH_W), BF),
        compiler_params=_cparams("parallel", "parallel"), name="mixer_d",
    )(q_all, k_arr, v_arr, fq, fk, kbound, fbound)


def _merge_kernel(x_ref, g1_ref, oa_ref, ob_ref, oc_ref, od_ref, wg_ref, wb_ref, wo_ref, y_ref):
    x = x_ref[...]
    h = _rms(x, g1_ref[...]).astype(BF)
    merged = None
    for g, o_ref in enumerate((oa_ref, ob_ref, oc_ref, od_ref)):
        gate = 1.0 / (1.0 + jnp.exp(-_dot(h, wg_ref[:, g * D_MODEL:(g + 1) * D_MODEL])))
        term = gate * _dot(o_ref[...], wb_ref[g])
        merged = term if merged is None else merged + term
    y_ref[...] = x + _dot(merged.astype(BF), wo_ref[...])


def _merge(x2, outs, lp, tm):
    m_rows = x2.shape[0]
    row = lambda i: (i, 0)
    const = lambda i: (0, 0)
    return pl.pallas_call(
        _merge_kernel, grid=(m_rows // tm,),
        in_specs=[pl.BlockSpec((tm, D_MODEL), row), pl.BlockSpec((1, D_MODEL), const)]
                 + [pl.BlockSpec((tm, BRANCH_W), row)] * N_MIXERS
                 + [pl.BlockSpec((D_MODEL, N_MIXERS * D_MODEL), const),
                    pl.BlockSpec((N_MIXERS, BRANCH_W, D_MODEL), lambda i: (0, 0, 0)),
                    pl.BlockSpec((D_MODEL, D_MODEL), const)],
        out_specs=pl.BlockSpec((tm, D_MODEL), row),
        out_shape=jax.ShapeDtypeStruct((m_rows, D_MODEL), F32),
        compiler_params=_cparams("parallel"), name="merge",
    )(x2, lp["g1"], *outs, lp["w_gate"], lp["w_branch"], lp["w_o"])


def _gelu_tanh(x):
    return x * (0.5 * (1.0 + jnp.tanh(np.sqrt(2.0 / np.pi).astype(np.float32) * (x + 0.044715 * (x * x * x)))))


def _ffn_kernel(x_ref, xp_ref, st_ref, g2_ref, win_ref, wc_ref, bc_ref, wd_ref, y_ref, ut_ref, h_ref, a_ref,
                *, tm, tf, tiles_per_seq):
    first_tile = pl.program_id(0) % tiles_per_seq == 0
    g2 = g2_ref[...]
    x = x_ref[...]
    h_ref[:CTX_ROWS, :] = _rms(xp_ref[...], g2).astype(BF)
    h_ref[CTX_ROWS:, :] = _rms(x, g2).astype(BF)
    h = h_ref[...]
    for f in range(0, D_FF, tf):
        u = _dot(h, win_ref[:, f:f + tf])
        gt = _dot(h, win_ref[:, D_FF + f:D_FF + f + tf])[CTX_ROWS:]
        ctx = jnp.where(first_tile, st_ref[0, :, f:f + tf], u[:CTX_ROWS])
        ue = jnp.concatenate([ctx, u[CTX_ROWS:]], axis=0)
        wc = wc_ref[:, f:f + tf]
        conv = bc_ref[:, f:f + tf] + ((ue[CTX_ROWS - 2:CTX_ROWS - 2 + tm] * wc[0:1]
                                       + ue[CTX_ROWS - 1:CTX_ROWS - 1 + tm] * wc[1:2]) + ue[CTX_ROWS:] * wc[2:3])
        a_ref[:, f:f + tf] = (_gelu_tanh(conv) * gt).astype(BF)
        ut_ref[0, :, f:f + tf] = ue[tm:]
    y_ref[...] = x + _dot(a_ref[...], wd_ref[...])


def _ffn(x2, state8, lp, tm, tf, seq_len):
    m_rows = x2.shape[0]
    tiles_per_seq = seq_len // tm
    const = lambda i: (0, 0)
    once = pl.Buffered(1)
    return pl.pallas_call(
        functools.partial(_ffn_kernel, tm=tm, tf=tf, tiles_per_seq=tiles_per_seq),
        grid=(m_rows // tm,),
        in_specs=[pl.BlockSpec((tm, D_MODEL), lambda i: (i, 0)),
                  pl.BlockSpec((CTX_ROWS, D_MODEL), lambda i: (jnp.maximum(i * (tm // CTX_ROWS) - 1, 0), 0)),
                  pl.BlockSpec((1, CTX_ROWS, D_FF), lambda i: (i // tiles_per_seq, 0, 0)),
                  pl.BlockSpec((1, D_MODEL), const),
                  pl.BlockSpec((D_MODEL, 2 * D_FF), const, pipeline_mode=once),
                  pl.BlockSpec((CONV_W, D_FF), const),
                  pl.BlockSpec((1, D_FF), const),
                  pl.BlockSpec((D_FF, D_MODEL), const, pipeline_mode=once)],
        out_specs=[pl.BlockSpec((tm, D_MODEL), lambda i: (i, 0)),
                   pl.BlockSpec((1, CTX_ROWS, D_FF), lambda i: (i, 0, 0))],
        out_shape=[jax.ShapeDtypeStruct((m_rows, D_MODEL), F32),
                   jax.ShapeDtypeStruct((m_rows // tm, CTX_ROWS, D_FF), F32)],
        scratch_shapes=[pltpu.VMEM((tm + CTX_ROWS, D_MODEL), BF), pltpu.VMEM((tm, D_FF), BF)],
        compiler_params=_cparams("parallel"), name="ffn",
    )(x2, x2, state8, lp["g2"], lp["w_ffn_in"], lp["w_conv"], lp["b_conv"], lp["w_down"])


def _rope_tables(q_start, length):
    half = HEAD_DIM // 2
    inv = ROPE_THETA ** (-jnp.arange(half, dtype=F32) / half)
    ang = (q_start + jnp.arange(length)).astype(F32)[:, None] * inv[None, :]
    cos = jnp.cos(ang)
    sin = jnp.sin(ang)
    cos_h = jnp.concatenate([cos, cos], axis=1)
    sin_h = jnp.concatenate([-sin, sin], axis=1)
    n_rot = H_IDX + 1
    pad = MISC_W - n_rot * HEAD_DIM
    return {
        "cq": jnp.tile(cos_h, (1, N_HEADS)), "sq": jnp.tile(sin_h, (1, N_HEADS)),
        "cm": jnp.concatenate([jnp.tile(cos_h, (1, n_rot)), jnp.ones((length, pad), F32)], axis=1),
        "sm": jnp.concatenate([jnp.tile(sin_h, (1, n_rot)), jnp.zeros((length, pad), F32)], axis=1),
    }


def _layer_params(norm1_g, w_in, b_forget, qk_g, rel_table, w_branch, w_o, norm2_g, w_ffn_in, w_conv, b_conv,
                  w_down):
    a = 3 * QKV_W
    i1 = a + H_IDX * D_IDX + D_IDX + H_IDX
    d1 = i1 + QKV_W
    f1 = d1 + N_HEADS
    n_misc = (i1 - a) + N_HEADS
    w_misc = jnp.concatenate([w_in[:, a:i1], w_in[:, d1:f1], jnp.zeros((D_MODEL, MISC_W - n_misc), w_in.dtype)],
                             axis=1)
    f_lane = D_IDX + H_IDX
    bf_row = jnp.zeros((1, LANES), F32).at[0, f_lane:f_lane + N_HEADS].set(b_forget.astype(F32))
    head_ones = (np.arange(BRANCH_W)[:, None] // HEAD_DIM == np.arange(BRANCH_W)[None, :] // HEAD_DIM)
    return {
        "g1": norm1_g.astype(F32)[None, :], "g2": norm2_g.astype(F32)[None, :],
        "w_qkv": jnp.concatenate([w_in[:, :a], w_in[:, i1:d1]], axis=1).astype(BF),
        "w_misc": w_misc.astype(BF), "w_gate": w_in[:, f1:].astype(BF),
        "qkg": jnp.tile(qk_g.astype(F32), (1, N_HEADS)), "bf_row": bf_row,
        "head_ones": jnp.asarray(head_ones.astype(np.float32), BF),
        "rel_table": rel_table, "w_branch": w_branch.astype(BF), "w_o": w_o.astype(BF),
        "w_ffn_in": w_ffn_in.astype(BF), "w_conv": w_conv.astype(F32), "b_conv": b_conv.astype(F32)[None, :],
        "w_down": w_down.astype(BF),
    }


def _pad_rows(a, rows):
    return jnp.pad(a, ((0, 0), (0, rows - a.shape[1])) + ((0, 0),) * (a.ndim - 2))


def _round_up(n, m):
    return -(-n // m) * m


def _layer(x, past, lp, tabs):
    b, l, _ = x.shape
    m_rows = b * l
    q_start = 0 if past is None else past[2].shape[1]
    n_keys = q_start + l
    tq = min(256, l)
    tq_c = LANES
    tk = 256
    tk_d = 512
    kp = _round_up(n_keys, max(tk, tk_d))
    tm = min(256, m_rows)

    x2 = x.reshape(m_rows, D_MODEL)
    (q_all, kbf, vbf, ka, va, kb, vb, kc, vc, kd, vd, iq, misc) = _projection(x2, lp, tabs, tm)
    ik = misc[:, :D_IDX].reshape(b, l, D_IDX)
    iw = misc[:, D_IDX:D_IDX + H_IDX].reshape(b, l, H_IDX)
    logf = misc[:, D_IDX + H_IDX:D_IDX + H_IDX + N_HEADS].reshape(b, l, N_HEADS)
    q_all = q_all.reshape(b, l, N_MIXERS * BRANCH_W)
    iq = iq.reshape(b, l, H_IDX * D_IDX)
    heads = lambda t: t.reshape(b, l, N_HEADS, HEAD_DIM)
    flat = lambda t: t.reshape(t.shape[0], t.shape[1], BRANCH_W)

    if past is None:
        kbf3 = kbf.reshape(b, l, N_MIXERS * BRANCH_W)
        vbf3 = vbf.reshape(b, l, N_MIXERS * BRANCH_W)
        kv = [(kbf3, vbf3, c) for c in range(N_MIXERS)]
        ik_all = ik.astype(BF)
        logf_all = logf
        a_args = (kbf3, vbf3, 0, _band_bias(lp["rel_table"], tq, BAND_PAST + tq, BAND_PAST + tq), tq, 3, tq)
        conv_state = jnp.zeros((b, CONV_W - 1, D_FF), F32)
    else:
        (pa_k, pa_v, pb_k, pb_v, pc_k, pc_v, pc_ki, pd_k, pd_v, pd_f, conv_state) = past
        new_bf = lambda t, c: t.reshape(b, l, N_MIXERS, BRANCH_W)[:, :, c]
        cat = lambda p, c, src, rows: _pad_rows(jnp.concatenate([flat(p).astype(BF), new_bf(src, c)], axis=1), rows)
        kv = [None] + [(cat(pk, c, kbf, kp), cat(pv, c, vbf, kp), 0)
                       for c, (pk, pv) in ((1, (pb_k, pb_v)), (2, (pc_k, pc_v)), (3, (pd_k, pd_v)))]
        ik_all = jnp.concatenate([pc_ki.astype(BF), ik.astype(BF)], axis=1)
        logf_all = jnp.concatenate([pd_f.astype(F32), logf], axis=1)
        a_keys = pa_k.shape[1] + l
        wa = _round_up(BAND_PAST + l, LANES)
        a_args = (cat(pa_k, 0, kbf, wa), cat(pa_v, 0, vbf, wa), 0,
                  _band_bias(lp["rel_table"], l, wa, a_keys), l, 1, wa)

    f_cum = jnp.cumsum(logf_all, axis=1) * LOG2E
    fq = f_cum[:, q_start:]
    fk = jnp.moveaxis(_pad_rows(f_cum, kp), 1, 2).reshape(b, N_HEADS * (kp // tk_d), tk_d)

    o_a = _mixer_a(q_all, *a_args)
    o_b = _mixer_b(q_all, *kv[1], tq, tk, q_start)
    ik_pair = _pad_rows(jnp.concatenate([ik_all, ik_all], axis=-1), kp)
    o_c = _mixer_c(q_all, iq, iw, ik_pair, *kv[2], tq_c, tk_d, q_start, n_keys)
    sq_norm = lambda t: jnp.sum(jnp.square(t.astype(BF).astype(F32)), axis=-1)
    kd_sq = sq_norm(heads(kd)) if past is None else jnp.concatenate([sq_norm(past[7]), sq_norm(heads(kd))], axis=1)
    o_d = _mixer_d(q_all, *kv[3], fq, fk, _pad_rows(kd_sq, kp), tq, tk_d, q_start, 2)
    x2 = _merge(x2, [o.reshape(m_rows, BRANCH_W) for o in (o_a, o_b, o_c, o_d)], lp, tm)

    state8 = jnp.pad(conv_state.astype(F32), ((0, 0), (CTX_ROWS - (CONV_W - 1), 0), (0, 0)))
    x2, u_tail = _ffn(x2, state8, lp, min(512, l), 256, l)
    u_tail = u_tail.reshape(b, -1, CTX_ROWS, D_FF)[:, -1]

    if past is None:
        a_keep = min(BAND_PAST, l)
        new_a = (heads(ka)[:, -a_keep:], heads(va)[:, -a_keep:])
    else:
        a_keep = pa_k.shape[1]
        new_a = (jnp.concatenate([pa_k, heads(ka)], axis=1)[:, -a_keep:],
                 jnp.concatenate([pa_v, heads(va)], axis=1)[:, -a_keep:])
    new = new_a + (heads(kb), heads(vb), heads(kc), heads(vc), ik, heads(kd), heads(vd), logf,
                   u_tail[:, -(CONV_W - 1):])
    return x2.reshape(b, l, D_MODEL), new


def kernel(x_prompt, x_sample, cache_a_k, cache_a_v, cache_b_k, cache_b_v, cache_c_k, cache_c_v, cache_c_kidx, cache_d_k, cache_d_v, cache_d_logf, state_ffn_conv, norm1_g, w_in, b_forget, qk_norm_g, rel_bias, w_branch, w_o, norm2_g, w_ffn_in, w_conv, b_conv, w_down):
    depth = w_in.shape[0]
    past_len = cache_b_k.shape[2]
    tabs_p = _rope_tables(0, x_prompt.shape[1])
    tabs_s = {k: jnp.tile(v, (x_sample.shape[0], 1)) for k, v in _rope_tables(past_len, x_sample.shape[1]).items()}
    y_p, y_s = x_prompt, x_sample
    p_states, s_states = [], []
    for d in range(depth):
        lp = _layer_params(norm1_g[d], w_in[d], b_forget[d], qk_norm_g[d], rel_bias[d], w_branch[d], w_o[d],
                           norm2_g[d], w_ffn_in[d], w_conv[d], b_conv[d], w_down[d])
        y_p, st_p = _layer(y_p, None, lp, tabs_p)
        past = (cache_a_k[d], cache_a_v[d], cache_b_k[d], cache_b_v[d], cache_c_k[d], cache_c_v[d],
                cache_c_kidx[d], cache_d_k[d], cache_d_v[d], cache_d_logf[d], state_ffn_conv[d])
        y_s, st_s = _layer(y_s, past, lp, tabs_s)
        p_states.append(st_p)
        s_states.append(st_s)
    p_out = [jnp.stack(t) for t in zip(*p_states)]
    s_out = [jnp.stack(t) for t in zip(*s_states)]
    return (y_p, y_s, *p_out, *s_out)
```

```python
import functools

import numpy as np
import jax
import jax.numpy as jnp
from jax import lax
from jax.experimental import pallas as pl
from jax.experimental.pallas import tpu as pltpu

D_MODEL = 1024
CHUNK = 64
N_MIXERS = 4
N_HEADS = 4
HEAD_DIM = 64
BRANCH_W = N_HEADS * HEAD_DIM
BAND_PAST = 8 * CHUNK
REL_CLIP = 128
H_IDX = 4
D_IDX = 64
TOPK_MAX = 256
ROPE_THETA = 10000.0
D_FF = 2816
CONV_W = 3
EPS = 1e-6

BF = jnp.bfloat16
F32 = jnp.float32
NEG = -1e30
LOG2E = float(np.log2(np.e))
F32_UNDERFLOW_LOG2 = 160.0
INT_MIN = -2 ** 31
LANES = 128
SUBLANES = 8
CTX_ROWS = 2 * SUBLANES
QKV_W = 3 * BRANCH_W
MISC_W = 3 * LANES
VMEM_LIMIT = 56 * 1024 * 1024


def _cparams(*sem):
    return pltpu.CompilerParams(dimension_semantics=sem, vmem_limit_bytes=VMEM_LIMIT)


def _dot(a, b):
    return jnp.dot(a, b, preferred_element_type=F32)


def _dot_nt(a, b):
    return lax.dot_general(a, b, (((1,), (1,)), ((), ())), preferred_element_type=F32)


def _split_dot(x, m, terms):
    acc = None
    r = x
    for t in range(terms):
        hi = r.astype(BF)
        d = _dot(hi, m)
        acc = d if acc is None else acc + d
        if t + 1 < terms:
            r = r - hi.astype(F32)
    return acc


def _rms(x, g):
    ms = jnp.mean(x * x, axis=-1, keepdims=True)
    return x * lax.rsqrt(ms + EPS) * g


def _head_rms(t, g, head_ones):
    ms = _split_dot(t * t, head_ones, 3) * (1.0 / HEAD_DIM)
    return t * lax.rsqrt(ms + EPS) * g


def _swap_halves(y):
    n = y.shape[-1]
    lane = lax.broadcasted_iota(jnp.int32, y.shape, 1)
    first = (lane & (HEAD_DIM - 1)) < (HEAD_DIM // 2)
    return jnp.where(first, pltpu.roll(y, n - HEAD_DIM // 2, 1), pltpu.roll(y, HEAD_DIM // 2, 1))


def _softplus2(z):
    neg_abs = pltpu.bitcast(pltpu.bitcast(z, jnp.int32) | jnp.int32(INT_MIN), F32)
    return jnp.maximum(z, 0.0) + jnp.log2(1.0 + jnp.exp2(neg_abs))


def _tree_sum(xs):
    while len(xs) > 1:
        xs = [xs[i] + xs[i + 1] for i in range(0, len(xs) - 1, 2)] + ([xs[-1]] if len(xs) % 2 else [])
    return xs[0]


def _low_half(shape):
    return lax.broadcasted_iota(jnp.int32, shape, 1) < HEAD_DIM


def _masked_pair_heads(ref):
    out = []
    for pr in range(ref.shape[-1] // LANES):
        slab = ref[0, :, pr * LANES:(pr + 1) * LANES].astype(F32)
        low = _low_half(slab.shape)
        out.append(jnp.where(low, slab, 0.0).astype(BF))
        out.append(jnp.where(low, 0.0, slab).astype(BF))
    return out


def _stacked_pair_heads(ref):
    masked = _masked_pair_heads(ref)
    return [jnp.concatenate(masked[2 * pr:2 * pr + 2], axis=0) for pr in range(len(masked) // 2)]


def _pair_logits(keys, stacked, tq):
    out = []
    for pr, rhs in enumerate(stacked):
        slab = keys if keys.shape[-1] == LANES else keys[:, pr * LANES:(pr + 1) * LANES]
        both = _dot_nt(slab, rhs)
        out += [both[:, :tq], both[:, tq:]]
    return out


def _softmax_init(tq):
    return (tuple(jnp.full((tq, 1), NEG, F32) for _ in range(N_HEADS)),
            tuple(jnp.zeros((tq, 1), F32) for _ in range(N_HEADS)),
            tuple(jnp.zeros((tq, LANES), F32) for _ in range(N_HEADS // 2)))


def _softmax_step(carry, logits, v_tile, low):
    ms, ls, accs = carry
    tiles = range(len(logits))
    ms_new, ls_new, alphas, ps = [], [], [], []
    for h in range(N_HEADS):
        m_new = functools.reduce(jnp.maximum, [ms[h]] + [jnp.max(logits[u][h], axis=1, keepdims=True) for u in tiles])
        alpha = jnp.exp2(ms[h] - m_new)
        p = [jnp.exp2(logits[u][h] - m_new) for u in tiles]
        ms_new.append(m_new)
        ls_new.append(alpha * ls[h] + functools.reduce(jnp.add, [jnp.sum(pu, axis=1, keepdims=True) for pu in p]))
        alphas.append(alpha)
        ps.append([pu.astype(BF) for pu in p])
    pvs = [functools.reduce(jnp.add, [_dot(ps[h][u], v_tile(u, h // 2)) for u in tiles]) for h in range(N_HEADS)]
    accs_new = [jnp.where(low, alphas[2 * pr], alphas[2 * pr + 1]) * accs[pr]
                + jnp.where(low, pvs[2 * pr], pvs[2 * pr + 1]) for pr in range(N_HEADS // 2)]
    return tuple(ms_new), tuple(ls_new), tuple(accs_new)


def _softmax_finish(carry, o_ref, low):
    _, ls, accs = carry
    for pr in range(N_HEADS // 2):
        l_pair = jnp.where(low, ls[2 * pr], ls[2 * pr + 1])
        o_ref[0, :, pr * LANES:(pr + 1) * LANES] = (accs[pr] / l_pair).astype(o_ref.dtype)


def _proj_kernel(x_ref, g1_ref, wqkv_ref, wmisc_ref, qkg_ref, hones_ref, cq_ref, sq_ref, cm_ref, sm_ref,
                 bf_ref, q_ref, kbf_ref, vbf_ref, ka_ref, va_ref, kb_ref, vb_ref, kc_ref, vc_ref,
                 kd_ref, vd_ref, iq_ref, misc_ref):
    h = _rms(x_ref[...], g1_ref[...]).astype(BF)
    hones = hones_ref[...]
    k32 = (ka_ref, kb_ref, kc_ref, kd_ref)
    v32 = (va_ref, vb_ref, vc_ref, vd_ref)
    norm_row = (0, None, 2, 4)
    for m in range(N_MIXERS):
        y = _dot(h, wqkv_ref[:, m * QKV_W:(m + 1) * QKV_W])
        q, k, v = y[:, :BRANCH_W], y[:, BRANCH_W:2 * BRANCH_W], y[:, 2 * BRANCH_W:]
        if norm_row[m] is not None:
            r = norm_row[m]
            q = _head_rms(q, qkg_ref[r:r + 1, :], hones)
            k = _head_rms(k, qkg_ref[r + 1:r + 2, :], hones)
        if m == 2:
            c, s = cq_ref[...], sq_ref[...]
            q = q * c + _swap_halves(q) * s
            k = k * c + _swap_halves(k) * s
        cols = slice(m * BRANCH_W, (m + 1) * BRANCH_W)
        q_ref[:, cols] = (q * (HEAD_DIM ** -0.5 * LOG2E)).astype(BF)
        kbf_ref[:, cols] = k.astype(BF)
        vbf_ref[:, cols] = v.astype(BF)
        k32[m][...] = k
        v32[m][...] = v
    ym = _dot(h, wmisc_ref[...])
    r = ym * cm_ref[...] + _swap_halves(ym) * sm_ref[...]
    iq_ref[...] = r[:, :H_IDX * D_IDX].astype(BF)
    g2 = r[:, H_IDX * D_IDX:]
    lane = lax.broadcasted_iota(jnp.int32, g2.shape, 1)
    z = g2 + bf_ref[...]
    logf = jnp.minimum(z, 0.0) - jnp.log1p(jnp.exp(-jnp.abs(z)))
    is_w = (lane >= D_IDX) & (lane < D_IDX + H_IDX)
    is_f = (lane >= D_IDX + H_IDX) & (lane < D_IDX + H_IDX + N_HEADS)
    misc_ref[...] = jnp.where(is_f, logf, jnp.where(is_w, g2 * (H_IDX ** -0.5 * D_IDX ** -0.5), g2))


def _projection(x2, lp, tabs, tm):
    m_rows = x2.shape[0]
    n_tab = tabs["cq"].shape[0] // tm
    row = lambda i: (i, 0)
    const = lambda i: (0, 0)
    tab = lambda i: (i % n_tab, 0)
    wide = pl.BlockSpec((tm, N_MIXERS * BRANCH_W), row)
    head = pl.BlockSpec((tm, BRANCH_W), row)
    in_specs = [
        pl.BlockSpec((tm, D_MODEL), row),
        pl.BlockSpec((1, D_MODEL), const),
        pl.BlockSpec((D_MODEL, N_MIXERS * QKV_W), const),
        pl.BlockSpec((D_MODEL, MISC_W), const),
        pl.BlockSpec((6, BRANCH_W), const),
        pl.BlockSpec((BRANCH_W, BRANCH_W), const),
        pl.BlockSpec((tm, BRANCH_W), tab),
        pl.BlockSpec((tm, BRANCH_W), tab),
        pl.BlockSpec((tm, MISC_W), tab),
        pl.BlockSpec((tm, MISC_W), tab),
        pl.BlockSpec((1, LANES), const),
    ]
    out_shape = ([jax.ShapeDtypeStruct((m_rows, N_MIXERS * BRANCH_W), BF)] * 3
                 + [jax.ShapeDtypeStruct((m_rows, BRANCH_W), F32)] * 8
                 + [jax.ShapeDtypeStruct((m_rows, BRANCH_W), BF),
                    jax.ShapeDtypeStruct((m_rows, LANES), F32)])
    out_specs = [wide] * 3 + [head] * 8 + [head, pl.BlockSpec((tm, LANES), row)]
    return pl.pallas_call(
        _proj_kernel, grid=(m_rows // tm,), in_specs=in_specs, out_specs=out_specs, out_shape=out_shape,
        compiler_params=_cparams("parallel"), name="projection",
    )(x2, lp["g1"], lp["w_qkv"], lp["w_misc"], lp["qkg"], lp["head_ones"],
      tabs["cq"], tabs["sq"], tabs["cm"], tabs["sm"], lp["bf_row"])


def _mixer_a_kernel(*refs, nwb, wb):
    q_ref = refs[0]
    k_refs = refs[1:1 + nwb]
    v_refs = refs[1 + nwb:1 + 2 * nwb]
    bias_ref = refs[1 + 2 * nwb]
    o_ref = refs[2 + 2 * nwb]
    i = pl.program_id(1)
    qm = _masked_pair_heads(q_ref)
    low = _low_half((q_ref.shape[1], LANES))
    for pr in range(N_HEADS // 2):
        slab = slice(pr * LANES, (pr + 1) * LANES)
        outs = []
        for h in (2 * pr, 2 * pr + 1):
            logits = []
            for r in range(nwb):
                s = _dot_nt(qm[h], k_refs[r][0, :, slab]) + bias_ref[h, :, r * wb:(r + 1) * wb]
                logits.append(jnp.where(i - (nwb - 1) + r >= 0, s, NEG))
            m = functools.reduce(jnp.maximum, [jnp.max(s, axis=1, keepdims=True) for s in logits])
            ps = [jnp.exp2(s - m) for s in logits]
            l = functools.reduce(jnp.add, [jnp.sum(p, axis=1, keepdims=True) for p in ps])
            acc = functools.reduce(jnp.add, [_dot(p.astype(BF), v_refs[r][0, :, slab]) for r, p in enumerate(ps)])
            outs.append(acc / l)
        o_ref[0, :, slab] = jnp.where(low, outs[0], outs[1]).astype(o_ref.dtype)


def _mixer_a(q_all, k_arr, v_arr, kv_col, bias, tq, nwb, wb):
    b, l, _ = q_all.shape
    kspec = lambda r: pl.BlockSpec((1, wb, BRANCH_W),
                                   lambda bi, i, r=r: (bi, jnp.maximum(i - (nwb - 1) + r, 0), kv_col))
    in_specs = ([pl.BlockSpec((1, tq, BRANCH_W), lambda bi, i: (bi, i, 0))]
                + [kspec(r) for r in range(nwb)] * 2
                + [pl.BlockSpec(bias.shape, lambda bi, i: (0, 0, 0))])
    return pl.pallas_call(
        functools.partial(_mixer_a_kernel, nwb=nwb, wb=wb),
        grid=(b, l // tq), in_specs=in_specs,
        out_specs=pl.BlockSpec((1, tq, BRANCH_W), lambda bi, i: (bi, i, 0)),
        out_shape=jax.ShapeDtypeStruct((b, l, BRANCH_W), BF),
        compiler_params=_cparams("parallel", "parallel"), name="mixer_a",
    )(q_all, *([k_arr] * nwb), *([v_arr] * nwb), bias)


def _band_bias(rel_table, tq, w, n_valid_cols):
    t = np.arange(tq)[:, None]
    c = np.arange(w)[None, :]
    krel = c - BAND_PAST
    ct = t // CHUNK
    inband = (krel >= CHUNK * ct - BAND_PAST) & (krel < CHUNK * ct + CHUNK) & (c < n_valid_cols)
    n = tq + w - 1
    idx = np.clip(np.arange(n) - (w - 1 - BAND_PAST), -REL_CLIP, REL_CLIP) + REL_CLIP
    r0, r1 = int((idx == idx[0]).sum()), int((idx == idx[-1]).sum())
    tab = rel_table.astype(F32) * LOG2E
    f = jnp.concatenate([jnp.broadcast_to(tab[idx[0]], (r0, N_HEADS)), tab[idx[r0]:idx[n - r1 - 1] + 1],
                         jnp.broadcast_to(tab[idx[-1]], (r1, N_HEADS))], axis=0).T
    hankel = jnp.tile(f, (1, tq + 1))[:, :tq * (n + 1)].reshape(N_HEADS, tq, n + 1)[:, :, :w]
    return jnp.where(inband[None], hankel[:, :, ::-1], NEG)


def _mixer_b_kernel(q_ref, k_ref, v_ref, upper_ref, o_ref, *, tq, tk, q_start):
    q0 = q_start + pl.program_id(1) * tq
    nblk = (q0 + tq - 2) // tk + 1
    upper = upper_ref[...]
    qpos = q0 + lax.broadcasted_iota(jnp.int32, (tq, tk), 0)
    kiota = lax.broadcasted_iota(jnp.int32, (tq, tk), 1)
    n_full = q0 // tk
    qm = _masked_pair_heads(q_ref)
    low = _low_half((tq, LANES))

    slabs = [slice(pr * LANES, (pr + 1) * LANES) for pr in range(N_HEADS // 2)]

    def body(t, carry, masked, first, nsub):
        accs, laters = carry
        starts = [pl.multiple_of((first - t * nsub - u) * tk, tk) for u in range(nsub)]
        masks = [(ks + kiota) < qpos for ks in starts]
        zs = [[_dot_nt(qm[h], k_ref[0, pl.ds(ks, tk), slabs[h // 2]]) for h in range(N_HEADS)] for ks in starts]
        log_betas, leaves = [], []
        for u in range(nsub):
            sps = [_softplus2(z) for z in zs[u]]
            leaves.append([jnp.where(masks[u], sp, 0.0) for sp in sps] if masked else sps)
            log_betas.append([z - sp for z, sp in zip(zs[u], sps)])
        tails = [[_dot(s.astype(BF), upper) for s in leaves[u]] for u in range(nsub)]
        laters = list(laters)
        ws = []
        for u in range(nsub):
            ws_u = []
            for h in range(N_HEADS):
                w = jnp.exp2(log_betas[u][h] - (tails[u][h] + laters[h]))
                ws_u.append((jnp.where(masks[u], w, 0.0) if masked else w).astype(BF))
                laters[h] = laters[h] + jnp.sum(leaves[u][h], axis=1, keepdims=True)
            ws.append(ws_u)
        wvs = [functools.reduce(jnp.add, [_dot(ws[u][h], v_ref[0, pl.ds(starts[u], tk), slabs[h // 2]])
                                          for u in range(nsub)]) for h in range(N_HEADS)]
        accs_new = [accs[pr] + jnp.where(low, wvs[2 * pr], wvs[2 * pr + 1]) for pr in range(N_HEADS // 2)]
        return tuple(accs_new), tuple(laters)

    carry = (tuple(jnp.zeros((tq, LANES), F32) for _ in range(N_HEADS // 2)),
             tuple(jnp.zeros((tq, 1), F32) for _ in range(N_HEADS)))
    carry = lax.fori_loop(0, nblk - n_full, functools.partial(body, masked=True, first=nblk - 1, nsub=1), carry)

    def live(carry):
        return jnp.min(functools.reduce(jnp.minimum, carry[1])) < F32_UNDERFLOW_LOG2

    pair = functools.partial(body, masked=False, first=n_full - 1, nsub=2)
    t_end, carry = lax.while_loop(lambda c: (c[0] < n_full // 2) & live(c[1]),
                                  lambda c: (c[0] + 1, pair(c[0], c[1])), (jnp.int32(0), carry))
    last = jnp.where((t_end == n_full // 2) & live(carry), n_full % 2, 0)
    accs, _ = lax.fori_loop(0, last, functools.partial(body, masked=False, first=0, nsub=1), carry)
    for pr in range(N_HEADS // 2):
        o_ref[0, :, pr * LANES:(pr + 1) * LANES] = accs[pr].astype(o_ref.dtype)


def _resident_spec(kp, width, col):
    return pl.BlockSpec((1, kp, width), lambda bi, i: (bi, 0, col), pipeline_mode=pl.Buffered(1))


def _kv_spec(kp, col):
    return _resident_spec(kp, BRANCH_W, col)


def _mixer_b(q_all, k_arr, v_arr, kv_col, tq, tk, q_start):
    b, l, _ = q_all.shape
    kp = k_arr.shape[1]
    upper = (np.arange(tk)[:, None] > np.arange(tk)[None, :]).astype(np.float32)
    return pl.pallas_call(
        functools.partial(_mixer_b_kernel, tq=tq, tk=tk, q_start=q_start),
        grid=(b, l // tq),
        in_specs=[pl.BlockSpec((1, tq, BRANCH_W), lambda bi, i: (bi, i, 1)), _kv_spec(kp, kv_col),
                  _kv_spec(kp, kv_col), pl.BlockSpec((tk, tk), lambda bi, i: (0, 0))],
        out_specs=pl.BlockSpec((1, tq, BRANCH_W), lambda bi, i: (bi, i, 0)),
        out_shape=jax.ShapeDtypeStruct((b, l, BRANCH_W), BF),
        compiler_params=_cparams("parallel", "parallel"), name="mixer_b",
    )(q_all, k_arr, v_arr, jnp.asarray(upper, BF))


def _mixer_c_kernel(q_ref, iq_ref, iw_ref, ik_ref, k_ref, vt_ref, lower_ref, o_ref, key_ref, plane_ref, tied_ref,
                    *, tq, tk, q_start, n_keys, topk):
    q0 = q_start + pl.program_id(1) * tq
    last_adm = jnp.minimum(((q0 + tq - 1) // CHUNK) * CHUNK + CHUNK - 1, n_keys - 1)
    nblk = last_adm // tk + 1
    n_full = jnp.minimum((q0 // CHUNK + 1) * CHUNK, n_keys) // tk
    qchunk = (q0 + lax.broadcasted_iota(jnp.int32, (tk, tq), 1)) // CHUNK
    kiota = lax.broadcasted_iota(jnp.int32, (tk, tq), 0)
    topk_f = jnp.float32(topk)
    iq_pairs = _stacked_pair_heads(iq_ref)
    iw_rows = [iw_ref[0, g:g + 1, :] for g in range(H_IDX)]
    slabs = [slice(pr * LANES, (pr + 1) * LANES) for pr in range(N_HEADS // 2)]

    def score_body(t, carry, masked, first, nsub):
        blocks = [first + t * nsub + u for u in range(nsub)]
        starts = [pl.multiple_of(j * tk, tk) for j in blocks]
        dots = [_pair_logits(ik_ref[0, pl.ds(ks, tk), :], iq_pairs, tq) for ks in starts]
        for u in range(nsub):
            sc = None
            for g in range(H_IDX):
                term = iw_rows[g] * jnp.maximum(dots[u][g], 0.0)
                sc = term if sc is None else sc + term
            sc = jnp.where(sc == 0.0, 0.0, sc)
            bits = pltpu.bitcast(sc, jnp.int32)
            key = bits ^ ((bits >> 31) & jnp.int32(0x7FFFFFFF))
            if masked:
                kpos = starts[u] + kiota
                adm = ((kpos // CHUNK) <= qchunk) & (kpos < n_keys)
                key = jnp.where(adm, key, jnp.int32(INT_MIN))
            key_ref[blocks[u]] = key
        return carry

    lax.fori_loop(0, n_full // 2, functools.partial(score_body, masked=False, first=0, nsub=2), 0)
    lax.fori_loop(0, n_full % 2, functools.partial(score_body, masked=False, first=n_full - 1, nsub=1), 0)
    lax.fori_loop(0, nblk - n_full, functools.partial(score_body, masked=True, first=n_full, nsub=1), 0)

    grp_per_blk = tk // (32 * SUBLANES)
    n_quad = (nblk + BLK_PER_STEP - 1) // BLK_PER_STEP
    grp_per_step = BLK_PER_STEP * grp_per_blk

    def blank_block(j, carry):
        for half in range(grp_per_blk):
            g = j * grp_per_blk + half
            for t in range(32):
                plane_ref[g, t] = jnp.zeros((SUBLANES, tq), jnp.int32)
            tied_ref[g] = jnp.full((SUBLANES, tq), -1, jnp.int32)
        return carry

    lax.fori_loop(nblk, n_quad * BLK_PER_STEP, blank_block, 0)

    def transpose_block(j, carry):
        for half in range(grp_per_blk):
            base = half * 32 * SUBLANES
            w = [key_ref[j, base + c * SUBLANES:base + (c + 1) * SUBLANES, :] ^ jnp.int32(INT_MIN) for c in range(32)]
            s, m = 16, 0x0000FFFF
            while s:
                k = 0
                while k < 32:
                    t = (w[k] ^ (w[k + s] >> s)) & jnp.int32(m)
                    w[k] = w[k] ^ t
                    w[k + s] = w[k + s] ^ (t << s)
                    k = (k + s + 1) & ~s
                s >>= 1
                m ^= (m << s) & 0xFFFFFFFF
            g = j * grp_per_blk + half
            for t in range(32):
                plane_ref[g, t] = w[t]
            tied_ref[g] = jnp.full((SUBLANES, tq), -1, jnp.int32)
        return carry

    lax.fori_loop(0, nblk, transpose_block, 0)

    def sweep(t, narrow_by, count_bit):
        def step(i, acc):
            hits = []
            for u in range(grp_per_step):
                g = i * grp_per_step + u
                tied = tied_ref[g]
                if narrow_by is not None:
                    hit = tied & plane_ref[g, t - 1]
                    tied = jnp.where(narrow_by != 0, hit, tied ^ hit)
                    tied_ref[g] = tied
                if count_bit:
                    hits.append(lax.population_count(tied & plane_ref[g, t]))
            return acc + _tree_sum(hits) if count_bit else acc
        return lax.fori_loop(0, n_quad, step, jnp.zeros((SUBLANES, tq), jnp.int32))

    def decide(t, ones, prefix, n_above):
        cnt = n_above + jnp.sum(ones.astype(F32), axis=0, keepdims=True)
        take = cnt >= topk_f
        bit = jnp.left_shift(jnp.int32(1), 31 - t)
        return jnp.where(take, prefix | bit, prefix), jnp.where(take, n_above, cnt), take.astype(jnp.int32)

    def bit_body(t, carry):
        prefix, n_above, take = carry
        return decide(t, sweep(t, take, True), prefix, n_above)

    first = decide(0, sweep(0, None, True), jnp.zeros((1, tq), jnp.int32), jnp.zeros((1, tq), F32))
    prefix, n_above, take = lax.fori_loop(1, 32, bit_body, first)
    sweep(32, take, False)
    thr = jnp.maximum(prefix ^ jnp.int32(INT_MIN), jnp.int32(INT_MIN + 1))

    def tally_tied(i, acc):
        return acc + _tree_sum([lax.population_count(tied_ref[i * grp_per_step + u]) for u in range(grp_per_step)])

    n_tied = lax.fori_loop(0, n_quad, tally_tied, jnp.zeros((SUBLANES, tq), jnp.int32))
    n_tied = jnp.sum(n_tied.astype(F32), axis=0, keepdims=True)
    n_ge = jnp.where(prefix == 0, 0.0, n_above + n_tied)
    need = topk_f - n_above

    @pl.when(jnp.max(n_ge) > topk_f)
    def _():
        def fix(j, seen):
            kb = key_ref[j]
            eq = jnp.where(kb == thr, 1.0, 0.0)
            rank = _dot(lower_ref[...], eq.astype(BF)) + seen
            key_ref[j] = jnp.where(eq * rank > need, jnp.int32(INT_MIN), kb)
            return seen + jnp.sum(eq, axis=0, keepdims=True)
        lax.fori_loop(0, nblk, fix, jnp.zeros((1, tq), F32))

    q_pairs = _stacked_pair_heads(q_ref)

    def body(t, carry, first, nsub):
        ms, ls, accs = carry
        blocks = [first + t * nsub + u for u in range(nsub)]
        starts = [pl.multiple_of(j * tk, tk) for j in blocks]
        raw = [_pair_logits(k_ref[0, pl.ds(ks, tk), :], q_pairs, tq) for ks in starts]
        selected = [key_ref[j] >= thr for j in blocks]
        ms_new, ls_new, alphas, ps = [], [], [], []
        for h in range(N_HEADS):
            logits = [jnp.where(selected[u], raw[u][h], NEG) for u in range(nsub)]
            m_new = functools.reduce(jnp.maximum, [ms[h]] + [jnp.max(s, axis=0, keepdims=True) for s in logits])
            alpha = jnp.exp2(ms[h] - m_new)
            p = [jnp.exp2(s - m_new) for s in logits]
            ms_new.append(m_new)
            ls_new.append(alpha * ls[h] + functools.reduce(jnp.add, [jnp.sum(pu, axis=0, keepdims=True) for pu in p]))
            alphas.append(alpha)
            ps.append([pu.astype(BF) for pu in p])
        accs_new = []
        for pr in range(N_HEADS // 2):
            pv = functools.reduce(jnp.add, [_dot(vt_ref[0, blocks[u], slabs[pr], :],
                                                 jnp.concatenate([ps[2 * pr][u], ps[2 * pr + 1][u]], axis=1))
                                            for u in range(nsub)])
            for hh in range(2):
                h = 2 * pr + hh
                accs_new.append(alphas[h] * accs[h] + pv[hh * HEAD_DIM:(hh + 1) * HEAD_DIM, hh * tq:(hh + 1) * tq])
        return tuple(ms_new), tuple(ls_new), tuple(accs_new)

    carry = (tuple(jnp.full((1, tq), NEG, F32) for _ in range(N_HEADS)),
             tuple(jnp.zeros((1, tq), F32) for _ in range(N_HEADS)),
             tuple(jnp.zeros((HEAD_DIM, tq), F32) for _ in range(N_HEADS)))
    carry = lax.fori_loop(0, nblk // 4, functools.partial(body, first=0, nsub=4), carry)
    carry = lax.fori_loop(0, (nblk % 4) // 2, functools.partial(body, first=nblk - nblk % 4, nsub=2), carry)
    _, ls, accs = lax.fori_loop(0, nblk % 2, functools.partial(body, first=nblk - 1, nsub=1), carry)
    out_t = jnp.concatenate([accs[h] / ls[h] for h in range(N_HEADS)], axis=0)
    o_ref[0] = out_t.T.astype(o_ref.dtype)


BLK_PER_STEP = 8


def _mixer_c(q_all, iq, iw, ik_arr, k_arr, v_arr, kv_col, tq, tk, q_start, n_keys):
    b, l, _ = q_all.shape
    l_pad = _round_up(l, tq)
    q_all, iq, iw = (_pad_rows(t, l_pad) for t in (q_all, iq, iw))
    kp = k_arr.shape[1]
    n_kblk = kp // tk
    topk = min(TOPK_MAX, n_keys // 4)
    lower = (np.arange(tk)[:, None] >= np.arange(tk)[None, :]).astype(np.float32)
    n_blk_alloc = _round_up(n_kblk, BLK_PER_STEP)
    n_grp = n_blk_alloc * (tk // (32 * SUBLANES))
    vt = v_arr[:, :, kv_col * BRANCH_W:(kv_col + 1) * BRANCH_W].reshape(b, n_kblk, tk, BRANCH_W).transpose(0, 1, 3, 2)
    return pl.pallas_call(
        functools.partial(_mixer_c_kernel, tq=tq, tk=tk, q_start=q_start, n_keys=n_keys, topk=topk),
        grid=(b, l_pad // tq),
        in_specs=[pl.BlockSpec((1, tq, BRANCH_W), lambda bi, i: (bi, i, 2)),
                  pl.BlockSpec((1, tq, H_IDX * D_IDX), lambda bi, i: (bi, i, 0)),
                  pl.BlockSpec((1, H_IDX, tq), lambda bi, i: (bi, 0, i)),
                  _resident_spec(kp, 2 * D_IDX, 0),
                  _kv_spec(kp, kv_col),
                  pl.BlockSpec((1, n_kblk, BRANCH_W, tk), lambda bi, i: (bi, 0, 0, 0), pipeline_mode=pl.Buffered(1)),
                  pl.BlockSpec((tk, tk), lambda bi, i: (0, 0))],
        out_specs=pl.BlockSpec((1, tq, BRANCH_W), lambda bi, i: (bi, i, 0)),
        out_shape=jax.ShapeDtypeStruct((b, l_pad, BRANCH_W), BF),
        scratch_shapes=[pltpu.VMEM((n_blk_alloc, tk, tq), jnp.int32),
                        pltpu.VMEM((n_grp, 32, SUBLANES, tq), jnp.int32), pltpu.VMEM((n_grp, SUBLANES, tq), jnp.int32)],
        compiler_params=_cparams("parallel", "parallel"), name="mixer_c",
    )(q_all, iq, jnp.moveaxis(iw, 1, 2), ik_arr, k_arr, vt, jnp.asarray(lower, BF))[:, :l]


def _mixer_d_kernel(q_ref, k_ref, v_ref, fq_ref, fk_ref, kbound_ref, fbound_ref, o_ref,
                    *, tq, tk, q_start, n_kblk, step):
    q0 = q_start + pl.program_id(1) * tq
    nblk = (q0 + tq - 1) // tk + 1
    n_full = (q0 + 1) // tk
    qpos = q0 + lax.broadcasted_iota(jnp.int32, (tq, tk), 0)
    kiota = lax.broadcasted_iota(jnp.int32, (tq, tk), 1)
    qm = _masked_pair_heads(q_ref)
    fq = [fq_ref[0, :, h:h + 1] for h in range(N_HEADS)]
    low = _low_half((tq, LANES))

    slabs = [slice(pr * LANES, (pr + 1) * LANES) for pr in range(N_HEADS // 2)]

    def body(t, carry, masked, first, nsub, sign):
        blocks = [first + sign * (t * nsub + u) for u in range(nsub)]
        starts = [pl.multiple_of(j * tk, tk) for j in blocks]
        raw = [[_dot_nt(qm[h], k_ref[0, pl.ds(ks, tk), slabs[h // 2]]) for h in range(N_HEADS)] for ks in starts]
        logits = []
        for u in range(nsub):
            row = [raw[u][h] + fq[h] - fk_ref[0, pl.ds(h * n_kblk + blocks[u], 1), :] for h in range(N_HEADS)]
            if masked:
                visible = (starts[u] + kiota) <= qpos
                row = [jnp.where(visible, s, NEG) for s in row]
            logits.append(row)
        return _softmax_step(carry, logits, lambda u, pr: v_ref[0, pl.ds(starts[u], tk), slabs[pr]], low)

    q_norm = [1.001 * jnp.sqrt(jnp.sum(jnp.square(qm[h].astype(F32)), axis=1, keepdims=True)) for h in range(N_HEADS)]
    bi = pl.program_id(0)

    def live(carry, j):
        gap = [q_norm[h] * kbound_ref[bi, h * n_kblk + j] + fq[h] + fbound_ref[bi, h * n_kblk + j] - carry[0][h]
               for h in range(N_HEADS)]
        return jnp.max(functools.reduce(jnp.maximum, gap)) > -(F32_UNDERFLOW_LOG2 + 8.0)

    carry = lax.fori_loop(0, nblk - n_full, functools.partial(body, masked=True, first=n_full, nsub=1, sign=1),
                          _softmax_init(tq))
    group = functools.partial(body, masked=False, first=n_full - 1, nsub=step, sign=-1)
    t_end, carry = lax.while_loop(
        lambda c: (c[0] < n_full // step) & live(c[1], jnp.maximum(n_full - 1 - c[0] * step, 0)),
        lambda c: (c[0] + 1, group(c[0], c[1])), (jnp.int32(0), carry))
    n_rest = n_full % step
    rest = jnp.where((t_end == n_full // step) & live(carry, jnp.maximum(n_rest - 1, 0)), n_rest, 0)
    carry = lax.fori_loop(0, rest, functools.partial(body, masked=False, first=n_rest - 1, nsub=1, sign=-1), carry)
    _softmax_finish(carry, o_ref, low)


def _mixer_d(q_all, k_arr, v_arr, kv_col, fq, fk, k_sq, tq, tk, q_start, step):
    b, l, _ = q_all.shape
    kp = k_arr.shape[1]
    n_kblk = kp // tk
    k_norm = 1.001 * jnp.sqrt(jnp.max(k_sq.reshape(b, n_kblk, tk, N_HEADS), axis=2))
    kbound = jnp.moveaxis(lax.cummax(k_norm, axis=1), 1, 2).reshape(b, N_HEADS * n_kblk)
    fbound = lax.cummax(jnp.max(-fk, axis=-1).reshape(b, N_HEADS, n_kblk), axis=2).reshape(b, N_HEADS * n_kblk)
    smem = pl.BlockSpec(memory_space=pltpu.SMEM)
    return pl.pallas_call(
        functools.partial(_mixer_d_kernel, tq=tq, tk=tk, q_start=q_start, n_kblk=n_kblk, step=step),
        grid=(b, l // tq),
        in_specs=[pl.BlockSpec((1, tq, BRANCH_W), lambda bi, i: (bi, i, 3)), _kv_spec(kp, kv_col),
                  _kv_spec(kp, kv_col),
                  pl.BlockSpec((1, tq, N_HEADS), lambda bi, i: (bi, i, 0)),
                  pl.BlockSpec((1, N_HEADS * n_kblk, tk), lambda bi, i: (bi, 0, 0)), smem, smem],
        out_specs=pl.BlockSpec((1, tq, BRANCH_W), lambda bi, i: (bi, i, 0)),
        out_shape=jax.ShapeDtypeStruct((b, l, BRANCH_W), BF),
        compiler_params=_cparams("parallel", "parallel"), name="mixer_d",
    )(q_all, k_arr, v_arr, fq, fk, kbound, fbound)


def _merge_kernel(x_ref, g1_ref, oa_ref, ob_ref, oc_ref, od_ref, wg_ref, wb_ref, wo_ref, y_ref):
    x = x_ref[...]
    h = _rms(x, g1_ref[...]).astype(BF)
    merged = None
    for g, o_ref in enumerate((oa_ref, ob_ref, oc_ref, od_ref)):
        gate = 1.0 / (1.0 + jnp.exp(-_dot(h, wg_ref[:, g * D_MODEL:(g + 1) * D_MODEL])))
        term = gate * _dot(o_ref[...], wb_ref[g])
        merged = term if merged is None else merged + term
    y_ref[...] = x + _dot(merged.astype(BF), wo_ref[...])


def _merge(x2, outs, lp, tm):
    m_rows = x2.shape[0]
    row = lambda i: (i, 0)
    const = lambda i: (0, 0)
    return pl.pallas_call(
        _merge_kernel, grid=(m_rows // tm,),
        in_specs=[pl.BlockSpec((tm, D_MODEL), row), pl.BlockSpec((1, D_MODEL), const)]
                 + [pl.BlockSpec((tm, BRANCH_W), row)] * N_MIXERS
                 + [pl.BlockSpec((D_MODEL, N_MIXERS * D_MODEL), const),
                    pl.BlockSpec((N_MIXERS, BRANCH_W, D_MODEL), lambda i: (0, 0, 0)),
                    pl.BlockSpec((D_MODEL, D_MODEL), const)],
        out_specs=pl.BlockSpec((tm, D_MODEL), row),
        out_shape=jax.ShapeDtypeStruct((m_rows, D_MODEL), F32),
        compiler_params=_cparams("parallel"), name="merge",
    )(x2, lp["g1"], *outs, lp["w_gate"], lp["w_branch"], lp["w_o"])


def _gelu_tanh(x):
    return x * (0.5 * (1.0 + jnp.tanh(np.sqrt(2.0 / np.pi).astype(np.float32) * (x + 0.044715 * (x * x * x)))))


def _ffn_kernel(x_ref, xp_ref, st_ref, g2_ref, win_ref, wc_ref, bc_ref, wd_ref, y_ref, ut_ref, h_ref, a_ref,
                *, tm, tf, tiles_per_seq):
    first_tile = pl.program_id(0) % tiles_per_seq == 0
    g2 = g2_ref[...]
    x = x_ref[...]
    h_ref[:CTX_ROWS, :] = _rms(xp_ref[...], g2).astype(BF)
    h_ref[CTX_ROWS:, :] = _rms(x, g2).astype(BF)
    h = h_ref[...]
    for f in range(0, D_FF, tf):
        u = _dot(h, win_ref[:, f:f + tf])
        gt = _dot(h, win_ref[:, D_FF + f:D_FF + f + tf])[CTX_ROWS:]
        ctx = jnp.where(first_tile, st_ref[0, :, f:f + tf], u[:CTX_ROWS])
        ue = jnp.concatenate([ctx, u[CTX_ROWS:]], axis=0)
        wc = wc_ref[:, f:f + tf]
        conv = bc_ref[:, f:f + tf] + ((ue[CTX_ROWS - 2:CTX_ROWS - 2 + tm] * wc[0:1]
                                       + ue[CTX_ROWS - 1:CTX_ROWS - 1 + tm] * wc[1:2]) + ue[CTX_ROWS:] * wc[2:3])
        a_ref[:, f:f + tf] = (_gelu_tanh(conv) * gt).astype(BF)
        ut_ref[0, :, f:f + tf] = ue[tm:]
    y_ref[...] = x + _dot(a_ref[...], wd_ref[...])


def _ffn(x2, state8, lp, tm, tf, seq_len):
    m_rows = x2.shape[0]
    tiles_per_seq = seq_len // tm
    const = lambda i: (0, 0)
    once = pl.Buffered(1)
    return pl.pallas_call(
        functools.partial(_ffn_kernel, tm=tm, tf=tf, tiles_per_seq=tiles_per_seq),
        grid=(m_rows // tm,),
        in_specs=[pl.BlockSpec((tm, D_MODEL), lambda i: (i, 0)),
                  pl.BlockSpec((CTX_ROWS, D_MODEL), lambda i: (jnp.maximum(i * (tm // CTX_ROWS) - 1, 0), 0)),
                  pl.BlockSpec((1, CTX_ROWS, D_FF), lambda i: (i // tiles_per_seq, 0, 0)),
                  pl.BlockSpec((1, D_MODEL), const),
                  pl.BlockSpec((D_MODEL, 2 * D_FF), const, pipeline_mode=once),
                  pl.BlockSpec((CONV_W, D_FF), const),
                  pl.BlockSpec((1, D_FF), const),
                  pl.BlockSpec((D_FF, D_MODEL), const, pipeline_mode=once)],
        out_specs=[pl.BlockSpec((tm, D_MODEL), lambda i: (i, 0)),
                   pl.BlockSpec((1, CTX_ROWS, D_FF), lambda i: (i, 0, 0))],
        out_shape=[jax.ShapeDtypeStruct((m_rows, D_MODEL), F32),
                   jax.ShapeDtypeStruct((m_rows // tm, CTX_ROWS, D_FF), F32)],
        scratch_shapes=[pltpu.VMEM((tm + CTX_ROWS, D_MODEL), BF), pltpu.VMEM((tm, D_FF), BF)],
        compiler_params=_cparams("parallel"), name="ffn",
    )(x2, x2, state8, lp["g2"], lp["w_ffn_in"], lp["w_conv"], lp["b_conv"], lp["w_down"])


def _rope_tables(q_start, length):
    half = HEAD_DIM // 2
    inv = ROPE_THETA ** (-jnp.arange(half, dtype=F32) / half)
    ang = (q_start + jnp.arange(length)).astype(F32)[:, None] * inv[None, :]
    cos = jnp.cos(ang)
    sin = jnp.sin(ang)
    cos_h = jnp.concatenate([cos, cos], axis=1)
    sin_h = jnp.concatenate([-sin, sin], axis=1)
    n_rot = H_IDX + 1
    pad = MISC_W - n_rot * HEAD_DIM
    return {
        "cq": jnp.tile(cos_h, (1, N_HEADS)), "sq": jnp.tile(sin_h, (1, N_HEADS)),
        "cm": jnp.concatenate([jnp.tile(cos_h, (1, n_rot)), jnp.ones((length, pad), F32)], axis=1),
        "sm": jnp.concatenate([jnp.tile(sin_h, (1, n_rot)), jnp.zeros((length, pad), F32)], axis=1),
    }


def _layer_params(norm1_g, w_in, b_forget, qk_g, rel_table, w_branch, w_o, norm2_g, w_ffn_in, w_conv, b_conv,
                  w_down):
    a = 3 * QKV_W
    i1 = a + H_IDX * D_IDX + D_IDX + H_IDX
    d1 = i1 + QKV_W
    f1 = d1 + N_HEADS
    n_misc = (i1 - a) + N_HEADS
    w_misc = jnp.concatenate([w_in[:, a:i1], w_in[:, d1:f1], jnp.zeros((D_MODEL, MISC_W - n_misc), w_in.dtype)],
                             axis=1)
    f_lane = D_IDX + H_IDX
    bf_row = jnp.zeros((1, LANES), F32).at[0, f_lane:f_lane + N_HEADS].set(b_forget.astype(F32))
    head_ones = (np.arange(BRANCH_W)[:, None] // HEAD_DIM == np.arange(BRANCH_W)[None, :] // HEAD_DIM)
    return {
        "g1": norm1_g.astype(F32)[None, :], "g2": norm2_g.astype(F32)[None, :],
        "w_qkv": jnp.concatenate([w_in[:, :a], w_in[:, i1:d1]], axis=1).astype(BF),
        "w_misc": w_misc.astype(BF), "w_gate": w_in[:, f1:].astype(BF),
        "qkg": jnp.tile(qk_g.astype(F32), (1, N_HEADS)), "bf_row": bf_row,
        "head_ones": jnp.asarray(head_ones.astype(np.float32), BF),
        "rel_table": rel_table, "w_branch": w_branch.astype(BF), "w_o": w_o.astype(BF),
        "w_ffn_in": w_ffn_in.astype(BF), "w_conv": w_conv.astype(F32), "b_conv": b_conv.astype(F32)[None, :],
        "w_down": w_down.astype(BF),
    }


def _pad_rows(a, rows):
    return jnp.pad(a, ((0, 0), (0, rows - a.shape[1])) + ((0, 0),) * (a.ndim - 2))


def _round_up(n, m):
    return -(-n // m) * m


def _layer(x, past, lp, tabs):
    b, l, _ = x.shape
    m_rows = b * l
    q_start = 0 if past is None else past[2].shape[1]
    n_keys = q_start + l
    tq = min(256, l)
    tq_c = LANES
    tk = 256
    tk_d = 512
    kp = _round_up(n_keys, max(tk, tk_d))
    tm = min(256, m_rows)

    x2 = x.reshape(m_rows, D_MODEL)
    (q_all, kbf, vbf, ka, va, kb, vb, kc, vc, kd, vd, iq, misc) = _projection(x2, lp, tabs, tm)
    ik = misc[:, :D_IDX].reshape(b, l, D_IDX)
    iw = misc[:, D_IDX:D_IDX + H_IDX].reshape(b, l, H_IDX)
    logf = misc[:, D_IDX + H_IDX:D_IDX + H_IDX + N_HEADS].reshape(b, l, N_HEADS)
    q_all = q_all.reshape(b, l, N_MIXERS * BRANCH_W)
    iq = iq.reshape(b, l, H_IDX * D_IDX)
    heads = lambda t: t.reshape(b, l, N_HEADS, HEAD_DIM)
    flat = lambda t: t.reshape(t.shape[0], t.shape[1], BRANCH_W)

    if past is None:
        kbf3 = kbf.reshape(b, l, N_MIXERS * BRANCH_W)
        vbf3 = vbf.reshape(b, l, N_MIXERS * BRANCH_W)
        kv = [(kbf3, vbf3, c) for c in range(N_MIXERS)]
        ik_all = ik.astype(BF)
        logf_all = logf
        a_args = (kbf3, vbf3, 0, _band_bias(lp["rel_table"], tq, BAND_PAST + tq, BAND_PAST + tq), tq, 3, tq)
        conv_state = jnp.zeros((b, CONV_W - 1, D_FF), F32)
    else:
        (pa_k, pa_v, pb_k, pb_v, pc_k, pc_v, pc_ki, pd_k, pd_v, pd_f, conv_state) = past
        new_bf = lambda t, c: t.reshape(b, l, N_MIXERS, BRANCH_W)[:, :, c]
        cat = lambda p, c, src, rows: _pad_rows(jnp.concatenate([flat(p).astype(BF), new_bf(src, c)], axis=1), rows)
        kv = [None] + [(cat(pk, c, kbf, kp), cat(pv, c, vbf, kp), 0)
                       for c, (pk, pv) in ((1, (pb_k, pb_v)), (2, (pc_k, pc_v)), (3, (pd_k, pd_v)))]
        ik_all = jnp.concatenate([pc_ki.astype(BF), ik.astype(BF)], axis=1)
        logf_all = jnp.concatenate([pd_f.astype(F32), logf], axis=1)
        a_keys = pa_k.shape[1] + l
        wa = _round_up(BAND_PAST + l, LANES)
        a_args = (cat(pa_k, 0, kbf, wa), cat(pa_v, 0, vbf, wa), 0,
                  _band_bias(lp["rel_table"], l, wa, a_keys), l, 1, wa)

    f_cum = jnp.cumsum(logf_all, axis=1) * LOG2E
    fq = f_cum[:, q_start:]
    fk = jnp.moveaxis(_pad_rows(f_cum, kp), 1, 2).reshape(b, N_HEADS * (kp // tk_d), tk_d)

    o_a = _mixer_a(q_all, *a_args)
    o_b = _mixer_b(q_all, *kv[1], tq, tk, q_start)
    ik_pair = _pad_rows(jnp.concatenate([ik_all, ik_all], axis=-1), kp)
    o_c = _mixer_c(q_all, iq, iw, ik_pair, *kv[2], tq_c, tk_d, q_start, n_keys)
    head_of_col = jnp.asarray(np.arange(BRANCH_W)[:, None] // HEAD_DIM == np.arange(N_HEADS)[None, :], F32)
    kd_sq = jnp.dot(jnp.square(kd.astype(BF).astype(F32)), head_of_col,
                    precision=lax.Precision.HIGHEST).reshape(b, l, N_HEADS)
    if past is not None:
        kd_sq = jnp.concatenate([jnp.sum(jnp.square(past[7].astype(BF).astype(F32)), axis=-1), kd_sq], axis=1)
    o_d = _mixer_d(q_all, *kv[3], fq, fk, _pad_rows(kd_sq, kp), tq, tk_d, q_start, 2)
    x2 = _merge(x2, [o.reshape(m_rows, BRANCH_W) for o in (o_a, o_b, o_c, o_d)], lp, tm)

    state8 = jnp.pad(conv_state.astype(F32), ((0, 0), (CTX_ROWS - (CONV_W - 1), 0), (0, 0)))
    x2, u_tail = _ffn(x2, state8, lp, min(512, l), 256, l)
    u_tail = u_tail.reshape(b, -1, CTX_ROWS, D_FF)[:, -1]

    rows3 = lambda t: t.reshape(b, l, BRANCH_W)
    if past is None:
        a_keep = min(BAND_PAST, l)
        new_a = tuple(rows3(t)[:, -a_keep:].reshape(b, a_keep, N_HEADS, HEAD_DIM) for t in (ka, va))
    else:
        a_keep = pa_k.shape[1]
        new_a = (jnp.concatenate([pa_k, heads(ka)], axis=1)[:, -a_keep:],
                 jnp.concatenate([pa_v, heads(va)], axis=1)[:, -a_keep:])
    new = new_a + (rows3(kb), rows3(vb), rows3(kc), rows3(vc), ik, rows3(kd), rows3(vd), logf,
                   u_tail[:, -(CONV_W - 1):])
    return x2.reshape(b, l, D_MODEL), new


def kernel(x_prompt, x_sample, cache_a_k, cache_a_v, cache_b_k, cache_b_v, cache_c_k, cache_c_v, cache_c_kidx, cache_d_k, cache_d_v, cache_d_logf, state_ffn_conv, norm1_g, w_in, b_forget, qk_norm_g, rel_bias, w_branch, w_o, norm2_g, w_ffn_in, w_conv, b_conv, w_down):
    depth = w_in.shape[0]
    past_len = cache_b_k.shape[2]
    tabs_p = _rope_tables(0, x_prompt.shape[1])
    tabs_s = {k: jnp.tile(v, (x_sample.shape[0], 1)) for k, v in _rope_tables(past_len, x_sample.shape[1]).items()}
    y_p, y_s = x_prompt, x_sample
    p_states, s_states = [], []
    for d in range(depth):
        lp = _layer_params(norm1_g[d], w_in[d], b_forget[d], qk_norm_g[d], rel_bias[d], w_branch[d], w_o[d],
                           norm2_g[d], w_ffn_in[d], w_conv[d], b_conv[d], w_down[d])
        y_p, st_p = _layer(y_p, None, lp, tabs_p)
        past = (cache_a_k[d], cache_a_v[d], cache_b_k[d], cache_b_v[d], cache_c_k[d], cache_c_v[d],
                cache_c_kidx[d], cache_d_k[d], cache_d_v[d], cache_d_logf[d], state_ffn_conv[d])
        y_s, st_s = _layer(y_s, past, lp, tabs_s)
        p_states.append(st_p)
        s_states.append(st_s)
    def assemble(states):
        out = [jnp.stack(t) for t in zip(*states)]
        for i in (2, 3, 4, 5, 7, 8):
            out[i] = out[i].reshape(out[i].shape[:-1] + (N_HEADS, HEAD_DIM))
        return out

    return (y_p, y_s, *assemble(p_states), *assemble(s_states))
```

```python
import functools

import numpy as np
import jax
import jax.numpy as jnp
from jax import lax
from jax.experimental import pallas as pl
from jax.experimental.pallas import tpu as pltpu

D_MODEL = 1024
CHUNK = 64
N_MIXERS = 4
N_HEADS = 4
HEAD_DIM = 64
BRANCH_W = N_HEADS * HEAD_DIM
BAND_PAST = 8 * CHUNK
REL_CLIP = 128
H_IDX = 4
D_IDX = 64
TOPK_MAX = 256
ROPE_THETA = 10000.0
D_FF = 2816
CONV_W = 3
EPS = 1e-6

BF = jnp.bfloat16
F32 = jnp.float32
NEG = -1e30
LOG2E = float(np.log2(np.e))
F32_UNDERFLOW_LOG2 = 160.0
INT_MIN = -2 ** 31
LANES = 128
SUBLANES = 8
CTX_ROWS = 2 * SUBLANES
QKV_W = 3 * BRANCH_W
MISC_W = 3 * LANES
VMEM_LIMIT = 56 * 1024 * 1024


def _cparams(*sem):
    return pltpu.CompilerParams(dimension_semantics=sem, vmem_limit_bytes=VMEM_LIMIT)


def _dot(a, b):
    return jnp.dot(a, b, preferred_element_type=F32)


def _dot_nt(a, b):
    return lax.dot_general(a, b, (((1,), (1,)), ((), ())), preferred_element_type=F32)


def _split_dot(x, m, terms):
    acc = None
    r = x
    for t in range(terms):
        hi = r.astype(BF)
        d = _dot(hi, m)
        acc = d if acc is None else acc + d
        if t + 1 < terms:
            r = r - hi.astype(F32)
    return acc


def _rms(x, g):
    ms = jnp.mean(x * x, axis=-1, keepdims=True)
    return x * lax.rsqrt(ms + EPS) * g


def _head_rms(t, g, head_ones):
    ms = _split_dot(t * t, head_ones, 3) * (1.0 / HEAD_DIM)
    return t * lax.rsqrt(ms + EPS) * g


def _swap_halves(y):
    n = y.shape[-1]
    lane = lax.broadcasted_iota(jnp.int32, y.shape, 1)
    first = (lane & (HEAD_DIM - 1)) < (HEAD_DIM // 2)
    return jnp.where(first, pltpu.roll(y, n - HEAD_DIM // 2, 1), pltpu.roll(y, HEAD_DIM // 2, 1))


def _softplus2(z):
    neg_abs = pltpu.bitcast(pltpu.bitcast(z, jnp.int32) | jnp.int32(INT_MIN), F32)
    return jnp.maximum(z, 0.0) + jnp.log2(1.0 + jnp.exp2(neg_abs))


def _tree_sum(xs):
    while len(xs) > 1:
        xs = [xs[i] + xs[i + 1] for i in range(0, len(xs) - 1, 2)] + ([xs[-1]] if len(xs) % 2 else [])
    return xs[0]


def _low_half(shape):
    return lax.broadcasted_iota(jnp.int32, shape, 1) < HEAD_DIM


def _masked_pair_heads(ref):
    out = []
    for pr in range(ref.shape[-1] // LANES):
        slab = ref[0, :, pr * LANES:(pr + 1) * LANES].astype(F32)
        low = _low_half(slab.shape)
        out.append(jnp.where(low, slab, 0.0).astype(BF))
        out.append(jnp.where(low, 0.0, slab).astype(BF))
    return out


def _stacked_pair_heads(ref):
    masked = _masked_pair_heads(ref)
    return [jnp.concatenate(masked[2 * pr:2 * pr + 2], axis=0) for pr in range(len(masked) // 2)]


def _pair_logits(keys, stacked, tq):
    out = []
    for pr, rhs in enumerate(stacked):
        slab = keys if keys.shape[-1] == LANES else keys[:, pr * LANES:(pr + 1) * LANES]
        both = _dot_nt(slab, rhs)
        out += [both[:, :tq], both[:, tq:]]
    return out


def _softmax_init(tq):
    return (tuple(jnp.full((tq, 1), NEG, F32) for _ in range(N_HEADS)),
            tuple(jnp.zeros((tq, 1), F32) for _ in range(N_HEADS)),
            tuple(jnp.zeros((tq, LANES), F32) for _ in range(N_HEADS // 2)))


def _softmax_step(carry, logits, v_tile, low):
    ms, ls, accs = carry
    tiles = range(len(logits))
    ms_new, ls_new, alphas, ps = [], [], [], []
    for h in range(N_HEADS):
        m_new = functools.reduce(jnp.maximum, [ms[h]] + [jnp.max(logits[u][h], axis=1, keepdims=True) for u in tiles])
        alpha = jnp.exp2(ms[h] - m_new)
        p = [jnp.exp2(logits[u][h] - m_new) for u in tiles]
        ms_new.append(m_new)
        ls_new.append(alpha * ls[h] + functools.reduce(jnp.add, [jnp.sum(pu, axis=1, keepdims=True) for pu in p]))
        alphas.append(alpha)
        ps.append([pu.astype(BF) for pu in p])
    pvs = [functools.reduce(jnp.add, [_dot(ps[h][u], v_tile(u, h // 2)) for u in tiles]) for h in range(N_HEADS)]
    accs_new = [jnp.where(low, alphas[2 * pr], alphas[2 * pr + 1]) * accs[pr]
                + jnp.where(low, pvs[2 * pr], pvs[2 * pr + 1]) for pr in range(N_HEADS // 2)]
    return tuple(ms_new), tuple(ls_new), tuple(accs_new)


def _softmax_finish(carry, o_ref, low):
    _, ls, accs = carry
    for pr in range(N_HEADS // 2):
        l_pair = jnp.where(low, ls[2 * pr], ls[2 * pr + 1])
        o_ref[0, :, pr * LANES:(pr + 1) * LANES] = (accs[pr] / l_pair).astype(o_ref.dtype)


def _proj_kernel(x_ref, g1_ref, wqkv_ref, wmisc_ref, qkg_ref, hones_ref, cq_ref, sq_ref, cm_ref, sm_ref,
                 bf_ref, q_ref, kbf_ref, vbf_ref, ka_ref, va_ref, kb_ref, vb_ref, kc_ref, vc_ref,
                 kd_ref, vd_ref, iq_ref, misc_ref):
    h = _rms(x_ref[...], g1_ref[...]).astype(BF)
    hones = hones_ref[...]
    k32 = (ka_ref, kb_ref, kc_ref, kd_ref)
    v32 = (va_ref, vb_ref, vc_ref, vd_ref)
    norm_row = (0, None, 2, 4)
    for m in range(N_MIXERS):
        y = _dot(h, wqkv_ref[:, m * QKV_W:(m + 1) * QKV_W])
        q, k, v = y[:, :BRANCH_W], y[:, BRANCH_W:2 * BRANCH_W], y[:, 2 * BRANCH_W:]
        if norm_row[m] is not None:
            r = norm_row[m]
            q = _head_rms(q, qkg_ref[r:r + 1, :], hones)
            k = _head_rms(k, qkg_ref[r + 1:r + 2, :], hones)
        if m == 2:
            c, s = cq_ref[...], sq_ref[...]
            q = q * c + _swap_halves(q) * s
            k = k * c + _swap_halves(k) * s
        cols = slice(m * BRANCH_W, (m + 1) * BRANCH_W)
        q_ref[:, cols] = (q * (HEAD_DIM ** -0.5 * LOG2E)).astype(BF)
        kbf_ref[:, cols] = k.astype(BF)
        vbf_ref[:, cols] = v.astype(BF)
        k32[m][...] = k
        v32[m][...] = v
    ym = _dot(h, wmisc_ref[...])
    r = ym * cm_ref[...] + _swap_halves(ym) * sm_ref[...]
    iq_ref[...] = r[:, :H_IDX * D_IDX].astype(BF)
    g2 = r[:, H_IDX * D_IDX:]
    lane = lax.broadcasted_iota(jnp.int32, g2.shape, 1)
    z = g2 + bf_ref[...]
    logf = jnp.minimum(z, 0.0) - jnp.log1p(jnp.exp(-jnp.abs(z)))
    is_w = (lane >= D_IDX) & (lane < D_IDX + H_IDX)
    is_f = (lane >= D_IDX + H_IDX) & (lane < D_IDX + H_IDX + N_HEADS)
    misc_ref[...] = jnp.where(is_f, logf, jnp.where(is_w, g2 * (H_IDX ** -0.5 * D_IDX ** -0.5), g2))


def _projection(x2, lp, tabs, tm):
    m_rows = x2.shape[0]
    n_tab = tabs["cq"].shape[0] // tm
    row = lambda i: (i, 0)
    const = lambda i: (0, 0)
    tab = lambda i: (i % n_tab, 0)
    wide = pl.BlockSpec((tm, N_MIXERS * BRANCH_W), row)
    head = pl.BlockSpec((tm, BRANCH_W), row)
    in_specs = [
        pl.BlockSpec((tm, D_MODEL), row),
        pl.BlockSpec((1, D_MODEL), const),
        pl.BlockSpec((D_MODEL, N_MIXERS * QKV_W), const),
        pl.BlockSpec((D_MODEL, MISC_W), const),
        pl.BlockSpec((6, BRANCH_W), const),
        pl.BlockSpec((BRANCH_W, BRANCH_W), const),
        pl.BlockSpec((tm, BRANCH_W), tab),
        pl.BlockSpec((tm, BRANCH_W), tab),
        pl.BlockSpec((tm, MISC_W), tab),
        pl.BlockSpec((tm, MISC_W), tab),
        pl.BlockSpec((1, LANES), const),
    ]
    out_shape = ([jax.ShapeDtypeStruct((m_rows, N_MIXERS * BRANCH_W), BF)] * 3
                 + [jax.ShapeDtypeStruct((m_rows, BRANCH_W), F32)] * 8
                 + [jax.ShapeDtypeStruct((m_rows, BRANCH_W), BF),
                    jax.ShapeDtypeStruct((m_rows, LANES), F32)])
    out_specs = [wide] * 3 + [head] * 8 + [head, pl.BlockSpec((tm, LANES), row)]
    return pl.pallas_call(
        _proj_kernel, grid=(m_rows // tm,), in_specs=in_specs, out_specs=out_specs, out_shape=out_shape,
        compiler_params=_cparams("parallel"), name="projection",
    )(x2, lp["g1"], lp["w_qkv"], lp["w_misc"], lp["qkg"], lp["head_ones"],
      tabs["cq"], tabs["sq"], tabs["cm"], tabs["sm"], lp["bf_row"])


def _mixer_a_kernel(*refs, nwb, wb):
    q_ref = refs[0]
    k_refs = refs[1:1 + nwb]
    v_refs = refs[1 + nwb:1 + 2 * nwb]
    bias_ref = refs[1 + 2 * nwb]
    o_ref = refs[2 + 2 * nwb]
    i = pl.program_id(1)
    qm = _masked_pair_heads(q_ref)
    low = _low_half((q_ref.shape[1], LANES))
    for pr in range(N_HEADS // 2):
        slab = slice(pr * LANES, (pr + 1) * LANES)
        outs = []
        for h in (2 * pr, 2 * pr + 1):
            logits = []
            for r in range(nwb):
                s = _dot_nt(qm[h], k_refs[r][0, :, slab]) + bias_ref[h, :, r * wb:(r + 1) * wb]
                logits.append(jnp.where(i - (nwb - 1) + r >= 0, s, NEG))
            m = functools.reduce(jnp.maximum, [jnp.max(s, axis=1, keepdims=True) for s in logits])
            ps = [jnp.exp2(s - m) for s in logits]
            l = functools.reduce(jnp.add, [jnp.sum(p, axis=1, keepdims=True) for p in ps])
            acc = functools.reduce(jnp.add, [_dot(p.astype(BF), v_refs[r][0, :, slab]) for r, p in enumerate(ps)])
            outs.append(acc / l)
        o_ref[0, :, slab] = jnp.where(low, outs[0], outs[1]).astype(o_ref.dtype)


def _mixer_a(q_all, k_arr, v_arr, kv_col, bias, tq, nwb, wb):
    b, l, _ = q_all.shape
    kspec = lambda r: pl.BlockSpec((1, wb, BRANCH_W),
                                   lambda bi, i, r=r: (bi, jnp.maximum(i - (nwb - 1) + r, 0), kv_col))
    in_specs = ([pl.BlockSpec((1, tq, BRANCH_W), lambda bi, i: (bi, i, 0))]
                + [kspec(r) for r in range(nwb)] * 2
                + [pl.BlockSpec(bias.shape, lambda bi, i: (0, 0, 0))])
    return pl.pallas_call(
        functools.partial(_mixer_a_kernel, nwb=nwb, wb=wb),
        grid=(b, l // tq), in_specs=in_specs,
        out_specs=pl.BlockSpec((1, tq, BRANCH_W), lambda bi, i: (bi, i, 0)),
        out_shape=jax.ShapeDtypeStruct((b, l, BRANCH_W), BF),
        compiler_params=_cparams("parallel", "parallel"), name="mixer_a",
    )(q_all, *([k_arr] * nwb), *([v_arr] * nwb), bias)


def _band_bias(rel_table, tq, w, n_valid_cols):
    t = np.arange(tq)[:, None]
    c = np.arange(w)[None, :]
    krel = c - BAND_PAST
    ct = t // CHUNK
    inband = (krel >= CHUNK * ct - BAND_PAST) & (krel < CHUNK * ct + CHUNK) & (c < n_valid_cols)
    n = tq + w - 1
    idx = np.clip(np.arange(n) - (w - 1 - BAND_PAST), -REL_CLIP, REL_CLIP) + REL_CLIP
    r0, r1 = int((idx == idx[0]).sum()), int((idx == idx[-1]).sum())
    tab = rel_table.astype(F32) * LOG2E
    f = jnp.concatenate([jnp.broadcast_to(tab[idx[0]], (r0, N_HEADS)), tab[idx[r0]:idx[n - r1 - 1] + 1],
                         jnp.broadcast_to(tab[idx[-1]], (r1, N_HEADS))], axis=0).T
    hankel = jnp.tile(f, (1, tq + 1))[:, :tq * (n + 1)].reshape(N_HEADS, tq, n + 1)[:, :, :w]
    return jnp.where(inband[None], hankel[:, :, ::-1], NEG)


def _mixer_b_kernel(q_ref, k_ref, v_ref, upper_ref, o_ref, *, tq, tk, q_start):
    q0 = q_start + pl.program_id(1) * tq
    nblk = (q0 + tq - 2) // tk + 1
    upper = upper_ref[...]
    qpos = q0 + lax.broadcasted_iota(jnp.int32, (tq, tk), 0)
    kiota = lax.broadcasted_iota(jnp.int32, (tq, tk), 1)
    n_full = q0 // tk
    qm = _masked_pair_heads(q_ref)
    low = _low_half((tq, LANES))

    slabs = [slice(pr * LANES, (pr + 1) * LANES) for pr in range(N_HEADS // 2)]

    def body(t, carry, masked, first, nsub):
        accs, laters = carry
        starts = [pl.multiple_of((first - t * nsub - u) * tk, tk) for u in range(nsub)]
        masks = [(ks + kiota) < qpos for ks in starts]
        zs = [[_dot_nt(qm[h], k_ref[0, pl.ds(ks, tk), slabs[h // 2]]) for h in range(N_HEADS)] for ks in starts]
        log_betas, leaves = [], []
        for u in range(nsub):
            sps = [_softplus2(z) for z in zs[u]]
            leaves.append([jnp.where(masks[u], sp, 0.0) for sp in sps] if masked else sps)
            log_betas.append([z - sp for z, sp in zip(zs[u], sps)])
        tails = [[_dot(s.astype(BF), upper) for s in leaves[u]] for u in range(nsub)]
        laters = list(laters)
        ws = []
        for u in range(nsub):
            ws_u = []
            for h in range(N_HEADS):
                w = jnp.exp2(log_betas[u][h] - (tails[u][h] + laters[h]))
                ws_u.append((jnp.where(masks[u], w, 0.0) if masked else w).astype(BF))
                laters[h] = laters[h] + jnp.sum(leaves[u][h], axis=1, keepdims=True)
            ws.append(ws_u)
        wvs = [functools.reduce(jnp.add, [_dot(ws[u][h], v_ref[0, pl.ds(starts[u], tk), slabs[h // 2]])
                                          for u in range(nsub)]) for h in range(N_HEADS)]
        accs_new = [accs[pr] + jnp.where(low, wvs[2 * pr], wvs[2 * pr + 1]) for pr in range(N_HEADS // 2)]
        return tuple(accs_new), tuple(laters)

    carry = (tuple(jnp.zeros((tq, LANES), F32) for _ in range(N_HEADS // 2)),
             tuple(jnp.zeros((tq, 1), F32) for _ in range(N_HEADS)))
    carry = lax.fori_loop(0, nblk - n_full, functools.partial(body, masked=True, first=nblk - 1, nsub=1), carry)

    def live(carry):
        return jnp.min(functools.reduce(jnp.minimum, carry[1])) < F32_UNDERFLOW_LOG2

    pair = functools.partial(body, masked=False, first=n_full - 1, nsub=2)
    t_end, carry = lax.while_loop(lambda c: (c[0] < n_full // 2) & live(c[1]),
                                  lambda c: (c[0] + 1, pair(c[0], c[1])), (jnp.int32(0), carry))
    last = jnp.where((t_end == n_full // 2) & live(carry), n_full % 2, 0)
    accs, _ = lax.fori_loop(0, last, functools.partial(body, masked=False, first=0, nsub=1), carry)
    for pr in range(N_HEADS // 2):
        o_ref[0, :, pr * LANES:(pr + 1) * LANES] = accs[pr].astype(o_ref.dtype)


def _resident_spec(kp, width, col):
    return pl.BlockSpec((1, kp, width), lambda bi, i: (bi, 0, col), pipeline_mode=pl.Buffered(1))


def _kv_spec(kp, col):
    return _resident_spec(kp, BRANCH_W, col)


def _mixer_b(q_all, k_arr, v_arr, kv_col, tq, tk, q_start):
    b, l, _ = q_all.shape
    kp = k_arr.shape[1]
    upper = (np.arange(tk)[:, None] > np.arange(tk)[None, :]).astype(np.float32)
    return pl.pallas_call(
        functools.partial(_mixer_b_kernel, tq=tq, tk=tk, q_start=q_start),
        grid=(b, l // tq),
        in_specs=[pl.BlockSpec((1, tq, BRANCH_W), lambda bi, i: (bi, i, 1)), _kv_spec(kp, kv_col),
                  _kv_spec(kp, kv_col), pl.BlockSpec((tk, tk), lambda bi, i: (0, 0))],
        out_specs=pl.BlockSpec((1, tq, BRANCH_W), lambda bi, i: (bi, i, 0)),
        out_shape=jax.ShapeDtypeStruct((b, l, BRANCH_W), BF),
        compiler_params=_cparams("parallel", "parallel"), name="mixer_b",
    )(q_all, k_arr, v_arr, jnp.asarray(upper, BF))


def _mixer_c_kernel(q_ref, iq_ref, iw_ref, ik_ref, k_ref, vt_ref, lower_ref, o_ref, key_ref, plane_ref, tied_ref,
                    *, tq, tk, q_start, n_keys, topk):
    q0 = q_start + pl.program_id(1) * tq
    last_adm = jnp.minimum(((q0 + tq - 1) // CHUNK) * CHUNK + CHUNK - 1, n_keys - 1)
    nblk = last_adm // tk + 1
    n_full = jnp.minimum((q0 // CHUNK + 1) * CHUNK, n_keys) // tk
    qchunk = (q0 + lax.broadcasted_iota(jnp.int32, (tk, tq), 1)) // CHUNK
    kiota = lax.broadcasted_iota(jnp.int32, (tk, tq), 0)
    topk_f = jnp.float32(topk)
    iq_pairs = _stacked_pair_heads(iq_ref)
    iw_rows = [iw_ref[0, g:g + 1, :] for g in range(H_IDX)]
    slabs = [slice(pr * LANES, (pr + 1) * LANES) for pr in range(N_HEADS // 2)]

    def score_body(t, carry, masked, first, nsub):
        blocks = [first + t * nsub + u for u in range(nsub)]
        starts = [pl.multiple_of(j * tk, tk) for j in blocks]
        dots = [_pair_logits(ik_ref[0, pl.ds(ks, tk), :], iq_pairs, tq) for ks in starts]
        for u in range(nsub):
            sc = None
            for g in range(H_IDX):
                term = iw_rows[g] * jnp.maximum(dots[u][g], 0.0)
                sc = term if sc is None else sc + term
            sc = jnp.where(sc == 0.0, 0.0, sc)
            bits = pltpu.bitcast(sc, jnp.int32)
            key = bits ^ ((bits >> 31) & jnp.int32(0x7FFFFFFF))
            if masked:
                kpos = starts[u] + kiota
                adm = ((kpos // CHUNK) <= qchunk) & (kpos < n_keys)
                key = jnp.where(adm, key, jnp.int32(INT_MIN))
            key_ref[blocks[u]] = key
        return carry

    lax.fori_loop(0, n_full // 4, functools.partial(score_body, masked=False, first=0, nsub=4), 0)
    lax.fori_loop(0, (n_full % 4) // 2, functools.partial(score_body, masked=False, first=n_full - n_full % 4,
                                                          nsub=2), 0)
    lax.fori_loop(0, n_full % 2, functools.partial(score_body, masked=False, first=n_full - 1, nsub=1), 0)
    lax.fori_loop(0, nblk - n_full, functools.partial(score_body, masked=True, first=n_full, nsub=1), 0)

    grp_per_blk = tk // (32 * SUBLANES)
    n_quad = (nblk + BLK_PER_STEP - 1) // BLK_PER_STEP
    grp_per_step = BLK_PER_STEP * grp_per_blk

    def blank_block(j, carry):
        for half in range(grp_per_blk):
            g = j * grp_per_blk + half
            for t in range(32):
                plane_ref[t, g] = jnp.zeros((SUBLANES, tq), jnp.int32)
            tied_ref[g] = jnp.full((SUBLANES, tq), -1, jnp.int32)
        return carry

    lax.fori_loop(nblk, n_quad * BLK_PER_STEP, blank_block, 0)

    def transpose_block(j, carry):
        for half in range(grp_per_blk):
            base = half * 32 * SUBLANES
            w = [key_ref[j, base + c * SUBLANES:base + (c + 1) * SUBLANES, :] ^ jnp.int32(INT_MIN) for c in range(32)]
            s, m = 16, 0x0000FFFF
            while s:
                k = 0
                while k < 32:
                    t = (w[k] ^ (w[k + s] >> s)) & jnp.int32(m)
                    w[k] = w[k] ^ t
                    w[k + s] = w[k + s] ^ (t << s)
                    k = (k + s + 1) & ~s
                s >>= 1
                m ^= (m << s) & 0xFFFFFFFF
            g = j * grp_per_blk + half
            for t in range(32):
                plane_ref[t, g] = w[t]
            tied_ref[g] = jnp.full((SUBLANES, tq), -1, jnp.int32)
        return carry

    lax.fori_loop(0, nblk, transpose_block, 0)

    def sweep(t, narrow_by, count_bit):
        def step(i, acc):
            grps = pl.ds(i * grp_per_step, grp_per_step)
            tied = tied_ref[grps]
            if narrow_by is not None:
                hit = tied & plane_ref[t - 1, grps]
                tied = jnp.where(narrow_by != 0, hit, tied ^ hit)
                tied_ref[grps] = tied
            if not count_bit:
                return acc
            hits = lax.population_count(tied & plane_ref[t, grps])
            return acc + _tree_sum([hits[u] for u in range(grp_per_step)])
        return lax.fori_loop(0, n_quad, step, jnp.zeros((SUBLANES, tq), jnp.int32))

    def decide(t, ones, prefix, n_above):
        cnt = n_above + jnp.sum(ones.astype(F32), axis=0, keepdims=True)
        take = cnt >= topk_f
        bit = jnp.left_shift(jnp.int32(1), 31 - t)
        return jnp.where(take, prefix | bit, prefix), jnp.where(take, n_above, cnt), take.astype(jnp.int32)

    def bit_body(t, carry):
        prefix, n_above, take = carry
        return decide(t, sweep(t, take, True), prefix, n_above)

    first = decide(0, sweep(0, None, True), jnp.zeros((1, tq), jnp.int32), jnp.zeros((1, tq), F32))
    prefix, n_above, take = lax.fori_loop(1, 32, bit_body, first)
    sweep(32, take, False)
    thr = jnp.maximum(prefix ^ jnp.int32(INT_MIN), jnp.int32(INT_MIN + 1))

    def tally_tied(i, acc):
        ones = lax.population_count(tied_ref[pl.ds(i * grp_per_step, grp_per_step)])
        return acc + _tree_sum([ones[u] for u in range(grp_per_step)])

    n_tied = lax.fori_loop(0, n_quad, tally_tied, jnp.zeros((SUBLANES, tq), jnp.int32))
    n_tied = jnp.sum(n_tied.astype(F32), axis=0, keepdims=True)
    n_ge = jnp.where(prefix == 0, 0.0, n_above + n_tied)
    need = topk_f - n_above

    @pl.when(jnp.max(n_ge) > topk_f)
    def _():
        def fix(j, seen):
            kb = key_ref[j]
            eq = jnp.where(kb == thr, 1.0, 0.0)
            rank = _dot(lower_ref[...], eq.astype(BF)) + seen
            key_ref[j] = jnp.where(eq * rank > need, jnp.int32(INT_MIN), kb)
            return seen + jnp.sum(eq, axis=0, keepdims=True)
        lax.fori_loop(0, nblk, fix, jnp.zeros((1, tq), F32))

    q_pairs = _stacked_pair_heads(q_ref)

    def body(t, carry, first, nsub):
        ms, ls, accs = carry
        blocks = [first + t * nsub + u for u in range(nsub)]
        starts = [pl.multiple_of(j * tk, tk) for j in blocks]
        raw = [_pair_logits(k_ref[0, pl.ds(ks, tk), :], q_pairs, tq) for ks in starts]
        selected = [key_ref[j] >= thr for j in blocks]
        ms_new, ls_new, alphas, ps = [], [], [], []
        for h in range(N_HEADS):
            logits = [jnp.where(selected[u], raw[u][h], NEG) for u in range(nsub)]
            m_new = functools.reduce(jnp.maximum, [ms[h]] + [jnp.max(s, axis=0, keepdims=True) for s in logits])
            alpha = jnp.exp2(ms[h] - m_new)
            p = [jnp.exp2(s - m_new) for s in logits]
            ms_new.append(m_new)
            ls_new.append(alpha * ls[h] + functools.reduce(jnp.add, [jnp.sum(pu, axis=0, keepdims=True) for pu in p]))
            alphas.append(alpha)
            ps.append([pu.astype(BF) for pu in p])
        accs_new = []
        for pr in range(N_HEADS // 2):
            pv = functools.reduce(jnp.add, [_dot(vt_ref[0, blocks[u], slabs[pr], :],
                                                 jnp.concatenate([ps[2 * pr][u], ps[2 * pr + 1][u]], axis=1))
                                            for u in range(nsub)])
            for hh in range(2):
                h = 2 * pr + hh
                accs_new.append(alphas[h] * accs[h] + pv[hh * HEAD_DIM:(hh + 1) * HEAD_DIM, hh * tq:(hh + 1) * tq])
        return tuple(ms_new), tuple(ls_new), tuple(accs_new)

    carry = (tuple(jnp.full((1, tq), NEG, F32) for _ in range(N_HEADS)),
             tuple(jnp.zeros((1, tq), F32) for _ in range(N_HEADS)),
             tuple(jnp.zeros((HEAD_DIM, tq), F32) for _ in range(N_HEADS)))
    carry = lax.fori_loop(0, nblk // 4, functools.partial(body, first=0, nsub=4), carry)
    carry = lax.fori_loop(0, (nblk % 4) // 2, functools.partial(body, first=nblk - nblk % 4, nsub=2), carry)
    _, ls, accs = lax.fori_loop(0, nblk % 2, functools.partial(body, first=nblk - 1, nsub=1), carry)
    out_t = jnp.concatenate([accs[h] / ls[h] for h in range(N_HEADS)], axis=0)
    o_ref[0] = out_t.T.astype(o_ref.dtype)


BLK_PER_STEP = 8


def _mixer_c(q_all, iq, iw, ik_arr, k_arr, v_arr, kv_col, tq, tk, q_start, n_keys):
    b, l, _ = q_all.shape
    l_pad = _round_up(l, tq)
    q_all, iq, iw = (_pad_rows(t, l_pad) for t in (q_all, iq, iw))
    kp = k_arr.shape[1]
    n_kblk = kp // tk
    topk = min(TOPK_MAX, n_keys // 4)
    lower = (np.arange(tk)[:, None] >= np.arange(tk)[None, :]).astype(np.float32)
    n_blk_alloc = _round_up(n_kblk, BLK_PER_STEP)
    n_grp = n_blk_alloc * (tk // (32 * SUBLANES))
    vt = v_arr[:, :, kv_col * BRANCH_W:(kv_col + 1) * BRANCH_W].reshape(b, n_kblk, tk, BRANCH_W).transpose(0, 1, 3, 2)
    return pl.pallas_call(
        functools.partial(_mixer_c_kernel, tq=tq, tk=tk, q_start=q_start, n_keys=n_keys, topk=topk),
        grid=(b, l_pad // tq),
        in_specs=[pl.BlockSpec((1, tq, BRANCH_W), lambda bi, i: (bi, i, 2)),
                  pl.BlockSpec((1, tq, H_IDX * D_IDX), lambda bi, i: (bi, i, 0)),
                  pl.BlockSpec((1, H_IDX, tq), lambda bi, i: (bi, 0, i)),
                  _resident_spec(kp, 2 * D_IDX, 0),
                  _kv_spec(kp, kv_col),
                  pl.BlockSpec((1, n_kblk, BRANCH_W, tk), lambda bi, i: (bi, 0, 0, 0), pipeline_mode=pl.Buffered(1)),
                  pl.BlockSpec((tk, tk), lambda bi, i: (0, 0))],
        out_specs=pl.BlockSpec((1, tq, BRANCH_W), lambda bi, i: (bi, i, 0)),
        out_shape=jax.ShapeDtypeStruct((b, l_pad, BRANCH_W), BF),
        scratch_shapes=[pltpu.VMEM((n_blk_alloc, tk, tq), jnp.int32),
                        pltpu.VMEM((32, n_grp, SUBLANES, tq), jnp.int32), pltpu.VMEM((n_grp, SUBLANES, tq), jnp.int32)],
        compiler_params=_cparams("parallel", "parallel"), name="mixer_c",
    )(q_all, iq, jnp.moveaxis(iw, 1, 2), ik_arr, k_arr, vt, jnp.asarray(lower, BF))[:, :l]


def _mixer_d_kernel(q_ref, k_ref, v_ref, fq_ref, fk_ref, kbound_ref, fbound_ref, o_ref,
                    *, tq, tk, q_start, n_kblk, step):
    q0 = q_start + pl.program_id(1) * tq
    nblk = (q0 + tq - 1) // tk + 1
    n_full = (q0 + 1) // tk
    qpos = q0 + lax.broadcasted_iota(jnp.int32, (tq, tk), 0)
    kiota = lax.broadcasted_iota(jnp.int32, (tq, tk), 1)
    qm = _masked_pair_heads(q_ref)
    fq = [fq_ref[0, :, h:h + 1] for h in range(N_HEADS)]
    low = _low_half((tq, LANES))

    slabs = [slice(pr * LANES, (pr + 1) * LANES) for pr in range(N_HEADS // 2)]

    def body(t, carry, masked, first, nsub, sign):
        blocks = [first + sign * (t * nsub + u) for u in range(nsub)]
        starts = [pl.multiple_of(j * tk, tk) for j in blocks]
        raw = [[_dot_nt(qm[h], k_ref[0, pl.ds(ks, tk), slabs[h // 2]]) for h in range(N_HEADS)] for ks in starts]
        logits = []
        for u in range(nsub):
            row = [raw[u][h] + fq[h] - fk_ref[0, pl.ds(h * n_kblk + blocks[u], 1), :] for h in range(N_HEADS)]
            if masked:
                visible = (starts[u] + kiota) <= qpos
                row = [jnp.where(visible, s, NEG) for s in row]
            logits.append(row)
        return _softmax_step(carry, logits, lambda u, pr: v_ref[0, pl.ds(starts[u], tk), slabs[pr]], low)

    q_norm = [1.001 * jnp.sqrt(jnp.sum(jnp.square(qm[h].astype(F32)), axis=1, keepdims=True)) for h in range(N_HEADS)]
    bi = pl.program_id(0)

    def live(carry, j):
        gap = [q_norm[h] * kbound_ref[bi, h * n_kblk + j] + fq[h] + fbound_ref[bi, h * n_kblk + j] - carry[0][h]
               for h in range(N_HEADS)]
        return jnp.max(functools.reduce(jnp.maximum, gap)) > -(F32_UNDERFLOW_LOG2 + 8.0)

    carry = lax.fori_loop(0, nblk - n_full, functools.partial(body, masked=True, first=n_full, nsub=1, sign=1),
                          _softmax_init(tq))
    group = functools.partial(body, masked=False, first=n_full - 1, nsub=step, sign=-1)
    t_end, carry = lax.while_loop(
        lambda c: (c[0] < n_full // step) & live(c[1], jnp.maximum(n_full - 1 - c[0] * step, 0)),
        lambda c: (c[0] + 1, group(c[0], c[1])), (jnp.int32(0), carry))
    n_rest = n_full % step
    rest = jnp.where((t_end == n_full // step) & live(carry, jnp.maximum(n_rest - 1, 0)), n_rest, 0)
    carry = lax.fori_loop(0, rest, functools.partial(body, masked=False, first=n_rest - 1, nsub=1, sign=-1), carry)
    _softmax_finish(carry, o_ref, low)


def _mixer_d(q_all, k_arr, v_arr, kv_col, fq, fk, k_sq, tq, tk, q_start, step):
    b, l, _ = q_all.shape
    kp = k_arr.shape[1]
    n_kblk = kp // tk
    k_norm = 1.001 * jnp.sqrt(jnp.max(k_sq.reshape(b, n_kblk, tk, N_HEADS), axis=2))
    kbound = jnp.moveaxis(lax.cummax(k_norm, axis=1), 1, 2).reshape(b, N_HEADS * n_kblk)
    fbound = lax.cummax(jnp.max(-fk, axis=-1).reshape(b, N_HEADS, n_kblk), axis=2).reshape(b, N_HEADS * n_kblk)
    smem = pl.BlockSpec(memory_space=pltpu.SMEM)
    return pl.pallas_call(
        functools.partial(_mixer_d_kernel, tq=tq, tk=tk, q_start=q_start, n_kblk=n_kblk, step=step),
        grid=(b, l // tq),
        in_specs=[pl.BlockSpec((1, tq, BRANCH_W), lambda bi, i: (bi, i, 3)), _kv_spec(kp, kv_col),
                  _kv_spec(kp, kv_col),
                  pl.BlockSpec((1, tq, N_HEADS), lambda bi, i: (bi, i, 0)),
                  pl.BlockSpec((1, N_HEADS * n_kblk, tk), lambda bi, i: (bi, 0, 0)), smem, smem],
        out_specs=pl.BlockSpec((1, tq, BRANCH_W), lambda bi, i: (bi, i, 0)),
        out_shape=jax.ShapeDtypeStruct((b, l, BRANCH_W), BF),
        compiler_params=_cparams("parallel", "parallel"), name="mixer_d",
    )(q_all, k_arr, v_arr, fq, fk, kbound, fbound)


def _merge_kernel(x_ref, g1_ref, oa_ref, ob_ref, oc_ref, od_ref, wg_ref, wb_ref, wo_ref, y_ref):
    x = x_ref[...]
    h = _rms(x, g1_ref[...]).astype(BF)
    merged = None
    for g, o_ref in enumerate((oa_ref, ob_ref, oc_ref, od_ref)):
        gate = 1.0 / (1.0 + jnp.exp(-_dot(h, wg_ref[:, g * D_MODEL:(g + 1) * D_MODEL])))
        term = gate * _dot(o_ref[...], wb_ref[g])
        merged = term if merged is None else merged + term
    y_ref[...] = x + _dot(merged.astype(BF), wo_ref[...])


def _merge(x2, outs, lp, tm):
    m_rows = x2.shape[0]
    row = lambda i: (i, 0)
    const = lambda i: (0, 0)
    return pl.pallas_call(
        _merge_kernel, grid=(m_rows // tm,),
        in_specs=[pl.BlockSpec((tm, D_MODEL), row), pl.BlockSpec((1, D_MODEL), const)]
                 + [pl.BlockSpec((tm, BRANCH_W), row)] * N_MIXERS
                 + [pl.BlockSpec((D_MODEL, N_MIXERS * D_MODEL), const),
                    pl.BlockSpec((N_MIXERS, BRANCH_W, D_MODEL), lambda i: (0, 0, 0)),
                    pl.BlockSpec((D_MODEL, D_MODEL), const)],
        out_specs=pl.BlockSpec((tm, D_MODEL), row),
        out_shape=jax.ShapeDtypeStruct((m_rows, D_MODEL), F32),
        compiler_params=_cparams("parallel"), name="merge",
    )(x2, lp["g1"], *outs, lp["w_gate"], lp["w_branch"], lp["w_o"])


def _gelu_tanh(x):
    return x * (0.5 * (1.0 + jnp.tanh(np.sqrt(2.0 / np.pi).astype(np.float32) * (x + 0.044715 * (x * x * x)))))


def _ffn_kernel(x_ref, xp_ref, st_ref, g2_ref, win_ref, wc_ref, bc_ref, wd_ref, y_ref, ut_ref, h_ref, a_ref,
                *, tm, tf, tiles_per_seq):
    first_tile = pl.program_id(0) % tiles_per_seq == 0
    g2 = g2_ref[...]
    x = x_ref[...]
    h_ref[:CTX_ROWS, :] = _rms(xp_ref[...], g2).astype(BF)
    h_ref[CTX_ROWS:, :] = _rms(x, g2).astype(BF)
    h = h_ref[...]
    for f in range(0, D_FF, tf):
        u = _dot(h, win_ref[:, f:f + tf])
        gt = _dot(h, win_ref[:, D_FF + f:D_FF + f + tf])[CTX_ROWS:]
        ctx = jnp.where(first_tile, st_ref[0, :, f:f + tf], u[:CTX_ROWS])
        ue = jnp.concatenate([ctx, u[CTX_ROWS:]], axis=0)
        wc = wc_ref[:, f:f + tf]
        conv = bc_ref[:, f:f + tf] + ((ue[CTX_ROWS - 2:CTX_ROWS - 2 + tm] * wc[0:1]
                                       + ue[CTX_ROWS - 1:CTX_ROWS - 1 + tm] * wc[1:2]) + ue[CTX_ROWS:] * wc[2:3])
        a_ref[:, f:f + tf] = (_gelu_tanh(conv) * gt).astype(BF)
        ut_ref[0, :, f:f + tf] = ue[tm:]
    y_ref[...] = x + _dot(a_ref[...], wd_ref[...])


def _ffn(x2, state8, lp, tm, tf, seq_len):
    m_rows = x2.shape[0]
    tiles_per_seq = seq_len // tm
    const = lambda i: (0, 0)
    once = pl.Buffered(1)
    return pl.pallas_call(
        functools.partial(_ffn_kernel, tm=tm, tf=tf, tiles_per_seq=tiles_per_seq),
        grid=(m_rows // tm,),
        in_specs=[pl.BlockSpec((tm, D_MODEL), lambda i: (i, 0)),
                  pl.BlockSpec((CTX_ROWS, D_MODEL), lambda i: (jnp.maximum(i * (tm // CTX_ROWS) - 1, 0), 0)),
                  pl.BlockSpec((1, CTX_ROWS, D_FF), lambda i: (i // tiles_per_seq, 0, 0)),
                  pl.BlockSpec((1, D_MODEL), const),
                  pl.BlockSpec((D_MODEL, 2 * D_FF), const, pipeline_mode=once),
                  pl.BlockSpec((CONV_W, D_FF), const),
                  pl.BlockSpec((1, D_FF), const),
                  pl.BlockSpec((D_FF, D_MODEL), const, pipeline_mode=once)],
        out_specs=[pl.BlockSpec((tm, D_MODEL), lambda i: (i, 0)),
                   pl.BlockSpec((1, CTX_ROWS, D_FF), lambda i: (i, 0, 0))],
        out_shape=[jax.ShapeDtypeStruct((m_rows, D_MODEL), F32),
                   jax.ShapeDtypeStruct((m_rows // tm, CTX_ROWS, D_FF), F32)],
        scratch_shapes=[pltpu.VMEM((tm + CTX_ROWS, D_MODEL), BF), pltpu.VMEM((tm, D_FF), BF)],
        compiler_params=_cparams("parallel"), name="ffn",
    )(x2, x2, state8, lp["g2"], lp["w_ffn_in"], lp["w_conv"], lp["b_conv"], lp["w_down"])


def _rope_tables(q_start, length):
    half = HEAD_DIM // 2
    inv = ROPE_THETA ** (-jnp.arange(half, dtype=F32) / half)
    ang = (q_start + jnp.arange(length)).astype(F32)[:, None] * inv[None, :]
    cos = jnp.cos(ang)
    sin = jnp.sin(ang)
    cos_h = jnp.concatenate([cos, cos], axis=1)
    sin_h = jnp.concatenate([-sin, sin], axis=1)
    n_rot = H_IDX + 1
    pad = MISC_W - n_rot * HEAD_DIM
    return {
        "cq": jnp.tile(cos_h, (1, N_HEADS)), "sq": jnp.tile(sin_h, (1, N_HEADS)),
        "cm": jnp.concatenate([jnp.tile(cos_h, (1, n_rot)), jnp.ones((length, pad), F32)], axis=1),
        "sm": jnp.concatenate([jnp.tile(sin_h, (1, n_rot)), jnp.zeros((length, pad), F32)], axis=1),
    }


def _layer_params(norm1_g, w_in, b_forget, qk_g, rel_table, w_branch, w_o, norm2_g, w_ffn_in, w_conv, b_conv,
                  w_down):
    a = 3 * QKV_W
    i1 = a + H_IDX * D_IDX + D_IDX + H_IDX
    d1 = i1 + QKV_W
    f1 = d1 + N_HEADS
    n_misc = (i1 - a) + N_HEADS
    w_misc = jnp.concatenate([w_in[:, a:i1], w_in[:, d1:f1], jnp.zeros((D_MODEL, MISC_W - n_misc), w_in.dtype)],
                             axis=1)
    f_lane = D_IDX + H_IDX
    bf_row = jnp.zeros((1, LANES), F32).at[0, f_lane:f_lane + N_HEADS].set(b_forget.astype(F32))
    head_ones = (np.arange(BRANCH_W)[:, None] // HEAD_DIM == np.arange(BRANCH_W)[None, :] // HEAD_DIM)
    return {
        "g1": norm1_g.astype(F32)[None, :], "g2": norm2_g.astype(F32)[None, :],
        "w_qkv": jnp.concatenate([w_in[:, :a], w_in[:, i1:d1]], axis=1).astype(BF),
        "w_misc": w_misc.astype(BF), "w_gate": w_in[:, f1:].astype(BF),
        "qkg": jnp.tile(qk_g.astype(F32), (1, N_HEADS)), "bf_row": bf_row,
        "head_ones": jnp.asarray(head_ones.astype(np.float32), BF),
        "rel_table": rel_table, "w_branch": w_branch.astype(BF), "w_o": w_o.astype(BF),
        "w_ffn_in": w_ffn_in.astype(BF), "w_conv": w_conv.astype(F32), "b_conv": b_conv.astype(F32)[None, :],
        "w_down": w_down.astype(BF),
    }


def _pad_rows(a, rows):
    return jnp.pad(a, ((0, 0), (0, rows - a.shape[1])) + ((0, 0),) * (a.ndim - 2))


def _round_up(n, m):
    return -(-n // m) * m


def _layer(x, past, lp, tabs):
    b, l, _ = x.shape
    m_rows = b * l
    q_start = 0 if past is None else past[2].shape[1]
    n_keys = q_start + l
    tq = min(256, l)
    tq_c = LANES
    tk = 256
    tk_d = 512
    kp = _round_up(n_keys, max(tk, tk_d))
    tm = min(256, m_rows)

    x2 = x.reshape(m_rows, D_MODEL)
    (q_all, kbf, vbf, ka, va, kb, vb, kc, vc, kd, vd, iq, misc) = _projection(x2, lp, tabs, tm)
    ik = misc[:, :D_IDX].reshape(b, l, D_IDX)
    iw = misc[:, D_IDX:D_IDX + H_IDX].reshape(b, l, H_IDX)
    logf = misc[:, D_IDX + H_IDX:D_IDX + H_IDX + N_HEADS].reshape(b, l, N_HEADS)
    q_all = q_all.reshape(b, l, N_MIXERS * BRANCH_W)
    iq = iq.reshape(b, l, H_IDX * D_IDX)
    heads = lambda t: t.reshape(b, l, N_HEADS, HEAD_DIM)
    flat = lambda t: t.reshape(t.shape[0], t.shape[1], BRANCH_W)

    if past is None:
        kbf3 = kbf.reshape(b, l, N_MIXERS * BRANCH_W)
        vbf3 = vbf.reshape(b, l, N_MIXERS * BRANCH_W)
        kv = [(kbf3, vbf3, c) for c in range(N_MIXERS)]
        ik_all = ik.astype(BF)
        logf_all = logf
        a_args = (kbf3, vbf3, 0, _band_bias(lp["rel_table"], tq, BAND_PAST + tq, BAND_PAST + tq), tq, 3, tq)
        conv_state = jnp.zeros((b, CONV_W - 1, D_FF), F32)
    else:
        (pa_k, pa_v, pb_k, pb_v, pc_k, pc_v, pc_ki, pd_k, pd_v, pd_f, conv_state) = past
        new_bf = lambda t, c: t.reshape(b, l, N_MIXERS, BRANCH_W)[:, :, c]
        cat = lambda p, c, src, rows: _pad_rows(jnp.concatenate([flat(p).astype(BF), new_bf(src, c)], axis=1), rows)
        kv = [None] + [(cat(pk, c, kbf, kp), cat(pv, c, vbf, kp), 0)
                       for c, (pk, pv) in ((1, (pb_k, pb_v)), (2, (pc_k, pc_v)), (3, (pd_k, pd_v)))]
        ik_all = jnp.concatenate([pc_ki.astype(BF), ik.astype(BF)], axis=1)
        logf_all = jnp.concatenate([pd_f.astype(F32), logf], axis=1)
        a_keys = pa_k.shape[1] + l
        wa = _round_up(BAND_PAST + l, LANES)
        a_args = (cat(pa_k, 0, kbf, wa), cat(pa_v, 0, vbf, wa), 0,
                  _band_bias(lp["rel_table"], l, wa, a_keys), l, 1, wa)

    f_cum = jnp.cumsum(logf_all, axis=1) * LOG2E
    fq = f_cum[:, q_start:]
    fk = jnp.moveaxis(_pad_rows(f_cum, kp), 1, 2).reshape(b, N_HEADS * (kp // tk_d), tk_d)

    o_a = _mixer_a(q_all, *a_args)
    o_b = _mixer_b(q_all, *kv[1], tq, tk, q_start)
    ik_pair = _pad_rows(jnp.concatenate([ik_all, ik_all], axis=-1), kp)
    o_c = _mixer_c(q_all, iq, iw, ik_pair, *kv[2], tq_c, tk_d, q_start, n_keys)
    head_of_col = jnp.asarray(np.arange(BRANCH_W)[:, None] // HEAD_DIM == np.arange(N_HEADS)[None, :], F32)
    kd_sq = jnp.dot(jnp.square(kd.astype(BF).astype(F32)), head_of_col,
                    precision=lax.Precision.HIGHEST).reshape(b, l, N_HEADS)
    if past is not None:
        kd_sq = jnp.concatenate([jnp.sum(jnp.square(past[7].astype(BF).astype(F32)), axis=-1), kd_sq], axis=1)
    o_d = _mixer_d(q_all, *kv[3], fq, fk, _pad_rows(kd_sq, kp), tq, tk_d, q_start, 2)
    x2 = _merge(x2, [o.reshape(m_rows, BRANCH_W) for o in (o_a, o_b, o_c, o_d)], lp, tm)

    state8 = jnp.pad(conv_state.astype(F32), ((0, 0), (CTX_ROWS - (CONV_W - 1), 0), (0, 0)))
    x2, u_tail = _ffn(x2, state8, lp, min(512, l), 256, l)
    u_tail = u_tail.reshape(b, -1, CTX_ROWS, D_FF)[:, -1]

    rows3 = lambda t: t.reshape(b, l, BRANCH_W)
    if past is None:
        a_keep = min(BAND_PAST, l)
        new_a = tuple(rows3(t)[:, -a_keep:].reshape(b, a_keep, N_HEADS, HEAD_DIM) for t in (ka, va))
    else:
        a_keep = pa_k.shape[1]
        new_a = (jnp.concatenate([pa_k, heads(ka)], axis=1)[:, -a_keep:],
                 jnp.concatenate([pa_v, heads(va)], axis=1)[:, -a_keep:])
    new = new_a + (rows3(kb), rows3(vb), rows3(kc), rows3(vc), ik, rows3(kd), rows3(vd), logf,
                   u_tail[:, -(CONV_W - 1):])
    return x2.reshape(b, l, D_MODEL), new


def kernel(x_prompt, x_sample, cache_a_k, cache_a_v, cache_b_k, cache_b_v, cache_c_k, cache_c_v, cache_c_kidx, cache_d_k, cache_d_v, cache_d_logf, state_ffn_conv, norm1_g, w_in, b_forget, qk_norm_g, rel_bias, w_branch, w_o, norm2_g, w_ffn_in, w_conv, b_conv, w_down):
    depth = w_in.shape[0]
    past_len = cache_b_k.shape[2]
    tabs_p = _rope_tables(0, x_prompt.shape[1])
    tabs_s = {k: jnp.tile(v, (x_sample.shape[0], 1)) for k, v in _rope_tables(past_len, x_sample.shape[1]).items()}
    y_p, y_s = x_prompt, x_sample
    p_states, s_states = [], []
    for d in range(depth):
        lp = _layer_params(norm1_g[d], w_in[d], b_forget[d], qk_norm_g[d], rel_bias[d], w_branch[d], w_o[d],
                           norm2_g[d], w_ffn_in[d], w_conv[d], b_conv[d], w_down[d])
        y_p, st_p = _layer(y_p, None, lp, tabs_p)
        past = (cache_a_k[d], cache_a_v[d], cache_b_k[d], cache_b_v[d], cache_c_k[d], cache_c_v[d],
                cache_c_kidx[d], cache_d_k[d], cache_d_v[d], cache_d_logf[d], state_ffn_conv[d])
        y_s, st_s = _layer(y_s, past, lp, tabs_s)
        p_states.append(st_p)
        s_states.append(st_s)
    def assemble(states):
        out = [jnp.stack(t) for t in zip(*states)]
        for i in (2, 3, 4, 5, 7, 8):
            out[i] = out[i].reshape(out[i].shape[:-1] + (N_HEADS, HEAD_DIM))
        return out

    return (y_p, y_s, *assemble(p_states), *assemble(s_states))
```

```python
import functools

import numpy as np
import jax
import jax.numpy as jnp
from jax import lax
from jax.experimental import pallas as pl
from jax.experimental.pallas import tpu as pltpu

D_MODEL = 1024
CHUNK = 64
N_MIXERS = 4
N_HEADS = 4
HEAD_DIM = 64
BRANCH_W = N_HEADS * HEAD_DIM
BAND_PAST = 8 * CHUNK
REL_CLIP = 128
H_IDX = 4
D_IDX = 64
TOPK_MAX = 256
ROPE_THETA = 10000.0
D_FF = 2816
CONV_W = 3
EPS = 1e-6

BF = jnp.bfloat16
F32 = jnp.float32
NEG = -1e30
LOG2E = float(np.log2(np.e))
F32_UNDERFLOW_LOG2 = 160.0
INT_MIN = -2 ** 31
LANES = 128
SUBLANES = 8
CTX_ROWS = 2 * SUBLANES
QKV_W = 3 * BRANCH_W
MISC_W = 3 * LANES
VMEM_LIMIT = 56 * 1024 * 1024


def _cparams(*sem):
    return pltpu.CompilerParams(dimension_semantics=sem, vmem_limit_bytes=VMEM_LIMIT)


def _dot(a, b):
    return jnp.dot(a, b, preferred_element_type=F32)


def _dot_nt(a, b):
    return lax.dot_general(a, b, (((1,), (1,)), ((), ())), preferred_element_type=F32)


def _split_dot(x, m, terms):
    acc = None
    r = x
    for t in range(terms):
        hi = r.astype(BF)
        d = _dot(hi, m)
        acc = d if acc is None else acc + d
        if t + 1 < terms:
            r = r - hi.astype(F32)
    return acc


def _rms(x, g):
    ms = jnp.mean(x * x, axis=-1, keepdims=True)
    return x * lax.rsqrt(ms + EPS) * g


def _head_rms(t, g, head_ones):
    ms = _split_dot(t * t, head_ones, 3) * (1.0 / HEAD_DIM)
    return t * lax.rsqrt(ms + EPS) * g


def _swap_halves(y):
    n = y.shape[-1]
    lane = lax.broadcasted_iota(jnp.int32, y.shape, 1)
    first = (lane & (HEAD_DIM - 1)) < (HEAD_DIM // 2)
    return jnp.where(first, pltpu.roll(y, n - HEAD_DIM // 2, 1), pltpu.roll(y, HEAD_DIM // 2, 1))


def _softplus2(z):
    neg_abs = pltpu.bitcast(pltpu.bitcast(z, jnp.int32) | jnp.int32(INT_MIN), F32)
    return jnp.maximum(z, 0.0) + jnp.log2(1.0 + jnp.exp2(neg_abs))


def _tree_sum(xs):
    while len(xs) > 1:
        xs = [xs[i] + xs[i + 1] for i in range(0, len(xs) - 1, 2)] + ([xs[-1]] if len(xs) % 2 else [])
    return xs[0]


def _low_half(shape):
    return lax.broadcasted_iota(jnp.int32, shape, 1) < HEAD_DIM


def _masked_pair_heads(ref):
    out = []
    for pr in range(ref.shape[-1] // LANES):
        slab = ref[0, :, pr * LANES:(pr + 1) * LANES].astype(F32)
        low = _low_half(slab.shape)
        out.append(jnp.where(low, slab, 0.0).astype(BF))
        out.append(jnp.where(low, 0.0, slab).astype(BF))
    return out


def _stacked_pair_heads(ref):
    masked = _masked_pair_heads(ref)
    return [jnp.concatenate(masked[2 * pr:2 * pr + 2], axis=0) for pr in range(len(masked) // 2)]


def _pair_logits(keys, stacked, tq):
    out = []
    for pr, rhs in enumerate(stacked):
        slab = keys if keys.shape[-1] == LANES else keys[:, pr * LANES:(pr + 1) * LANES]
        both = _dot_nt(slab, rhs)
        out += [both[:, :tq], both[:, tq:]]
    return out


def _softmax_init(tq):
    return (tuple(jnp.full((tq, 1), NEG, F32) for _ in range(N_HEADS)),
            tuple(jnp.zeros((tq, 1), F32) for _ in range(N_HEADS)),
            tuple(jnp.zeros((tq, LANES), F32) for _ in range(N_HEADS // 2)))


def _softmax_step(carry, logits, v_tile, low):
    ms, ls, accs = carry
    tiles = range(len(logits))
    ms_new, ls_new, alphas, ps = [], [], [], []
    for h in range(N_HEADS):
        m_new = functools.reduce(jnp.maximum, [ms[h]] + [jnp.max(logits[u][h], axis=1, keepdims=True) for u in tiles])
        alpha = jnp.exp2(ms[h] - m_new)
        p = [jnp.exp2(logits[u][h] - m_new) for u in tiles]
        ms_new.append(m_new)
        ls_new.append(alpha * ls[h] + functools.reduce(jnp.add, [jnp.sum(pu, axis=1, keepdims=True) for pu in p]))
        alphas.append(alpha)
        ps.append([pu.astype(BF) for pu in p])
    pvs = [functools.reduce(jnp.add, [_dot(ps[h][u], v_tile(u, h // 2)) for u in tiles]) for h in range(N_HEADS)]
    accs_new = [jnp.where(low, alphas[2 * pr], alphas[2 * pr + 1]) * accs[pr]
                + jnp.where(low, pvs[2 * pr], pvs[2 * pr + 1]) for pr in range(N_HEADS // 2)]
    return tuple(ms_new), tuple(ls_new), tuple(accs_new)


def _softmax_finish(carry, o_ref, low):
    _, ls, accs = carry
    for pr in range(N_HEADS // 2):
        l_pair = jnp.where(low, ls[2 * pr], ls[2 * pr + 1])
        o_ref[0, :, pr * LANES:(pr + 1) * LANES] = (accs[pr] / l_pair).astype(o_ref.dtype)


def _proj_kernel(x_ref, g1_ref, wqkv_ref, wmisc_ref, qkg_ref, hones_ref, cq_ref, sq_ref, cm_ref, sm_ref,
                 bf_ref, q_ref, kbf_ref, vbf_ref, ka_ref, va_ref, kb_ref, vb_ref, kc_ref, vc_ref,
                 kd_ref, vd_ref, iq_ref, misc_ref):
    h = _rms(x_ref[...], g1_ref[...]).astype(BF)
    hones = hones_ref[...]
    k32 = (ka_ref, kb_ref, kc_ref, kd_ref)
    v32 = (va_ref, vb_ref, vc_ref, vd_ref)
    norm_row = (0, None, 2, 4)
    for m in range(N_MIXERS):
        y = _dot(h, wqkv_ref[:, m * QKV_W:(m + 1) * QKV_W])
        q, k, v = y[:, :BRANCH_W], y[:, BRANCH_W:2 * BRANCH_W], y[:, 2 * BRANCH_W:]
        if norm_row[m] is not None:
            r = norm_row[m]
            q = _head_rms(q, qkg_ref[r:r + 1, :], hones)
            k = _head_rms(k, qkg_ref[r + 1:r + 2, :], hones)
        if m == 2:
            c, s = cq_ref[...], sq_ref[...]
            q = q * c + _swap_halves(q) * s
            k = k * c + _swap_halves(k) * s
        cols = slice(m * BRANCH_W, (m + 1) * BRANCH_W)
        q_ref[:, cols] = (q * (HEAD_DIM ** -0.5 * LOG2E)).astype(BF)
        kbf_ref[:, cols] = k.astype(BF)
        vbf_ref[:, cols] = v.astype(BF)
        k32[m][...] = k
        v32[m][...] = v
    ym = _dot(h, wmisc_ref[...])
    r = ym * cm_ref[...] + _swap_halves(ym) * sm_ref[...]
    iq_ref[...] = r[:, :H_IDX * D_IDX].astype(BF)
    g2 = r[:, H_IDX * D_IDX:]
    lane = lax.broadcasted_iota(jnp.int32, g2.shape, 1)
    z = g2 + bf_ref[...]
    logf = jnp.minimum(z, 0.0) - jnp.log1p(jnp.exp(-jnp.abs(z)))
    is_w = (lane >= D_IDX) & (lane < D_IDX + H_IDX)
    is_f = (lane >= D_IDX + H_IDX) & (lane < D_IDX + H_IDX + N_HEADS)
    misc_ref[...] = jnp.where(is_f, logf, jnp.where(is_w, g2 * (H_IDX ** -0.5 * D_IDX ** -0.5), g2))


def _projection(x2, lp, tabs, tm):
    m_rows = x2.shape[0]
    n_tab = tabs["cq"].shape[0] // tm
    row = lambda i: (i, 0)
    const = lambda i: (0, 0)
    tab = lambda i: (i % n_tab, 0)
    wide = pl.BlockSpec((tm, N_MIXERS * BRANCH_W), row)
    head = pl.BlockSpec((tm, BRANCH_W), row)
    in_specs = [
        pl.BlockSpec((tm, D_MODEL), row),
        pl.BlockSpec((1, D_MODEL), const),
        pl.BlockSpec((D_MODEL, N_MIXERS * QKV_W), const),
        pl.BlockSpec((D_MODEL, MISC_W), const),
        pl.BlockSpec((6, BRANCH_W), const),
        pl.BlockSpec((BRANCH_W, BRANCH_W), const),
        pl.BlockSpec((tm, BRANCH_W), tab),
        pl.BlockSpec((tm, BRANCH_W), tab),
        pl.BlockSpec((tm, MISC_W), tab),
        pl.BlockSpec((tm, MISC_W), tab),
        pl.BlockSpec((1, LANES), const),
    ]
    out_shape = ([jax.ShapeDtypeStruct((m_rows, N_MIXERS * BRANCH_W), BF)] * 3
                 + [jax.ShapeDtypeStruct((m_rows, BRANCH_W), F32)] * 8
                 + [jax.ShapeDtypeStruct((m_rows, BRANCH_W), BF),
                    jax.ShapeDtypeStruct((m_rows, LANES), F32)])
    out_specs = [wide] * 3 + [head] * 8 + [head, pl.BlockSpec((tm, LANES), row)]
    return pl.pallas_call(
        _proj_kernel, grid=(m_rows // tm,), in_specs=in_specs, out_specs=out_specs, out_shape=out_shape,
        compiler_params=_cparams("parallel"), name="projection",
    )(x2, lp["g1"], lp["w_qkv"], lp["w_misc"], lp["qkg"], lp["head_ones"],
      tabs["cq"], tabs["sq"], tabs["cm"], tabs["sm"], lp["bf_row"])


def _mixer_a_kernel(*refs, nwb, wb):
    q_ref = refs[0]
    k_refs = refs[1:1 + nwb]
    v_refs = refs[1 + nwb:1 + 2 * nwb]
    bias_ref = refs[1 + 2 * nwb]
    o_ref = refs[2 + 2 * nwb]
    i = pl.program_id(1)
    qm = _masked_pair_heads(q_ref)
    low = _low_half((q_ref.shape[1], LANES))
    for pr in range(N_HEADS // 2):
        slab = slice(pr * LANES, (pr + 1) * LANES)
        outs = []
        for h in (2 * pr, 2 * pr + 1):
            logits = []
            for r in range(nwb):
                s = _dot_nt(qm[h], k_refs[r][0, :, slab]) + bias_ref[h, :, r * wb:(r + 1) * wb]
                logits.append(jnp.where(i - (nwb - 1) + r >= 0, s, NEG))
            m = functools.reduce(jnp.maximum, [jnp.max(s, axis=1, keepdims=True) for s in logits])
            ps = [jnp.exp2(s - m) for s in logits]
            l = functools.reduce(jnp.add, [jnp.sum(p, axis=1, keepdims=True) for p in ps])
            acc = functools.reduce(jnp.add, [_dot(p.astype(BF), v_refs[r][0, :, slab]) for r, p in enumerate(ps)])
            outs.append(acc / l)
        o_ref[0, :, slab] = jnp.where(low, outs[0], outs[1]).astype(o_ref.dtype)


def _mixer_a(q_all, k_arr, v_arr, kv_col, bias, tq, nwb, wb):
    b, l, _ = q_all.shape
    kspec = lambda r: pl.BlockSpec((1, wb, BRANCH_W),
                                   lambda bi, i, r=r: (bi, jnp.maximum(i - (nwb - 1) + r, 0), kv_col))
    in_specs = ([pl.BlockSpec((1, tq, BRANCH_W), lambda bi, i: (bi, i, 0))]
                + [kspec(r) for r in range(nwb)] * 2
                + [pl.BlockSpec(bias.shape, lambda bi, i: (0, 0, 0))])
    return pl.pallas_call(
        functools.partial(_mixer_a_kernel, nwb=nwb, wb=wb),
        grid=(b, l // tq), in_specs=in_specs,
        out_specs=pl.BlockSpec((1, tq, BRANCH_W), lambda bi, i: (bi, i, 0)),
        out_shape=jax.ShapeDtypeStruct((b, l, BRANCH_W), BF),
        compiler_params=_cparams("parallel", "parallel"), name="mixer_a",
    )(q_all, *([k_arr] * nwb), *([v_arr] * nwb), bias)


def _band_bias(rel_table, tq, w, n_valid_cols):
    t = np.arange(tq)[:, None]
    c = np.arange(w)[None, :]
    krel = c - BAND_PAST
    ct = t // CHUNK
    inband = (krel >= CHUNK * ct - BAND_PAST) & (krel < CHUNK * ct + CHUNK) & (c < n_valid_cols)
    n = tq + w - 1
    idx = np.clip(np.arange(n) - (w - 1 - BAND_PAST), -REL_CLIP, REL_CLIP) + REL_CLIP
    r0, r1 = int((idx == idx[0]).sum()), int((idx == idx[-1]).sum())
    tab = rel_table.astype(F32) * LOG2E
    f = jnp.concatenate([jnp.broadcast_to(tab[idx[0]], (r0, N_HEADS)), tab[idx[r0]:idx[n - r1 - 1] + 1],
                         jnp.broadcast_to(tab[idx[-1]], (r1, N_HEADS))], axis=0).T
    hankel = jnp.tile(f, (1, tq + 1))[:, :tq * (n + 1)].reshape(N_HEADS, tq, n + 1)[:, :, :w]
    return jnp.where(inband[None], hankel[:, :, ::-1], NEG)


def _mixer_b_kernel(q_ref, k_ref, v_ref, upper_ref, o_ref, *, tq, tk, q_start):
    q0 = q_start + pl.program_id(1) * tq
    nblk = (q0 + tq - 2) // tk + 1
    upper = upper_ref[...]
    qpos = q0 + lax.broadcasted_iota(jnp.int32, (tq, tk), 0)
    kiota = lax.broadcasted_iota(jnp.int32, (tq, tk), 1)
    n_full = q0 // tk
    qm = _masked_pair_heads(q_ref)
    low = _low_half((tq, LANES))

    slabs = [slice(pr * LANES, (pr + 1) * LANES) for pr in range(N_HEADS // 2)]

    def body(t, carry, masked, first, nsub):
        accs, laters = carry
        starts = [pl.multiple_of((first - t * nsub - u) * tk, tk) for u in range(nsub)]
        masks = [(ks + kiota) < qpos for ks in starts]
        zs = [[_dot_nt(qm[h], k_ref[0, pl.ds(ks, tk), slabs[h // 2]]) for h in range(N_HEADS)] for ks in starts]
        log_betas, leaves = [], []
        for u in range(nsub):
            sps = [_softplus2(z) for z in zs[u]]
            leaves.append([jnp.where(masks[u], sp, 0.0) for sp in sps] if masked else sps)
            log_betas.append([z - sp for z, sp in zip(zs[u], sps)])
        tails = [[_dot(s.astype(BF), upper) for s in leaves[u]] for u in range(nsub)]
        laters = list(laters)
        ws = []
        for u in range(nsub):
            ws_u = []
            for h in range(N_HEADS):
                w = jnp.exp2(log_betas[u][h] - (tails[u][h] + laters[h]))
                ws_u.append((jnp.where(masks[u], w, 0.0) if masked else w).astype(BF))
                laters[h] = laters[h] + jnp.sum(leaves[u][h], axis=1, keepdims=True)
            ws.append(ws_u)
        wvs = [functools.reduce(jnp.add, [_dot(ws[u][h], v_ref[0, pl.ds(starts[u], tk), slabs[h // 2]])
                                          for u in range(nsub)]) for h in range(N_HEADS)]
        accs_new = [accs[pr] + jnp.where(low, wvs[2 * pr], wvs[2 * pr + 1]) for pr in range(N_HEADS // 2)]
        return tuple(accs_new), tuple(laters)

    carry = (tuple(jnp.zeros((tq, LANES), F32) for _ in range(N_HEADS // 2)),
             tuple(jnp.zeros((tq, 1), F32) for _ in range(N_HEADS)))
    carry = lax.fori_loop(0, nblk - n_full, functools.partial(body, masked=True, first=nblk - 1, nsub=1), carry)

    def live(carry):
        return jnp.min(functools.reduce(jnp.minimum, carry[1])) < F32_UNDERFLOW_LOG2

    lead = jnp.where((n_full >= 1) & live(carry), 1, 0)
    carry = lax.fori_loop(0, lead, functools.partial(body, masked=False, first=n_full - 1, nsub=1), carry)
    rest = jnp.maximum(n_full - 1, 0)
    pair = functools.partial(body, masked=False, first=n_full - 2, nsub=2)
    t_end, carry = lax.while_loop(lambda c: (c[0] < rest // 2) & live(c[1]),
                                  lambda c: (c[0] + 1, pair(c[0], c[1])), (jnp.int32(0), carry))
    last = jnp.where((t_end == rest // 2) & live(carry), rest % 2, 0)
    accs, _ = lax.fori_loop(0, last, functools.partial(body, masked=False, first=0, nsub=1), carry)
    for pr in range(N_HEADS // 2):
        o_ref[0, :, pr * LANES:(pr + 1) * LANES] = accs[pr].astype(o_ref.dtype)


def _resident_spec(kp, width, col):
    return pl.BlockSpec((1, kp, width), lambda bi, i: (bi, 0, col), pipeline_mode=pl.Buffered(1))


def _kv_spec(kp, col):
    return _resident_spec(kp, BRANCH_W, col)


def _mixer_b(q_all, k_arr, v_arr, kv_col, tq, tk, q_start):
    b, l, _ = q_all.shape
    kp = k_arr.shape[1]
    upper = (np.arange(tk)[:, None] > np.arange(tk)[None, :]).astype(np.float32)
    return pl.pallas_call(
        functools.partial(_mixer_b_kernel, tq=tq, tk=tk, q_start=q_start),
        grid=(b, l // tq),
        in_specs=[pl.BlockSpec((1, tq, BRANCH_W), lambda bi, i: (bi, i, 1)), _kv_spec(kp, kv_col),
                  _kv_spec(kp, kv_col), pl.BlockSpec((tk, tk), lambda bi, i: (0, 0))],
        out_specs=pl.BlockSpec((1, tq, BRANCH_W), lambda bi, i: (bi, i, 0)),
        out_shape=jax.ShapeDtypeStruct((b, l, BRANCH_W), BF),
        compiler_params=_cparams("parallel", "parallel"), name="mixer_b",
    )(q_all, k_arr, v_arr, jnp.asarray(upper, BF))


def _mixer_c_kernel(q_ref, iq_ref, iw_ref, ik_ref, k_ref, vt_ref, lower_ref, o_ref, key_ref, plane_ref, tied_ref,
                    *, tq, tk, q_start, n_keys, topk):
    q0 = q_start + pl.program_id(1) * tq
    last_adm = jnp.minimum(((q0 + tq - 1) // CHUNK) * CHUNK + CHUNK - 1, n_keys - 1)
    nblk = last_adm // tk + 1
    n_full = jnp.minimum((q0 // CHUNK + 1) * CHUNK, n_keys) // tk
    qchunk = (q0 + lax.broadcasted_iota(jnp.int32, (tk, tq), 1)) // CHUNK
    kiota = lax.broadcasted_iota(jnp.int32, (tk, tq), 0)
    topk_f = jnp.float32(topk)
    iq_pairs = _stacked_pair_heads(iq_ref)
    iw_rows = [iw_ref[0, g:g + 1, :] for g in range(H_IDX)]
    slabs = [slice(pr * LANES, (pr + 1) * LANES) for pr in range(N_HEADS // 2)]

    def score_body(t, carry, masked, first, nsub):
        blocks = [first + t * nsub + u for u in range(nsub)]
        starts = [pl.multiple_of(j * tk, tk) for j in blocks]
        dots = [_pair_logits(ik_ref[0, pl.ds(ks, tk), :], iq_pairs, tq) for ks in starts]
        for u in range(nsub):
            sc = None
            for g in range(H_IDX):
                term = iw_rows[g] * jnp.maximum(dots[u][g], 0.0)
                sc = term if sc is None else sc + term
            sc = jnp.where(sc == 0.0, 0.0, sc)
            bits = pltpu.bitcast(sc, jnp.int32)
            key = bits ^ ((bits >> 31) & jnp.int32(0x7FFFFFFF))
            if masked:
                kpos = starts[u] + kiota
                adm = ((kpos // CHUNK) <= qchunk) & (kpos < n_keys)
                key = jnp.where(adm, key, jnp.int32(INT_MIN))
            key_ref[blocks[u]] = key
        return carry

    lax.fori_loop(0, n_full // 4, functools.partial(score_body, masked=False, first=0, nsub=4), 0)
    lax.fori_loop(0, (n_full % 4) // 2, functools.partial(score_body, masked=False, first=n_full - n_full % 4,
                                                          nsub=2), 0)
    lax.fori_loop(0, n_full % 2, functools.partial(score_body, masked=False, first=n_full - 1, nsub=1), 0)
    lax.fori_loop(0, nblk - n_full, functools.partial(score_body, masked=True, first=n_full, nsub=1), 0)

    grp_per_blk = tk // (32 * SUBLANES)
    n_quad = (nblk + BLK_PER_STEP - 1) // BLK_PER_STEP
    grp_per_step = BLK_PER_STEP * grp_per_blk

    def blank_block(j, carry):
        for half in range(grp_per_blk):
            g = j * grp_per_blk + half
            for t in range(32):
                plane_ref[t, g] = jnp.zeros((SUBLANES, tq), jnp.int32)
            tied_ref[g] = jnp.full((SUBLANES, tq), -1, jnp.int32)
        return carry

    lax.fori_loop(nblk, n_quad * BLK_PER_STEP, blank_block, 0)

    def transpose_block(j, carry):
        for half in range(grp_per_blk):
            base = half * 32 * SUBLANES
            w = [key_ref[j, base + c * SUBLANES:base + (c + 1) * SUBLANES, :] ^ jnp.int32(INT_MIN) for c in range(32)]
            s, m = 16, 0x0000FFFF
            while s:
                k = 0
                while k < 32:
                    t = (w[k] ^ (w[k + s] >> s)) & jnp.int32(m)
                    w[k] = w[k] ^ t
                    w[k + s] = w[k + s] ^ (t << s)
                    k = (k + s + 1) & ~s
                s >>= 1
                m ^= (m << s) & 0xFFFFFFFF
            g = j * grp_per_blk + half
            for t in range(32):
                plane_ref[t, g] = w[t]
            tied_ref[g] = jnp.full((SUBLANES, tq), -1, jnp.int32)
        return carry

    lax.fori_loop(0, nblk, transpose_block, 0)

    def sweep(t, narrow_by, count_bit):
        def step(i, acc):
            grps = pl.ds(i * grp_per_step, grp_per_step)
            tied = tied_ref[grps]
            if narrow_by is not None:
                hit = tied & plane_ref[t - 1, grps]
                tied = jnp.where(narrow_by != 0, hit, tied ^ hit)
                tied_ref[grps] = tied
            if not count_bit:
                return acc
            hits = lax.population_count(tied & plane_ref[t, grps])
            return acc + _tree_sum([hits[u] for u in range(grp_per_step)])
        return lax.fori_loop(0, n_quad, step, jnp.zeros((SUBLANES, tq), jnp.int32))

    def decide(t, ones, prefix, n_above):
        cnt = n_above + jnp.sum(ones.astype(F32), axis=0, keepdims=True)
        take = cnt >= topk_f
        bit = jnp.left_shift(jnp.int32(1), 31 - t)
        return jnp.where(take, prefix | bit, prefix), jnp.where(take, n_above, cnt), take.astype(jnp.int32)

    def bit_body(t, carry):
        prefix, n_above, take = carry
        return decide(t, sweep(t, take, True), prefix, n_above)

    first = decide(0, sweep(0, None, True), jnp.zeros((1, tq), jnp.int32), jnp.zeros((1, tq), F32))
    prefix, n_above, take = lax.fori_loop(1, 32, bit_body, first)
    sweep(32, take, False)
    thr = jnp.maximum(prefix ^ jnp.int32(INT_MIN), jnp.int32(INT_MIN + 1))

    def tally_tied(i, acc):
        ones = lax.population_count(tied_ref[pl.ds(i * grp_per_step, grp_per_step)])
        return acc + _tree_sum([ones[u] for u in range(grp_per_step)])

    n_tied = lax.fori_loop(0, n_quad, tally_tied, jnp.zeros((SUBLANES, tq), jnp.int32))
    n_tied = jnp.sum(n_tied.astype(F32), axis=0, keepdims=True)
    n_ge = jnp.where(prefix == 0, 0.0, n_above + n_tied)
    need = topk_f - n_above

    @pl.when(jnp.max(n_ge) > topk_f)
    def _():
        def fix(j, seen):
            kb = key_ref[j]
            eq = jnp.where(kb == thr, 1.0, 0.0)
            rank = _dot(lower_ref[...], eq.astype(BF)) + seen
            key_ref[j] = jnp.where(eq * rank > need, jnp.int32(INT_MIN), kb)
            return seen + jnp.sum(eq, axis=0, keepdims=True)
        lax.fori_loop(0, nblk, fix, jnp.zeros((1, tq), F32))

    q_pairs = _stacked_pair_heads(q_ref)

    def body(t, carry, first, nsub):
        ms, ls, accs = carry
        blocks = [first + t * nsub + u for u in range(nsub)]
        starts = [pl.multiple_of(j * tk, tk) for j in blocks]
        raw = [_pair_logits(k_ref[0, pl.ds(ks, tk), :], q_pairs, tq) for ks in starts]
        selected = [key_ref[j] >= thr for j in blocks]
        ms_new, ls_new, alphas, ps = [], [], [], []
        for h in range(N_HEADS):
            logits = [jnp.where(selected[u], raw[u][h], NEG) for u in range(nsub)]
            m_new = functools.reduce(jnp.maximum, [ms[h]] + [jnp.max(s, axis=0, keepdims=True) for s in logits])
            alpha = jnp.exp2(ms[h] - m_new)
            p = [jnp.exp2(s - m_new) for s in logits]
            ms_new.append(m_new)
            ls_new.append(alpha * ls[h] + functools.reduce(jnp.add, [jnp.sum(pu, axis=0, keepdims=True) for pu in p]))
            alphas.append(alpha)
            ps.append([pu.astype(BF) for pu in p])
        accs_new = []
        for pr in range(N_HEADS // 2):
            pv = functools.reduce(jnp.add, [_dot(vt_ref[0, blocks[u], slabs[pr], :],
                                                 jnp.concatenate([ps[2 * pr][u], ps[2 * pr + 1][u]], axis=1))
                                            for u in range(nsub)])
            for hh in range(2):
                h = 2 * pr + hh
                accs_new.append(alphas[h] * accs[h] + pv[hh * HEAD_DIM:(hh + 1) * HEAD_DIM, hh * tq:(hh + 1) * tq])
        return tuple(ms_new), tuple(ls_new), tuple(accs_new)

    carry = (tuple(jnp.full((1, tq), NEG, F32) for _ in range(N_HEADS)),
             tuple(jnp.zeros((1, tq), F32) for _ in range(N_HEADS)),
             tuple(jnp.zeros((HEAD_DIM, tq), F32) for _ in range(N_HEADS)))
    carry = lax.fori_loop(0, nblk // 4, functools.partial(body, first=0, nsub=4), carry)
    carry = lax.fori_loop(0, (nblk % 4) // 2, functools.partial(body, first=nblk - nblk % 4, nsub=2), carry)
    _, ls, accs = lax.fori_loop(0, nblk % 2, functools.partial(body, first=nblk - 1, nsub=1), carry)
    out_t = jnp.concatenate([accs[h] / ls[h] for h in range(N_HEADS)], axis=0)
    o_ref[0] = out_t.T.astype(o_ref.dtype)


BLK_PER_STEP = 8


def _mixer_c(q_all, iq, iw, ik_arr, k_arr, v_arr, kv_col, tq, tk, q_start, n_keys):
    b, l, _ = q_all.shape
    l_pad = _round_up(l, tq)
    q_all, iq, iw = (_pad_rows(t, l_pad) for t in (q_all, iq, iw))
    kp = k_arr.shape[1]
    n_kblk = kp // tk
    topk = min(TOPK_MAX, n_keys // 4)
    lower = (np.arange(tk)[:, None] >= np.arange(tk)[None, :]).astype(np.float32)
    n_blk_alloc = _round_up(n_kblk, BLK_PER_STEP)
    n_grp = n_blk_alloc * (tk // (32 * SUBLANES))
    vt = v_arr[:, :, kv_col * BRANCH_W:(kv_col + 1) * BRANCH_W].reshape(b, n_kblk, tk, BRANCH_W).transpose(0, 1, 3, 2)
    return pl.pallas_call(
        functools.partial(_mixer_c_kernel, tq=tq, tk=tk, q_start=q_start, n_keys=n_keys, topk=topk),
        grid=(b, l_pad // tq),
        in_specs=[pl.BlockSpec((1, tq, BRANCH_W), lambda bi, i: (bi, i, 2)),
                  pl.BlockSpec((1, tq, H_IDX * D_IDX), lambda bi, i: (bi, i, 0)),
                  pl.BlockSpec((1, H_IDX, tq), lambda bi, i: (bi, 0, i)),
                  _resident_spec(kp, 2 * D_IDX, 0),
                  _kv_spec(kp, kv_col),
                  pl.BlockSpec((1, n_kblk, BRANCH_W, tk), lambda bi, i: (bi, 0, 0, 0), pipeline_mode=pl.Buffered(1)),
                  pl.BlockSpec((tk, tk), lambda bi, i: (0, 0))],
        out_specs=pl.BlockSpec((1, tq, BRANCH_W), lambda bi, i: (bi, i, 0)),
        out_shape=jax.ShapeDtypeStruct((b, l_pad, BRANCH_W), BF),
        scratch_shapes=[pltpu.VMEM((n_blk_alloc, tk, tq), jnp.int32),
                        pltpu.VMEM((32, n_grp, SUBLANES, tq), jnp.int32), pltpu.VMEM((n_grp, SUBLANES, tq), jnp.int32)],
        compiler_params=_cparams("parallel", "parallel"), name="mixer_c",
    )(q_all, iq, jnp.moveaxis(iw, 1, 2), ik_arr, k_arr, vt, jnp.asarray(lower, BF))[:, :l]


def _mixer_d_kernel(q_ref, k_ref, v_ref, fq_ref, fk_ref, kbound_ref, fbound_ref, o_ref,
                    *, tq, tk, q_start, n_kblk, step):
    q0 = q_start + pl.program_id(1) * tq
    nblk = (q0 + tq - 1) // tk + 1
    n_full = (q0 + 1) // tk
    qpos = q0 + lax.broadcasted_iota(jnp.int32, (tq, tk), 0)
    kiota = lax.broadcasted_iota(jnp.int32, (tq, tk), 1)
    qm = _masked_pair_heads(q_ref)
    fq = [fq_ref[0, :, h:h + 1] for h in range(N_HEADS)]
    low = _low_half((tq, LANES))

    slabs = [slice(pr * LANES, (pr + 1) * LANES) for pr in range(N_HEADS // 2)]

    def body(t, carry, masked, first, nsub, sign):
        blocks = [first + sign * (t * nsub + u) for u in range(nsub)]
        starts = [pl.multiple_of(j * tk, tk) for j in blocks]
        raw = [[_dot_nt(qm[h], k_ref[0, pl.ds(ks, tk), slabs[h // 2]]) for h in range(N_HEADS)] for ks in starts]
        logits = []
        for u in range(nsub):
            row = [raw[u][h] + fq[h] - fk_ref[0, pl.ds(h * n_kblk + blocks[u], 1), :] for h in range(N_HEADS)]
            if masked:
                visible = (starts[u] + kiota) <= qpos
                row = [jnp.where(visible, s, NEG) for s in row]
            logits.append(row)
        return _softmax_step(carry, logits, lambda u, pr: v_ref[0, pl.ds(starts[u], tk), slabs[pr]], low)

    q_norm = [1.001 * jnp.sqrt(jnp.sum(jnp.square(qm[h].astype(F32)), axis=1, keepdims=True)) for h in range(N_HEADS)]
    bi = pl.program_id(0)

    def live(carry, j):
        gap = [q_norm[h] * kbound_ref[bi, h * n_kblk + j] + fq[h] + fbound_ref[bi, h * n_kblk + j] - carry[0][h]
               for h in range(N_HEADS)]
        return jnp.max(functools.reduce(jnp.maximum, gap)) > -(F32_UNDERFLOW_LOG2 + 8.0)

    carry = lax.fori_loop(0, nblk - n_full, functools.partial(body, masked=True, first=n_full, nsub=1, sign=1),
                          _softmax_init(tq))
    group = functools.partial(body, masked=False, first=n_full - 1, nsub=step, sign=-1)
    t_end, carry = lax.while_loop(
        lambda c: (c[0] < n_full // step) & live(c[1], jnp.maximum(n_full - 1 - c[0] * step, 0)),
        lambda c: (c[0] + 1, group(c[0], c[1])), (jnp.int32(0), carry))
    n_rest = n_full % step
    rest = jnp.where((t_end == n_full // step) & live(carry, jnp.maximum(n_rest - 1, 0)), n_rest, 0)
    carry = lax.fori_loop(0, rest, functools.partial(body, masked=False, first=n_rest - 1, nsub=1, sign=-1), carry)
    _softmax_finish(carry, o_ref, low)


def _mixer_d(q_all, k_arr, v_arr, kv_col, fq, fk, k_sq, tq, tk, q_start, step):
    b, l, _ = q_all.shape
    kp = k_arr.shape[1]
    n_kblk = kp // tk
    k_norm = 1.001 * jnp.sqrt(jnp.max(k_sq.reshape(b, n_kblk, tk, N_HEADS), axis=2))
    kbound = jnp.moveaxis(lax.cummax(k_norm, axis=1), 1, 2).reshape(b, N_HEADS * n_kblk)
    fbound = lax.cummax(jnp.max(-fk, axis=-1).reshape(b, N_HEADS, n_kblk), axis=2).reshape(b, N_HEADS * n_kblk)
    smem = pl.BlockSpec(memory_space=pltpu.SMEM)
    return pl.pallas_call(
        functools.partial(_mixer_d_kernel, tq=tq, tk=tk, q_start=q_start, n_kblk=n_kblk, step=step),
        grid=(b, l // tq),
        in_specs=[pl.BlockSpec((1, tq, BRANCH_W), lambda bi, i: (bi, i, 3)), _kv_spec(kp, kv_col),
                  _kv_spec(kp, kv_col),
                  pl.BlockSpec((1, tq, N_HEADS), lambda bi, i: (bi, i, 0)),
                  pl.BlockSpec((1, N_HEADS * n_kblk, tk), lambda bi, i: (bi, 0, 0)), smem, smem],
        out_specs=pl.BlockSpec((1, tq, BRANCH_W), lambda bi, i: (bi, i, 0)),
        out_shape=jax.ShapeDtypeStruct((b, l, BRANCH_W), BF),
        compiler_params=_cparams("parallel", "parallel"), name="mixer_d",
    )(q_all, k_arr, v_arr, fq, fk, kbound, fbound)


def _merge_kernel(x_ref, g1_ref, oa_ref, ob_ref, oc_ref, od_ref, wg_ref, wb_ref, wo_ref, y_ref):
    x = x_ref[...]
    h = _rms(x, g1_ref[...]).astype(BF)
    merged = None
    for g, o_ref in enumerate((oa_ref, ob_ref, oc_ref, od_ref)):
        gate = 1.0 / (1.0 + jnp.exp(-_dot(h, wg_ref[:, g * D_MODEL:(g + 1) * D_MODEL])))
        term = gate * _dot(o_ref[...], wb_ref[g])
        merged = term if merged is None else merged + term
    y_ref[...] = x + _dot(merged.astype(BF), wo_ref[...])


def _merge(x2, outs, lp, tm):
    m_rows = x2.shape[0]
    row = lambda i: (i, 0)
    const = lambda i: (0, 0)
    return pl.pallas_call(
        _merge_kernel, grid=(m_rows // tm,),
        in_specs=[pl.BlockSpec((tm, D_MODEL), row), pl.BlockSpec((1, D_MODEL), const)]
                 + [pl.BlockSpec((tm, BRANCH_W), row)] * N_MIXERS
                 + [pl.BlockSpec((D_MODEL, N_MIXERS * D_MODEL), const),
                    pl.BlockSpec((N_MIXERS, BRANCH_W, D_MODEL), lambda i: (0, 0, 0)),
                    pl.BlockSpec((D_MODEL, D_MODEL), const)],
        out_specs=pl.BlockSpec((tm, D_MODEL), row),
        out_shape=jax.ShapeDtypeStruct((m_rows, D_MODEL), F32),
        compiler_params=_cparams("parallel"), name="merge",
    )(x2, lp["g1"], *outs, lp["w_gate"], lp["w_branch"], lp["w_o"])


def _gelu_tanh(x):
    return x * (0.5 * (1.0 + jnp.tanh(np.sqrt(2.0 / np.pi).astype(np.float32) * (x + 0.044715 * (x * x * x)))))


def _ffn_kernel(x_ref, xp_ref, st_ref, g2_ref, win_ref, wc_ref, bc_ref, wd_ref, y_ref, ut_ref, h_ref, a_ref,
                *, tm, tf, tiles_per_seq):
    first_tile = pl.program_id(0) % tiles_per_seq == 0
    g2 = g2_ref[...]
    x = x_ref[...]
    h_ref[:CTX_ROWS, :] = _rms(xp_ref[...], g2).astype(BF)
    h_ref[CTX_ROWS:, :] = _rms(x, g2).astype(BF)
    h = h_ref[...]
    for f in range(0, D_FF, tf):
        u = _dot(h, win_ref[:, f:f + tf])
        gt = _dot(h, win_ref[:, D_FF + f:D_FF + f + tf])[CTX_ROWS:]
        ctx = jnp.where(first_tile, st_ref[0, :, f:f + tf], u[:CTX_ROWS])
        ue = jnp.concatenate([ctx, u[CTX_ROWS:]], axis=0)
        wc = wc_ref[:, f:f + tf]
        conv = bc_ref[:, f:f + tf] + ((ue[CTX_ROWS - 2:CTX_ROWS - 2 + tm] * wc[0:1]
                                       + ue[CTX_ROWS - 1:CTX_ROWS - 1 + tm] * wc[1:2]) + ue[CTX_ROWS:] * wc[2:3])
        a_ref[:, f:f + tf] = (_gelu_tanh(conv) * gt).astype(BF)
        ut_ref[0, :, f:f + tf] = ue[tm:]
    y_ref[...] = x + _dot(a_ref[...], wd_ref[...])


def _ffn(x2, state8, lp, tm, tf, seq_len):
    m_rows = x2.shape[0]
    tiles_per_seq = seq_len // tm
    const = lambda i: (0, 0)
    once = pl.Buffered(1)
    return pl.pallas_call(
        functools.partial(_ffn_kernel, tm=tm, tf=tf, tiles_per_seq=tiles_per_seq),
        grid=(m_rows // tm,),
        in_specs=[pl.BlockSpec((tm, D_MODEL), lambda i: (i, 0)),
                  pl.BlockSpec((CTX_ROWS, D_MODEL), lambda i: (jnp.maximum(i * (tm // CTX_ROWS) - 1, 0), 0)),
                  pl.BlockSpec((1, CTX_ROWS, D_FF), lambda i: (i // tiles_per_seq, 0, 0)),
                  pl.BlockSpec((1, D_MODEL), const),
                  pl.BlockSpec((D_MODEL, 2 * D_FF), const, pipeline_mode=once),
                  pl.BlockSpec((CONV_W, D_FF), const),
                  pl.BlockSpec((1, D_FF), const),
                  pl.BlockSpec((D_FF, D_MODEL), const, pipeline_mode=once)],
        out_specs=[pl.BlockSpec((tm, D_MODEL), lambda i: (i, 0)),
                   pl.BlockSpec((1, CTX_ROWS, D_FF), lambda i: (i, 0, 0))],
        out_shape=[jax.ShapeDtypeStruct((m_rows, D_MODEL), F32),
                   jax.ShapeDtypeStruct((m_rows // tm, CTX_ROWS, D_FF), F32)],
        scratch_shapes=[pltpu.VMEM((tm + CTX_ROWS, D_MODEL), BF), pltpu.VMEM((tm, D_FF), BF)],
        compiler_params=_cparams("parallel"), name="ffn",
    )(x2, x2, state8, lp["g2"], lp["w_ffn_in"], lp["w_conv"], lp["b_conv"], lp["w_down"])


def _rope_tables(q_start, length):
    half = HEAD_DIM // 2
    inv = ROPE_THETA ** (-jnp.arange(half, dtype=F32) / half)
    ang = (q_start + jnp.arange(length)).astype(F32)[:, None] * inv[None, :]
    cos = jnp.cos(ang)
    sin = jnp.sin(ang)
    cos_h = jnp.concatenate([cos, cos], axis=1)
    sin_h = jnp.concatenate([-sin, sin], axis=1)
    n_rot = H_IDX + 1
    pad = MISC_W - n_rot * HEAD_DIM
    return {
        "cq": jnp.tile(cos_h, (1, N_HEADS)), "sq": jnp.tile(sin_h, (1, N_HEADS)),
        "cm": jnp.concatenate([jnp.tile(cos_h, (1, n_rot)), jnp.ones((length, pad), F32)], axis=1),
        "sm": jnp.concatenate([jnp.tile(sin_h, (1, n_rot)), jnp.zeros((length, pad), F32)], axis=1),
    }


def _layer_params(norm1_g, w_in, b_forget, qk_g, rel_table, w_branch, w_o, norm2_g, w_ffn_in, w_conv, b_conv,
                  w_down):
    a = 3 * QKV_W
    i1 = a + H_IDX * D_IDX + D_IDX + H_IDX
    d1 = i1 + QKV_W
    f1 = d1 + N_HEADS
    n_misc = (i1 - a) + N_HEADS
    w_misc = jnp.concatenate([w_in[:, a:i1], w_in[:, d1:f1], jnp.zeros((D_MODEL, MISC_W - n_misc), w_in.dtype)],
                             axis=1)
    f_lane = D_IDX + H_IDX
    bf_row = jnp.zeros((1, LANES), F32).at[0, f_lane:f_lane + N_HEADS].set(b_forget.astype(F32))
    head_ones = (np.arange(BRANCH_W)[:, None] // HEAD_DIM == np.arange(BRANCH_W)[None, :] // HEAD_DIM)
    return {
        "g1": norm1_g.astype(F32)[None, :], "g2": norm2_g.astype(F32)[None, :],
        "w_qkv": jnp.concatenate([w_in[:, :a], w_in[:, i1:d1]], axis=1).astype(BF),
        "w_misc": w_misc.astype(BF), "w_gate": w_in[:, f1:].astype(BF),
        "qkg": jnp.tile(qk_g.astype(F32), (1, N_HEADS)), "bf_row": bf_row,
        "head_ones": jnp.asarray(head_ones.astype(np.float32), BF),
        "rel_table": rel_table, "w_branch": w_branch.astype(BF), "w_o": w_o.astype(BF),
        "w_ffn_in": w_ffn_in.astype(BF), "w_conv": w_conv.astype(F32), "b_conv": b_conv.astype(F32)[None, :],
        "w_down": w_down.astype(BF),
    }


def _pad_rows(a, rows):
    return jnp.pad(a, ((0, 0), (0, rows - a.shape[1])) + ((0, 0),) * (a.ndim - 2))


def _round_up(n, m):
    return -(-n // m) * m


def _layer(x, past, lp, tabs):
    b, l, _ = x.shape
    m_rows = b * l
    q_start = 0 if past is None else past[2].shape[1]
    n_keys = q_start + l
    tq = min(256, l)
    tq_c = LANES
    tk = 256
    tk_d = 512
    kp = _round_up(n_keys, max(tk, tk_d))
    tm = min(256, m_rows)

    x2 = x.reshape(m_rows, D_MODEL)
    (q_all, kbf, vbf, ka, va, kb, vb, kc, vc, kd, vd, iq, misc) = _projection(x2, lp, tabs, tm)
    ik = misc[:, :D_IDX].reshape(b, l, D_IDX)
    iw = misc[:, D_IDX:D_IDX + H_IDX].reshape(b, l, H_IDX)
    logf = misc[:, D_IDX + H_IDX:D_IDX + H_IDX + N_HEADS].reshape(b, l, N_HEADS)
    q_all = q_all.reshape(b, l, N_MIXERS * BRANCH_W)
    iq = iq.reshape(b, l, H_IDX * D_IDX)
    heads = lambda t: t.reshape(b, l, N_HEADS, HEAD_DIM)
    flat = lambda t: t.reshape(t.shape[0], t.shape[1], BRANCH_W)

    if past is None:
        kbf3 = kbf.reshape(b, l, N_MIXERS * BRANCH_W)
        vbf3 = vbf.reshape(b, l, N_MIXERS * BRANCH_W)
        kv = [(kbf3, vbf3, c) for c in range(N_MIXERS)]
        ik_all = ik.astype(BF)
        logf_all = logf
        a_args = (kbf3, vbf3, 0, _band_bias(lp["rel_table"], tq, BAND_PAST + tq, BAND_PAST + tq), tq, 3, tq)
        conv_state = jnp.zeros((b, CONV_W - 1, D_FF), F32)
    else:
        (pa_k, pa_v, pb_k, pb_v, pc_k, pc_v, pc_ki, pd_k, pd_v, pd_f, conv_state) = past
        new_bf = lambda t, c: t.reshape(b, l, N_MIXERS, BRANCH_W)[:, :, c]
        cat = lambda p, c, src, rows: _pad_rows(jnp.concatenate([flat(p).astype(BF), new_bf(src, c)], axis=1), rows)
        kv = [None] + [(cat(pk, c, kbf, kp), cat(pv, c, vbf, kp), 0)
                       for c, (pk, pv) in ((1, (pb_k, pb_v)), (2, (pc_k, pc_v)), (3, (pd_k, pd_v)))]
        ik_all = jnp.concatenate([pc_ki.astype(BF), ik.astype(BF)], axis=1)
        logf_all = jnp.concatenate([pd_f.astype(F32), logf], axis=1)
        a_keys = pa_k.shape[1] + l
        wa = _round_up(BAND_PAST + l, LANES)
        a_args = (cat(pa_k, 0, kbf, wa), cat(pa_v, 0, vbf, wa), 0,
                  _band_bias(lp["rel_table"], l, wa, a_keys), l, 1, wa)

    f_cum = jnp.cumsum(logf_all, axis=1) * LOG2E
    fq = f_cum[:, q_start:]
    fk = jnp.moveaxis(_pad_rows(f_cum, kp), 1, 2).reshape(b, N_HEADS * (kp // tk_d), tk_d)

    o_a = _mixer_a(q_all, *a_args)
    o_b = _mixer_b(q_all, *kv[1], tq, tk, q_start)
    ik_pair = _pad_rows(jnp.concatenate([ik_all, ik_all], axis=-1), kp)
    o_c = _mixer_c(q_all, iq, iw, ik_pair, *kv[2], tq_c, tk_d, q_start, n_keys)
    head_of_col = jnp.asarray(np.arange(BRANCH_W)[:, None] // HEAD_DIM == np.arange(N_HEADS)[None, :], F32)
    kd_sq = jnp.dot(jnp.square(kd.astype(BF).astype(F32)), head_of_col,
                    precision=lax.Precision.HIGHEST).reshape(b, l, N_HEADS)
    if past is not None:
        kd_sq = jnp.concatenate([jnp.sum(jnp.square(past[7].astype(BF).astype(F32)), axis=-1), kd_sq], axis=1)
    o_d = _mixer_d(q_all, *kv[3], fq, fk, _pad_rows(kd_sq, kp), tq, tk_d, q_start, 2)
    x2 = _merge(x2, [o.reshape(m_rows, BRANCH_W) for o in (o_a, o_b, o_c, o_d)], lp, tm)

    state8 = jnp.pad(conv_state.astype(F32), ((0, 0), (CTX_ROWS - (CONV_W - 1), 0), (0, 0)))
    x2, u_tail = _ffn(x2, state8, lp, min(512, l), 256, l)
    u_tail = u_tail.reshape(b, -1, CTX_ROWS, D_FF)[:, -1]

    rows3 = lambda t: t.reshape(b, l, BRANCH_W)
    if past is None:
        a_keep = min(BAND_PAST, l)
        new_a = tuple(rows3(t)[:, -a_keep:].reshape(b, a_keep, N_HEADS, HEAD_DIM) for t in (ka, va))
    else:
        a_keep = pa_k.shape[1]
        new_a = (jnp.concatenate([pa_k, heads(ka)], axis=1)[:, -a_keep:],
                 jnp.concatenate([pa_v, heads(va)], axis=1)[:, -a_keep:])
    new = new_a + (rows3(kb), rows3(vb), rows3(kc), rows3(vc), ik, rows3(kd), rows3(vd), logf,
                   u_tail[:, -(CONV_W - 1):])
    return x2.reshape(b, l, D_MODEL), new


def kernel(x_prompt, x_sample, cache_a_k, cache_a_v, cache_b_k, cache_b_v, cache_c_k, cache_c_v, cache_c_kidx, cache_d_k, cache_d_v, cache_d_logf, state_ffn_conv, norm1_g, w_in, b_forget, qk_norm_g, rel_bias, w_branch, w_o, norm2_g, w_ffn_in, w_conv, b_conv, w_down):
    depth = w_in.shape[0]
    past_len = cache_b_k.shape[2]
    tabs_p = _rope_tables(0, x_prompt.shape[1])
    tabs_s = {k: jnp.tile(v, (x_sample.shape[0], 1)) for k, v in _rope_tables(past_len, x_sample.shape[1]).items()}
    y_p, y_s = x_prompt, x_sample
    p_states, s_states = [], []
    for d in range(depth):
        lp = _layer_params(norm1_g[d], w_in[d], b_forget[d], qk_norm_g[d], rel_bias[d], w_branch[d], w_o[d],
                           norm2_g[d], w_ffn_in[d], w_conv[d], b_conv[d], w_down[d])
        y_p, st_p = _layer(y_p, None, lp, tabs_p)
        past = (cache_a_k[d], cache_a_v[d], cache_b_k[d], cache_b_v[d], cache_c_k[d], cache_c_v[d],
                cache_c_kidx[d], cache_d_k[d], cache_d_v[d], cache_d_logf[d], state_ffn_conv[d])
        y_s, st_s = _layer(y_s, past, lp, tabs_s)
        p_states.append(st_p)
        s_states.append(st_s)
    def assemble(states):
        out = [jnp.stack(t) for t in zip(*states)]
        for i in (2, 3, 4, 5, 7, 8):
            out[i] = out[i].reshape(out[i].shape[:-1] + (N_HEADS, HEAD_DIM))
        return out

    return (y_p, y_s, *assemble(p_states), *assemble(s_states))
```
